```python
import math
import jax
import jax.numpy as jnp
from jax import lax
import numpy as np

D_MODEL = 1024
BATCH = 4
SEQ = 4096
DEPTH = 2

D_MIX = D_MODEL
RWKV_HEADS = 4
RWKV_HEAD_DIM = 64
D_RWKV = RWKV_HEADS * RWKV_HEAD_DIM
DECAY_LORA = 64
AAA_LORA = 64
MV_LORA = 32
GATE_LORA = 128
D_RWKV_SHIFT = 3 * D_RWKV + DECAY_LORA + AAA_LORA + GATE_LORA
W_DECAY_SCALE = 0.6065306597126334
GN_EPS = 64e-5
DIFF_HEADS = 4
DIFF_QK_DIM = 64
DIFF_V_DIM = 2 * DIFF_QK_DIM
D_DIFF_QK = DIFF_HEADS * 2 * DIFF_QK_DIM
D_DIFF = DIFF_HEADS * DIFF_V_DIM
ROT_DIM = DIFF_QK_DIM // 4
ROPE_THETA = 500000.0
Q_BLOCK = 128
SUBLN_EPS = 1e-5
POOL_WINDOWS = (2, 4, 8, 16)
POOL_GROUPS = len(POOL_WINDOWS)
POOL_GROUP_DIM = 64
D_POOL = POOL_GROUPS * POOL_GROUP_DIM
P_IN_FIRST = D_RWKV_SHIFT + 2 * D_DIFF_QK + D_DIFF + D_POOL
P_IN_REST = P_IN_FIRST + MV_LORA
D_FF = 3584
N_EXPERTS = 8
TOP_K = 2
NORM_EPS = 1e-6
N_DENSE = (DEPTH + 1) // 2
N_MOE = DEPTH // 2

kernel_name = 'hybrid_rwkv7_diffattn_pool_moe_encoder'


def split_cols(t, sizes):
    offs = np.cumsum([0] + list(sizes))
    return [t[..., int(offs[i]):int(offs[i + 1])] for i in range(len(sizes))]


def rmsnorm(x, g, eps=NORM_EPS):
    xf = x.astype(jnp.float32)
    y = xf * lax.rsqrt(jnp.mean(xf * xf, axis=-1, keepdims=True) + eps)
    return (y * g.astype(jnp.float32)).astype(x.dtype)


def centred_shift(u, mu):
    prev = jnp.pad(u[:, :-1], ((0, 0), (1, 0), (0, 0)))
    nxt = jnp.pad(u[:, 1:], ((0, 0), (0, 1), (0, 0)))
    return u + mu[0] * (prev - u) + mu[1] * (nxt - u)


def wkv7_scan(r, w, k, v, kk, a, reverse):
    bsz, _, h, n = r.shape

    def step(state, inp):
        r_t, w_t, k_t, v_t, kk_t, a_t = inp
        sa = jnp.einsum('bhvk,bhk->bhv', state, -kk_t)
        state = (state * w_t[:, :, None, :]
                 + sa[..., None] * (kk_t * a_t)[:, :, None, :]
                 + v_t[..., None] * k_t[:, :, None, :])
        return state, jnp.einsum('bhvk,bhk->bhv', state, r_t)

    xs = tuple(jnp.swapaxes(t, 0, 1) for t in (r, w, k, v, kk, a))
    s0 = jnp.zeros((bsz, h, n, n), jnp.float32)
    _, y = lax.scan(step, s0, xs, reverse=reverse)
    return jnp.swapaxes(y, 0, 1)


def rwkv7_mixer(u, v_down, v_first, tshift, decay_bias, decay_up, iclr_bias, iclr_up,
                gate_up, k_k, k_a, r_k, lnx_w, lnx_b, vres_bias, vres_up):
    f32 = jnp.float32
    bsz, seq, _ = u.shape
    us = centred_shift(u, tshift).astype(f32)
    r, k, v, xw, xa, xg = split_cols(us, (D_RWKV, D_RWKV, D_RWKV, DECAY_LORA, AAA_LORA, GATE_LORA))
    if v_first is None:
        v_first = v
    else:
        v = v + (v_first - v) * jax.nn.sigmoid(vres_bias + v_down.astype(f32) @ vres_up)
    g = jax.nn.sigmoid(xg) @ gate_up
    hs = (bsz, seq, RWKV_HEADS, RWKV_HEAD_DIM)
    kk = (k * k_k).reshape(hs)
    kk = kk * lax.rsqrt(jnp.maximum(jnp.sum(kk * kk, axis=-1, keepdims=True), 1e-24))
    rh = r.reshape(hs)
    vh = v.reshape(hs)
    ys = []
    bonuses = []
    for d in range(2):
        w = jnp.exp(-W_DECAY_SCALE * jax.nn.sigmoid(decay_bias[d] + jnp.tanh(xw) @ decay_up[d]))
        a = jax.nn.sigmoid(iclr_bias[d] + xa @ iclr_up[d])
        kd = (k * (1.0 + (a - 1.0) * k_a)).reshape(hs)
        ys.append(wkv7_scan(rh, w.reshape(hs), kd, vh, kk, a.reshape(hs), reverse=(d == 1)))
        bonuses.append(jnp.sum(rh * kd * r_k, axis=-1, keepdims=True) * vh)
    y = ys[0] + ys[1]
    mu = jnp.mean(y, axis=-1, keepdims=True)
    var = jnp.mean(jnp.square(y - mu), axis=-1, keepdims=True)
    y = ((y - mu) * lax.rsqrt(var + GN_EPS)).reshape(bsz, seq, D_RWKV) * lnx_w + lnx_b
    out = (y + (bonuses[0] + bonuses[1]).reshape(bsz, seq, D_RWKV)) * g
    return out.astype(u.dtype), v_first


def partial_rope(t, positions):
    half = ROT_DIM // 2
    inv_freq = jnp.power(ROPE_THETA, -(jnp.arange(half, dtype=jnp.float32) * 2.0 / ROT_DIM))
    ang = positions.astype(jnp.float32)[..., None] * inv_freq
    cos = jnp.cos(ang)[:, :, None, None, :]
    sin = jnp.sin(ang)[:, :, None, None, :]
    tf = t[..., :ROT_DIM].astype(jnp.float32)
    t1, t2 = tf[..., :half], tf[..., half:]
    rot = jnp.concatenate([t1 * cos - t2 * sin, t2 * cos + t1 * sin], axis=-1)
    return jnp.concatenate([rot.astype(t.dtype), t[..., ROT_DIM:]], axis=-1)


def diff_attention(q, k, v, positions, lambda_q, lambda_k, subln_w, lam_init):
    f32 = jnp.float32
    bsz, seq, _ = q.shape
    H, Dk, Dv = DIFF_HEADS, DIFF_QK_DIM, DIFF_V_DIM
    q = partial_rope(q.reshape(bsz, seq, H, 2, Dk), positions)
    k = partial_rope(k.reshape(bsz, seq, H, 2, Dk), positions)
    nb = seq // Q_BLOCK
    qb = q.reshape(bsz, nb, Q_BLOCK, H, 2, Dk).transpose(1, 0, 3, 4, 2, 5)
    kh = k.transpose(0, 2, 3, 1, 4)
    vh = v.reshape(bsz, seq, H, Dv).transpose(0, 2, 1, 3)
    lq = lambda_q.astype(f32)
    lk = lambda_k.astype(f32)
    lam = jnp.exp(jnp.sum(lq[0] * lk[0])) - jnp.exp(jnp.sum(lq[1] * lk[1])) + lam_init
    scale = DIFF_QK_DIM ** -0.5

    def block(q_blk):
        s = jnp.einsum('bhmqd,bhmkd->bhmqk', q_blk, kh).astype(f32) * scale
        p = jax.nn.softmax(s, axis=-1)
        attn = p[:, :, 0] - lam * p[:, :, 1]
        return jnp.einsum('bhqk,bhkd->bhqd', attn.astype(vh.dtype), vh)

    o = lax.map(block, qb)
    o = o.transpose(1, 0, 3, 2, 4).reshape(bsz, seq, H, Dv)
    o = rmsnorm(o, subln_w, SUBLN_EPS) * (1.0 - lam_init)
    return o.reshape(bsz, seq, D_DIFF)


def pool_mixer(u, pool_mix, pool_scale):
    f32 = jnp.float32
    bsz, seq, c = u.shape
    uf = u.astype(f32)
    csum = jnp.concatenate([jnp.zeros((bsz, 1, c), f32), jnp.cumsum(uf, axis=1)], axis=1)
    t = jnp.arange(seq)
    outs = []
    for g, w in enumerate(POOL_WINDOWS):
        sl = slice(g * POOL_GROUP_DIM, (g + 1) * POOL_GROUP_DIM)
        cs = csum[..., sl]
        lo = jnp.clip(t - w // 2, 0, seq)
        hi = jnp.clip(t + (w - w // 2), 0, seq)
        cnt = (hi - lo).astype(f32)[None, :, None]
        outs.append((cs[:, hi] - cs[:, lo]) / cnt - uf[..., sl])
    pooled = jnp.stack(outs, axis=2)
    mixed = jnp.einsum('bsgc,gcd->bsgd', pooled, pool_mix).reshape(bsz, seq, c)
    return (mixed * pool_scale).astype(u.dtype)


def swiglu(h, w_gate, w_up, w_down):
    return (jax.nn.silu(h @ w_gate) * (h @ w_up)) @ w_down


def moe_swiglu(h, router, w_gate, w_up, w_down):
    f32 = jnp.float32
    bsz, seq, d = h.shape
    t = h.reshape(-1, d)
    logits = (t @ router).astype(f32)
    top_val, top_idx = lax.top_k(logits, TOP_K)
    gates = jax.nn.softmax(top_val, axis=-1)
    combine = jnp.sum(jax.nn.one_hot(top_idx, N_EXPERTS, dtype=f32) * gates[..., None], axis=1)
    out = jnp.zeros((t.shape[0], d), f32)
    for e in range(N_EXPERTS):
        out = out + combine[:, e:e + 1] * swiglu(t, w_gate[e], w_up[e], w_down[e]).astype(f32)
    return out.reshape(bsz, seq, d).astype(h.dtype)


def setup_inputs(seed: int = 0) -> dict:
    key = jax.random.key(seed)
    ks = iter(jax.random.split(key, 48))
    f32 = jnp.float32
    L = DEPTH

    def nrm(shape, scale):
        return scale * jax.random.normal(next(ks), shape, f32)

    def gain(shape):
        return 1.0 + 0.1 * jax.random.normal(next(ks), shape, f32)

    def unif(shape, lo, hi):
        return jax.random.uniform(next(ks), shape, f32, lo, hi)

    x = jax.random.normal(next(ks), (BATCH, SEQ, D_MODEL), f32)
    offset = jax.random.randint(next(ks), (BATCH, 1), 0, SEQ, dtype=jnp.int32)
    positions = jnp.arange(SEQ, dtype=jnp.int32)[None, :] + offset
    return {
        'x': x,
        'positions': positions,
        'norm_mix': gain((L, D_MODEL)),
        'w_in_first': nrm((D_MODEL, P_IN_FIRST), D_MODEL ** -0.5),
        'w_in_rest': nrm((L - 1, D_MODEL, P_IN_REST), D_MODEL ** -0.5),
        'tshift': unif((L, 2, D_RWKV_SHIFT), 0.0, 0.5),
        'decay_bias': unif((L, 2, D_RWKV), -3.0, 3.0),
        'decay_up': nrm((L, 2, DECAY_LORA, D_RWKV), DECAY_LORA ** -0.5),
        'iclr_bias': nrm((L, 2, D_RWKV), 0.5),
        'iclr_up': nrm((L, 2, AAA_LORA, D_RWKV), AAA_LORA ** -0.5),
        'gate_up': nrm((L, GATE_LORA, D_RWKV), GATE_LORA ** -0.5),
        'k_k': gain((L, D_RWKV)),
        'k_a': gain((L, D_RWKV)),
        'r_k': nrm((L, RWKV_HEADS, RWKV_HEAD_DIM), 0.3),
        'lnx_w': gain((L, D_RWKV)),
        'lnx_b': nrm((L, D_RWKV), 0.02),
        'vres_bias': nrm((L - 1, D_RWKV), 0.5),
        'vres_up': nrm((L - 1, MV_LORA, D_RWKV), MV_LORA ** -0.5),
        'lambda_q': nrm((L, 2, DIFF_QK_DIM), 0.1),
        'lambda_k': nrm((L, 2, DIFF_QK_DIM), 0.1),
        'subln_w': gain((L, DIFF_V_DIM)),
        'pool_mix': nrm((L, POOL_GROUPS, POOL_GROUP_DIM, POOL_GROUP_DIM), POOL_GROUP_DIM ** -0.5),
        'pool_scale': gain((L, D_POOL)),
        'w_out': nrm((L, D_MIX, D_MODEL), D_MIX ** -0.5),
        'norm_ffn': gain((L, D_MODEL)),
        'ffn_gate': nrm((N_DENSE, D_MODEL, D_FF), D_MODEL ** -0.5),
        'ffn_up': nrm((N_DENSE, D_MODEL, D_FF), D_MODEL ** -0.5),
        'ffn_down': nrm((N_DENSE, D_FF, D_MODEL), D_FF ** -0.5),
        'router': nrm((N_MOE, D_MODEL, N_EXPERTS), D_MODEL ** -0.5),
        'exp_gate': nrm((N_MOE, N_EXPERTS, D_MODEL, D_FF), D_MODEL ** -0.5),
        'exp_up': nrm((N_MOE, N_EXPERTS, D_MODEL, D_FF), D_MODEL ** -0.5),
        'exp_down': nrm((N_MOE, N_EXPERTS, D_FF, D_MODEL), D_FF ** -0.5),
        'norm_out': gain((D_MODEL,)),
    }


def reference(x, positions, norm_mix, w_in_first, w_in_rest, tshift, decay_bias, decay_up,
              iclr_bias, iclr_up, gate_up, k_k, k_a, r_k, lnx_w, lnx_b, vres_bias, vres_up,
              lambda_q, lambda_k, subln_w, pool_mix, pool_scale, w_out, norm_ffn,
              ffn_gate, ffn_up, ffn_down, router, exp_gate, exp_up, exp_down, norm_out):
    v_first = None
    for l in range(DEPTH):
        h = rmsnorm(x, norm_mix[l])
        if l == 0:
            proj = h @ w_in_first
            u_rwkv, q, k, v, u_pool = split_cols(proj, (D_RWKV_SHIFT, D_DIFF_QK, D_DIFF_QK, D_DIFF, D_POOL))
            v_down, vb, vu = None, None, None
        else:
            proj = h @ w_in_rest[l - 1]
            u_rwkv, v_down, q, k, v, u_pool = split_cols(
                proj, (D_RWKV_SHIFT, MV_LORA, D_DIFF_QK, D_DIFF_QK, D_DIFF, D_POOL))
            vb, vu = vres_bias[l - 1], vres_up[l - 1]
        y_a, v_first = rwkv7_mixer(u_rwkv, v_down, v_first, tshift[l], decay_bias[l], decay_up[l],
                                   iclr_bias[l], iclr_up[l], gate_up[l], k_k[l], k_a[l], r_k[l],
                                   lnx_w[l], lnx_b[l], vb, vu)
        lam_init = 0.8 - 0.6 * math.exp(-0.3 * l)
        y_b = diff_attention(q, k, v, positions, lambda_q[l], lambda_k[l], subln_w[l], lam_init)
        y_c = pool_mixer(u_pool, pool_mix[l], pool_scale[l])
        x = x + jnp.concatenate([y_a, y_b, y_c], axis=-1) @ w_out[l]
        h = rmsnorm(x, norm_ffn[l])
        if l % 2 == 0:
            i = l // 2
            x = x + swiglu(h, ffn_gate[i], ffn_up[i], ffn_down[i])
        else:
            i = l // 2
            x = x + moe_swiglu(h, router[i], exp_gate[i], exp_up[i], exp_down[i])
    return rmsnorm(x, norm_out)
```

```python
import functools
import math

import jax
import jax.numpy as jnp
from jax import lax
from jax.experimental import pallas as pl
from jax.experimental.pallas import tpu as pltpu

F32 = jnp.float32
BF16 = jnp.bfloat16
HI = lax.Precision.HIGHEST

D_MODEL = 1024
RWKV_HEADS = 4
HEAD_DIM = 64
D_RWKV = RWKV_HEADS * HEAD_DIM
DECAY_LORA = 64
AAA_LORA = 64
MV_LORA = 32
GATE_LORA = 128
D_SHIFT = 3 * D_RWKV + DECAY_LORA + AAA_LORA + GATE_LORA
W_DECAY_SCALE = 0.6065306597126334
GN_EPS = 64e-5
DIFF_HEADS = 4
DIFF_QK = 64
DIFF_V = 128
D_DIFF_QK = DIFF_HEADS * 2 * DIFF_QK
D_DIFF = DIFF_HEADS * DIFF_V
ROT_DIM = DIFF_QK // 4
ROPE_THETA = 500000.0
SUBLN_EPS = 1e-5
POOL_WINDOWS = (2, 4, 8, 16)
D_POOL = 256
D_FF = 3584
N_EXPERTS = 8
NORM_EPS = 1e-6
P_IN = D_SHIFT + 2 * D_DIFF_QK + D_DIFF + D_POOL
COL_Q, COL_K, COL_V, COL_POOL, COL_VDOWN = 1024, 1536, 2048, 2560, 2816

LANES = 128
HALO = 8
VMEM_LIMIT = 48 * 1024 * 1024

SCAN_CHUNK = 128


def _cparams(sem):
    return pltpu.CompilerParams(dimension_semantics=sem, vmem_limit_bytes=VMEM_LIMIT)


def _sigmoid(x):
    return 1.0 / (1.0 + jnp.exp(-x))


def _dot(a, b, **kw):
    return jnp.dot(a, b, preferred_element_type=F32, **kw)


def _bdot(a, b):
    return jnp.dot(a.astype(BF16), b.astype(BF16), preferred_element_type=F32)


def _rms(x, g):
    return x * lax.rsqrt(jnp.mean(x * x, axis=-1, keepdims=True) + NORM_EPS) * g


def _norm_matmul_kernel(x_ref, g_ref, w_ref, o_ref):
    h = _rms(x_ref[...], g_ref[...]).astype(BF16)
    o_ref[...] = _dot(h, w_ref[...])


def norm_matmul(x2d, g, w_bf16, tm=512):
    t, d = x2d.shape
    n = w_bf16.shape[1]
    return pl.pallas_call(
        _norm_matmul_kernel,
        grid=(t // tm,),
        in_specs=[
            pl.BlockSpec((tm, d), lambda i: (i, 0)),
            pl.BlockSpec((1, d), lambda i: (0, 0)),
            pl.BlockSpec((d, n), lambda i: (0, 0)),
        ],
        out_specs=pl.BlockSpec((tm, n), lambda i: (i, 0)),
        out_shape=jax.ShapeDtypeStruct((t, n), F32),
        compiler_params=_cparams(("parallel",)),
        name="norm_matmul",
    )(x2d, g.reshape(1, d), w_bf16)


def _shifted(u, prev_row, next_row):
    ts = u.shape[0]
    row = lax.broadcasted_iota(jnp.int32, u.shape, 0)
    prev = jnp.where(row == 0, prev_row, pltpu.roll(u, 1, 0))
    nxt = jnp.where(row == ts - 1, next_row, pltpu.roll(u, ts - 1, 0))
    return prev, nxt


def _rwkv_prep_kernel(has_vres, *refs):
    if has_vres:
        (u_ref, up_ref, un_ref, vd_ref, vf_ref, vb_ref, vu_ref, mu_ref, wl_ref, bias_ref, gu_ref,
         kkw_ref, ka_ref, rk_ref, bd_ref,
         r_ref, v_ref, kk_ref, lw_ref, kd_ref, bb_ref, g_ref, bonus_ref) = refs
    else:
        (u_ref, up_ref, un_ref, mu_ref, wl_ref, bias_ref, gu_ref,
         kkw_ref, ka_ref, rk_ref, bd_ref,
         r_ref, v_ref, kk_ref, lw_ref, kd_ref, bb_ref, g_ref, bonus_ref) = refs
    i = pl.program_id(1)
    n = pl.num_programs(1)
    u = u_ref[0]
    prev_row = jnp.where(i > 0, up_ref[0, HALO - 1:HALO, :], 0.0)
    next_row = jnp.where(i < n - 1, un_ref[0, 0:1, :], 0.0)
    prev, nxt = _shifted(u, prev_row, next_row)
    us = u + mu_ref[0:1, :] * (prev - u) + mu_ref[1:2, :] * (nxt - u)
    r = us[:, 0:256]
    k = us[:, 256:512]
    v = us[:, 512:768]
    xwa = us[:, 768:896]
    xg = us[:, 896:1024]
    lhs = jnp.concatenate([jnp.tanh(xwa), xwa], axis=1)
    z = _dot(lhs, wl_ref[...], precision=HI) + bias_ref[...]
    if has_vres:
        mix = _sigmoid(vb_ref[...] + _dot(vd_ref[0], vu_ref[...], precision=HI))
        v = v + (vf_ref[0] - v) * mix
    g = _dot(_sigmoid(xg), gu_ref[...], precision=HI)
    bd = bd_ref[...]
    kkr = k * kkw_ref[...]
    ss = _dot(kkr * kkr, bd, precision=HI)
    kk = kkr * lax.rsqrt(jnp.maximum(ss, 1e-24))
    r_ref[0] = r
    v_ref[0] = v
    kk_ref[0] = kk
    g_ref[0] = g
    bonus = jnp.zeros_like(v)
    for d in range(2):
        lw = -W_DECAY_SCALE * _sigmoid(z[:, 256 * d:256 * d + 256])
        a = _sigmoid(z[:, 512 + 256 * d:768 + 256 * d])
        kd = k * (1.0 + (a - 1.0) * ka_ref[...])
        lw_ref[d, 0] = lw
        kd_ref[d, 0] = kd
        bb_ref[d, 0] = kk * a
        bonus = bonus + _dot(r * kd * rk_ref[...], bd, precision=HI) * v
    bonus_ref[0] = bonus


def rwkv_prep(proj, v_first, p, ts=512):
    b, s, _ = proj.shape
    has_vres = v_first is not None
    nblk = ts // HALO
    last = s // HALO - 1
    row = lambda shape: pl.BlockSpec(shape, lambda bi, i: (0,) * len(shape))
    in_specs = [
        pl.BlockSpec((1, ts, D_SHIFT), lambda bi, i: (bi, i, 0)),
        pl.BlockSpec((1, HALO, D_SHIFT), lambda bi, i: (bi, jnp.maximum(i * nblk - 1, 0), 0)),
        pl.BlockSpec((1, HALO, D_SHIFT), lambda bi, i: (bi, jnp.minimum((i + 1) * nblk, last), 0)),
    ]
    args = [proj, proj, proj]
    if has_vres:
        in_specs += [
            pl.BlockSpec((1, ts, LANES), lambda bi, i: (bi, i, COL_VDOWN // LANES)),
            pl.BlockSpec((1, ts, D_RWKV), lambda bi, i: (bi, i, 0)),
            row((1, D_RWKV)), row((LANES, D_RWKV)),
        ]
        args += [proj, v_first, p["vres_bias"], p["vres_up"]]
    in_specs += [row((2, D_SHIFT)), row((256, 1024)), row((1, 1024)), row((GATE_LORA, D_RWKV)),
                 row((1, D_RWKV)), row((1, D_RWKV)), row((1, D_RWKV)), row((D_RWKV, D_RWKV))]
    args += [p["tshift"], p["w_lora"], p["b_lora"], p["gate_up"], p["k_k"], p["k_a"], p["r_k"], p["bd"]]
    one = pl.BlockSpec((1, ts, D_RWKV), lambda bi, i: (bi, i, 0))
    two = pl.BlockSpec((2, 1, ts, D_RWKV), lambda bi, i: (0, bi, i, 0))
    s1 = jax.ShapeDtypeStruct((b, s, D_RWKV), F32)
    s2 = jax.ShapeDtypeStruct((2, b, s, D_RWKV), F32)
    return pl.pallas_call(
        functools.partial(_rwkv_prep_kernel, has_vres),
        grid=(b, s // ts),
        in_specs=in_specs,
        out_specs=[one, one, one, two, two, two, one, one],
        out_shape=[s1, s1, s1, s2, s2, s2, s1, s1],
        compiler_params=_cparams(("parallel", "parallel")),
        name="rwkv_prep",
    )(*args)


def _unit_lower_inverse(m, eye):
    x = eye - m
    p = m
    for _ in range(int(math.log2(m.shape[0])) - 1):
        p = _bdot(p, p)
        x = x + _bdot(x, p)
    return x


def _wkv_kernel(r_ref, v_ref, kk_ref, lw_ref, lwt_ref, kdt_ref, bbt_ref, y_ref, h_ref):
    c = SCAN_CHUNK
    rev = pl.program_id(1) == 1

    @pl.when(pl.program_id(2) == 0)
    def _():
        h_ref[...] = jnp.zeros_like(h_ref)

    ti = lax.broadcasted_iota(jnp.int32, (c, c), 0)
    tj = lax.broadcasted_iota(jnp.int32, (c, c), 1)
    delta = jnp.where(rev, ti - tj, tj - ti)
    strict = delta < 0
    incl = delta <= 0
    before = delta >= 0
    eye = (ti == tj).astype(F32)
    tri = incl.astype(F32)
    tri_t = jnp.concatenate([before.astype(F32), jnp.ones((c, LANES), F32)], axis=1)
    lane = lax.broadcasted_iota(jnp.int32, (c, LANES), 1)
    head_mask = (lane < HEAD_DIM, lane >= HEAD_DIM)
    bi = lax.broadcasted_iota(jnp.int32, (LANES, LANES), 0) // HEAD_DIM
    bj = lax.broadcasted_iota(jnp.int32, (LANES, LANES), 1) // HEAD_DIM
    same_head = bi == bj

    for p in range(2):
        sl = slice(LANES * p, LANES * p + LANES)
        lw = lw_ref[0, 0, :, sl]
        cum = _dot(tri, lw, precision=HI)
        a = kk_ref[0, :, sl] * jnp.exp(cum - lw)
        rt = r_ref[0, :, sl] * jnp.exp(cum)
        cum_t = _dot(lwt_ref[0, 0, sl, :], tri_t, precision=HI)
        einv = jnp.exp(-cum_t[:, :c])
        kt_t = kdt_ref[0, 0, sl, :] * einv
        bt_t = bbt_ref[0, 0, sl, :] * einv
        g_all = jnp.exp(cum_t[:, c:])
        v = v_ref[0, :, sl]
        h = h_ref[p]
        rhs = jnp.concatenate([bt_t, kt_t], axis=1).astype(BF16)
        t_inv, m_ak, l_o = [], [], []
        for hh in range(2):
            lhs = jnp.concatenate([jnp.where(head_mask[hh], a, 0.0), jnp.where(head_mask[hh], rt, 0.0)], axis=0)
            gm = _dot(lhs.astype(BF16), rhs)
            t_inv.append(_unit_lower_inverse(jnp.where(strict, gm[:c, :c], 0.0), eye))
            m_ak.append(jnp.where(strict, gm[:c, c:], 0.0))
            l_o.append(jnp.concatenate([jnp.where(incl, gm[c:, c:], 0.0), -jnp.where(incl, gm[c:, :c], 0.0)], axis=1))
        xr = _bdot(jnp.concatenate([a, rt], axis=0), h)
        vb = v.astype(BF16)
        z = xr[:c] + jnp.where(head_mask[0], _dot(m_ak[0].astype(BF16), vb), _dot(m_ak[1].astype(BF16), vb))
        zb = z.astype(BF16)
        u = jnp.where(head_mask[0], _dot(t_inv[0].astype(BF16), zb), _dot(t_inv[1].astype(BF16), zb))
        vu = jnp.concatenate([v, u], axis=0).astype(BF16)
        o = xr[c:] + jnp.where(head_mask[0], _dot(l_o[0].astype(BF16), vu), _dot(l_o[1].astype(BF16), vu))
        upd = _dot(jnp.concatenate([kt_t, -bt_t], axis=1).astype(BF16), vu)
        h_ref[p] = g_all * (h + jnp.where(same_head, upd, 0.0))
        y_ref[0, 0, :, sl] = o


def wkv_scan(r, v, kk, lw, lwt, kdt, bbt):
    b, s, _ = r.shape
    c = SCAN_CHUNK
    nc = s // c
    cidx = lambda d, ci: d * (nc - 1 - ci) + (1 - d) * ci
    row = pl.BlockSpec((1, c, D_RWKV), lambda bi, d, ci: (bi, cidx(d, ci), 0))
    rowd = pl.BlockSpec((1, 1, c, D_RWKV), lambda bi, d, ci: (d, bi, cidx(d, ci), 0))
    col = pl.BlockSpec((1, 1, D_RWKV, c), lambda bi, d, ci: (d, bi, 0, cidx(d, ci)))
    return pl.pallas_call(
        _wkv_kernel,
        grid=(b, 2, nc),
        in_specs=[row, row, row, rowd, col, col, col],
        out_specs=rowd,
        out_shape=jax.ShapeDtypeStruct((2, b, s, D_RWKV), F32),
        scratch_shapes=[pltpu.VMEM((2, LANES, LANES), F32)],
        compiler_params=_cparams(("parallel", "parallel", "arbitrary")),
        name="wkv_scan",
    )(r, v, kk, lw, lwt, kdt, bbt)


def _attn_prep_kernel(q_ref, k_ref, v_ref, c_ref, s1_ref, s2_ref, qa_ref, qb_ref, ko_ref, vo_ref):
    cs, s1, s2 = c_ref[0], s1_ref[0], s2_ref[0]
    lane = lax.broadcasted_iota(jnp.int32, cs.shape, 1)
    first = lane < DIFF_QK
    scale = DIFF_QK ** -0.5
    for j in range(D_DIFF_QK // LANES):
        sl = slice(LANES * j, LANES * j + LANES)

        def rope(t):
            return t * cs + pltpu.roll(t, ROT_DIM // 2, 1) * s1 + pltpu.roll(t, LANES - ROT_DIM // 2, 1) * s2

        q = rope(q_ref[0, :, sl]) * scale
        qa_ref[0, :, sl] = jnp.where(first, q, 0.0).astype(BF16)
        qb_ref[0, :, sl] = jnp.where(first, 0.0, q).astype(BF16)
        ko_ref[0, :, sl] = rope(k_ref[0, :, sl]).astype(BF16)
    vo_ref[0] = v_ref[0].astype(BF16)


def attn_prep(proj, cs, s1, s2, ts=512):
    b, s, _ = proj.shape
    col = lambda off: pl.BlockSpec((1, ts, 512), lambda bi, i: (bi, i, off // 512))
    tab = pl.BlockSpec((1, ts, LANES), lambda bi, i: (bi, i, 0))
    out = pl.BlockSpec((1, ts, 512), lambda bi, i: (bi, i, 0))
    sh = jax.ShapeDtypeStruct((b, s, 512), BF16)
    return pl.pallas_call(
        _attn_prep_kernel,
        grid=(b, s // ts),
        in_specs=[col(COL_Q), col(COL_K), col(COL_V), tab, tab, tab],
        out_specs=[out, out, out, out],
        out_shape=[sh, sh, sh, sh],
        compiler_params=_cparams(("parallel", "parallel")),
        name="attn_prep",
    )(proj, proj, proj, cs, s1, s2)


def _attn_kernel(lam_init, qa_ref, qb_ref, k_ref, v_ref, lq_ref, lk_ref, sw_ref, o_ref,
                 m0_ref, l0_ref, a0_ref, m1_ref, l1_ref, a1_ref):
    j = pl.program_id(3)

    @pl.when(j == 0)
    def _():
        for m_ref, l_ref, a_ref in ((m0_ref, l0_ref, a0_ref), (m1_ref, l1_ref, a1_ref)):
            m_ref[...] = jnp.full_like(m_ref, -jnp.inf)
            l_ref[...] = jnp.zeros_like(l_ref)
            a_ref[...] = jnp.zeros_like(a_ref)

    kt = k_ref[0]
    vt = v_ref[0]
    reps = kt.shape[0] // LANES
    for q_ref, m_ref, l_ref, a_ref in ((qa_ref, m0_ref, l0_ref, a0_ref), (qb_ref, m1_ref, l1_ref, a1_ref)):
        s = lax.dot_general(q_ref[0], kt, (((1,), (1,)), ((), ())), preferred_element_type=F32)
        m_prev = m_ref[...]
        m_new = jnp.maximum(m_prev, jnp.max(s, axis=1, keepdims=True))
        alpha = jnp.exp(m_prev - m_new)
        p = jnp.exp(s - jnp.concatenate([m_new] * reps, axis=1))
        l_ref[...] = alpha * l_ref[...] + jnp.sum(p, axis=1, keepdims=True)
        a_ref[...] = alpha * a_ref[...] + _dot(p.astype(BF16), vt)
        m_ref[...] = m_new

    @pl.when(j == pl.num_programs(3) - 1)
    def _():
        e = jnp.exp(jnp.sum(lq_ref[...] * lk_ref[...], axis=1, keepdims=True))
        lam = e[0:1] - e[1:2] + lam_init
        o = a0_ref[...] / l0_ref[...] - lam * (a1_ref[...] / l1_ref[...])
        o = o * lax.rsqrt(jnp.mean(o * o, axis=1, keepdims=True) + SUBLN_EPS) * sw_ref[...]
        o_ref[0] = o * (1.0 - lam_init)


def diff_attention(qa, qb, k, v, lq, lk, sw, lam_init, tq=512, tk=512):
    b, s, _ = qa.shape
    qspec = pl.BlockSpec((1, tq, LANES), lambda bi, h, i, j: (bi, i, h))
    kspec = pl.BlockSpec((1, tk, LANES), lambda bi, h, i, j: (bi, j, h))
    small = lambda shape: pl.BlockSpec(shape, lambda bi, h, i, j: (0, 0))
    acc = pltpu.VMEM((tq, LANES), F32)
    return pl.pallas_call(
        functools.partial(_attn_kernel, lam_init),
        grid=(b, DIFF_HEADS, s // tq, s // tk),
        in_specs=[qspec, qspec, kspec, kspec, small((2, DIFF_QK)), small((2, DIFF_QK)), small((1, DIFF_V))],
        out_specs=pl.BlockSpec((1, tq, LANES), lambda bi, h, i, j: (bi, i, h)),
        out_shape=jax.ShapeDtypeStruct((b, s, D_DIFF), F32),
        scratch_shapes=[acc] * 6,
        compiler_params=_cparams(("parallel", "parallel", "parallel", "arbitrary")),
        name="diff_attention",
    )(qa, qb, k, v, lq, lk, sw.reshape(1, DIFF_V))


def _pool_kernel(seq, u_ref, up_ref, un_ref, mix_ref, scale_ref, o_ref):
    i = pl.program_id(1)
    n = pl.num_programs(1)
    u = u_ref[0]
    ts = u.shape[0]
    prev = jnp.where(i > 0, up_ref[0], 0.0)
    nxt = jnp.where(i < n - 1, un_ref[0], 0.0)
    ext = jnp.concatenate([prev, u, nxt], axis=0)
    rows = ext.shape[0]
    back = lambda t, sft: pltpu.roll(t, sft, 0)
    fwd = lambda t, sft: pltpu.roll(t, rows - sft, 0)
    w2 = ext + back(ext, 1)
    w4 = back(w2, 1) + fwd(w2, 1)
    w8 = back(w4, 2) + fwd(w4, 2)
    w16 = back(w8, 4) + fwd(w8, 4)
    lane = lax.broadcasted_iota(jnp.int32, (ts, D_POOL), 1)
    t = lax.broadcasted_iota(jnp.int32, (ts, D_POOL), 0) + i * ts
    grp = lane // HEAD_DIM
    sel = lambda vals: jnp.where(grp == 0, vals[0], jnp.where(grp == 1, vals[1], jnp.where(grp == 2, vals[2], vals[3])))
    wsum = sel([w[HALO:HALO + ts] for w in (w2, w4, w8, w16)])
    half = sel([wd // 2 for wd in POOL_WINDOWS])
    width = sel(list(POOL_WINDOWS))
    cnt = jnp.minimum(t + width - half, seq) - jnp.maximum(t - half, 0)
    pooled = wsum / cnt.astype(F32) - u
    o_ref[0] = _dot(pooled, mix_ref[...], precision=HI) * scale_ref[...]


def pool_mixer(proj, mix_bd, scale, ts=512):
    b, s, _ = proj.shape
    nblk = ts // HALO
    last = s // HALO - 1
    cb = COL_POOL // D_POOL
    hb = COL_POOL // D_POOL
    return pl.pallas_call(
        functools.partial(_pool_kernel, s),
        grid=(b, s // ts),
        in_specs=[
            pl.BlockSpec((1, ts, D_POOL), lambda bi, i: (bi, i, cb)),
            pl.BlockSpec((1, HALO, D_POOL), lambda bi, i: (bi, jnp.maximum(i * nblk - 1, 0), hb)),
            pl.BlockSpec((1, HALO, D_POOL), lambda bi, i: (bi, jnp.minimum((i + 1) * nblk, last), hb)),
            pl.BlockSpec((D_POOL, D_POOL), lambda bi, i: (0, 0)),
            pl.BlockSpec((1, D_POOL), lambda bi, i: (0, 0)),
        ],
        out_specs=pl.BlockSpec((1, ts, D_POOL), lambda bi, i: (bi, i, 0)),
        out_shape=jax.ShapeDtypeStruct((b, s, D_POOL), F32),
        compiler_params=_cparams(("parallel", "parallel")),
        name="pool_mixer",
    )(proj, proj, proj, mix_bd, scale.reshape(1, D_POOL))


def _out_proj_kernel(x_ref, yf_ref, yb_ref, bonus_ref, g_ref, att_ref, pool_ref, lw_ref, lb_ref, bd_ref, w_ref, o_ref):
    y = yf_ref[0] + yb_ref[0]
    bd = bd_ref[...]
    mu = _dot(y, bd, precision=HI) * (1.0 / HEAD_DIM)
    yc = y - mu
    var = _dot(yc * yc, bd, precision=HI) * (1.0 / HEAD_DIM)
    ya = (yc * lax.rsqrt(var + GN_EPS) * lw_ref[...] + lb_ref[...] + bonus_ref[...]) * g_ref[...]
    cat = jnp.concatenate([ya, att_ref[...], pool_ref[...]], axis=1).astype(BF16)
    o_ref[...] = x_ref[...] + _dot(cat, w_ref[...])


def out_proj(x2d, y_scan, bonus, g, att, pool, lnx_w, lnx_b, bd, w_bf16, tm=512):
    t, d = x2d.shape
    rows = lambda n: pl.BlockSpec((tm, n), lambda i: (i, 0))
    fixed = lambda shape: pl.BlockSpec(shape, lambda i: (0, 0))
    return pl.pallas_call(
        _out_proj_kernel,
        grid=(t // tm,),
        in_specs=[rows(d),
                  pl.BlockSpec((1, tm, D_RWKV), lambda i: (0, i, 0)),
                  pl.BlockSpec((1, tm, D_RWKV), lambda i: (1, i, 0)),
                  rows(D_RWKV), rows(D_RWKV), rows(D_DIFF), rows(D_POOL),
                  fixed((1, D_RWKV)), fixed((1, D_RWKV)), fixed((D_RWKV, D_RWKV)), fixed((d, d))],
        out_specs=rows(d),
        out_shape=jax.ShapeDtypeStruct((t, d), F32),
        compiler_params=_cparams(("parallel",)),
        name="out_proj",
    )(x2d, y_scan, y_scan, bonus, g, att, pool, lnx_w.reshape(1, -1), lnx_b.reshape(1, -1), bd, w_bf16)


def _ffn_kernel(x_ref, g_ref, wg_ref, wu_ref, wd_ref, o_ref, h_ref, acc_ref):
    f = pl.program_id(1)

    @pl.when(f == 0)
    def _():
        h_ref[...] = _rms(x_ref[...], g_ref[...]).astype(BF16)
        acc_ref[...] = jnp.zeros_like(acc_ref)

    h = h_ref[...]
    gate = _dot(h, wg_ref[...])
    act = gate * _sigmoid(gate) * _dot(h, wu_ref[...])
    acc_ref[...] += _dot(act.astype(BF16), wd_ref[...])

    @pl.when(f == pl.num_programs(1) - 1)
    def _():
        o_ref[...] = x_ref[...] + acc_ref[...]


def ffn_dense(x2d, g, wg, wu, wd, tm=1024, tf=512):
    t, d = x2d.shape
    ff = wg.shape[1]
    return pl.pallas_call(
        _ffn_kernel,
        grid=(t // tm, ff // tf),
        in_specs=[pl.BlockSpec((tm, d), lambda i, f: (i, 0)),
                  pl.BlockSpec((1, d), lambda i, f: (0, 0)),
                  pl.BlockSpec((d, tf), lambda i, f: (0, f)),
                  pl.BlockSpec((d, tf), lambda i, f: (0, f)),
                  pl.BlockSpec((tf, d), lambda i, f: (f, 0))],
        out_specs=pl.BlockSpec((tm, d), lambda i, f: (i, 0)),
        out_shape=jax.ShapeDtypeStruct((t, d), F32),
        scratch_shapes=[pltpu.VMEM((tm, d), BF16), pltpu.VMEM((tm, d), F32)],
        compiler_params=_cparams(("parallel", "arbitrary")),
        name="ffn_dense",
    )(x2d, g.reshape(1, d), wg, wu, wd)


def _router_kernel(x_ref, g_ref, wr_ref, h_ref, comb_ref):
    h = _rms(x_ref[...], g_ref[...])
    h_ref[...] = h.astype(BF16)
    logits = _dot(h, wr_ref[...], precision=HI)
    lane = lax.broadcasted_iota(jnp.int32, logits.shape, 1)
    logits = jnp.where(lane < N_EXPERTS, logits, -jnp.inf)
    v1 = jnp.max(logits, axis=1, keepdims=True)
    i1 = jnp.min(jnp.where(logits == v1, lane, LANES), axis=1, keepdims=True)
    rest = jnp.where(lane == i1, -jnp.inf, logits)
    v2 = jnp.max(rest, axis=1, keepdims=True)
    i2 = jnp.min(jnp.where(rest == v2, lane, LANES), axis=1, keepdims=True)
    e2 = jnp.exp(v2 - v1)
    g1 = 1.0 / (1.0 + e2)
    comb_ref[...] = jnp.where(lane == i1, g1, jnp.where(lane == i2, e2 * g1, 0.0))


def moe_router(x2d, g, wr_pad, tm=512):
    t, d = x2d.shape
    return pl.pallas_call(
        _router_kernel,
        grid=(t // tm,),
        in_specs=[pl.BlockSpec((tm, d), lambda i: (i, 0)),
                  pl.BlockSpec((1, d), lambda i: (0, 0)),
                  pl.BlockSpec((d, LANES), lambda i: (0, 0))],
        out_specs=[pl.BlockSpec((tm, d), lambda i: (i, 0)), pl.BlockSpec((tm, LANES), lambda i: (i, 0))],
        out_shape=[jax.ShapeDtypeStruct((t, d), BF16), jax.ShapeDtypeStruct((t, LANES), F32)],
        compiler_params=_cparams(("parallel",)),
        name="moe_router",
    )(x2d, g.reshape(1, d), wr_pad)


def _moe_kernel(x_ref, h_ref, comb_ref, wg_ref, wu_ref, wd_ref, gout_ref, o_ref, acc_ref):
    e = pl.program_id(1)
    f = pl.program_id(2)

    @pl.when(jnp.logical_and(e == 0, f == 0))
    def _():
        acc_ref[...] = jnp.zeros_like(acc_ref)

    h = h_ref[...]
    gate = _dot(h, wg_ref[0])
    act = gate * _sigmoid(gate) * _dot(h, wu_ref[0])
    y = _dot(act.astype(BF16), wd_ref[0])
    onehot = (lax.broadcasted_iota(jnp.int32, (LANES, LANES), 0) == e).astype(F32)
    ce = _dot(comb_ref[...], onehot, precision=HI)
    acc_ref[...] += jnp.concatenate([ce] * (y.shape[1] // LANES), axis=1) * y

    @pl.when(jnp.logical_and(e == pl.num_programs(1) - 1, f == pl.num_programs(2) - 1))
    def _():
        o_ref[...] = _rms(x_ref[...] + acc_ref[...], gout_ref[...])


def moe_ffn(x2d, h, comb, wg, wu, wd, g_out, tm=1024, tf=512):
    t, d = x2d.shape
    ne, _, ff = wg.shape
    return pl.pallas_call(
        _moe_kernel,
        grid=(t // tm, ne, ff // tf),
        in_specs=[pl.BlockSpec((tm, d), lambda i, e, f: (i, 0)),
                  pl.BlockSpec((tm, d), lambda i, e, f: (i, 0)),
                  pl.BlockSpec((tm, LANES), lambda i, e, f: (i, 0)),
                  pl.BlockSpec((1, d, tf), lambda i, e, f: (e, 0, f)),
                  pl.BlockSpec((1, d, tf), lambda i, e, f: (e, 0, f)),
                  pl.BlockSpec((1, tf, d), lambda i, e, f: (e, f, 0)),
                  pl.BlockSpec((1, d), lambda i, e, f: (0, 0))],
        out_specs=pl.BlockSpec((tm, d), lambda i, e, f: (i, 0)),
        out_shape=jax.ShapeDtypeStruct((t, d), F32),
        scratch_shapes=[pltpu.VMEM((tm, d), F32)],
        compiler_params=_cparams(("parallel", "arbitrary", "arbitrary")),
        name="moe_ffn",
    )(x2d, h, comb, wg, wu, wd, g_out.reshape(1, d))


def _block_diag_ones(n, blk):
    i = jnp.arange(n) // blk
    return (i[:, None] == i[None, :]).astype(F32)


def _rope_tables(positions):
    half = ROT_DIM // 2
    inv_freq = jnp.power(ROPE_THETA, -(jnp.arange(half, dtype=F32) * 2.0 / ROT_DIM))
    ang = positions.astype(F32)[..., None] * inv_freq
    cos, sin = jnp.cos(ang), jnp.sin(ang)
    one = jnp.ones(cos.shape[:-1] + (DIFF_QK - ROT_DIM,), F32)
    zero = jnp.zeros_like(one)
    z8 = jnp.zeros_like(sin)
    cs = jnp.concatenate([cos, cos, one], axis=-1)
    s1 = jnp.concatenate([z8, sin, zero], axis=-1)
    s2 = jnp.concatenate([-sin, z8, zero], axis=-1)
    return tuple(jnp.concatenate([t, t], axis=-1) for t in (cs, s1, s2))


def kernel(x, positions, norm_mix, w_in_first, w_in_rest, tshift, decay_bias, decay_up, iclr_bias, iclr_up, gate_up, k_k, k_a, r_k, lnx_w, lnx_b, vres_bias, vres_up, lambda_q, lambda_k, subln_w, pool_mix, pool_scale, w_out, norm_ffn, ffn_gate, ffn_up, ffn_down, router, exp_gate, exp_up, exp_down, norm_out):
    bsz, seq, d = x.shape
    depth = norm_mix.shape[0]
    x2d = x.reshape(bsz * seq, d)
    bd = _block_diag_ones(D_RWKV, HEAD_DIM)
    cs, s1, s2 = _rope_tables(positions)
    v_first = None
    for l in range(depth):
        if l == 0:
            w_in = w_in_first
        else:
            wr = w_in_rest[l - 1]
            w_in = jnp.concatenate([wr[:, :D_SHIFT], wr[:, D_SHIFT + MV_LORA:], wr[:, D_SHIFT:D_SHIFT + MV_LORA],
                                    jnp.zeros((d, LANES - MV_LORA), F32)], axis=1)
        proj = norm_matmul(x2d, norm_mix[l], w_in.astype(BF16)).reshape(bsz, seq, -1)

        zeros = jnp.zeros((DECAY_LORA, 2 * D_RWKV), F32)
        dec = jnp.concatenate([decay_up[l, 0], decay_up[l, 1]], axis=1)
        icl = jnp.concatenate([iclr_up[l, 0], iclr_up[l, 1]], axis=1)
        w_lora = jnp.concatenate([
            jnp.concatenate([dec, zeros], axis=1), jnp.zeros((AAA_LORA, 4 * D_RWKV), F32),
            jnp.zeros((DECAY_LORA, 4 * D_RWKV), F32), jnp.concatenate([zeros, icl], axis=1)], axis=0)
        prm = dict(tshift=tshift[l], w_lora=w_lora,
                   b_lora=jnp.concatenate([decay_bias[l, 0], decay_bias[l, 1], iclr_bias[l, 0], iclr_bias[l, 1]]).reshape(1, -1),
                   gate_up=gate_up[l], k_k=k_k[l].reshape(1, -1), k_a=k_a[l].reshape(1, -1),
                   r_k=r_k[l].reshape(1, -1), bd=bd)
        if l > 0:
            prm["vres_bias"] = vres_bias[l - 1].reshape(1, -1)
            prm["vres_up"] = jnp.concatenate([vres_up[l - 1], jnp.zeros((LANES - MV_LORA, D_RWKV), F32)], axis=0)
        r, v, kk, lw, kd, bb, g, bonus = rwkv_prep(proj, v_first, prm)
        if l == 0:
            v_first = v
        tr = lambda t: jnp.swapaxes(t, 2, 3)
        y_scan = wkv_scan(r, v, kk, lw, tr(lw), tr(kd), tr(bb)).reshape(2, bsz * seq, D_RWKV)

        lam_init = 0.8 - 0.6 * math.exp(-0.3 * l)
        qa, qb, kr, vb = attn_prep(proj, cs, s1, s2)
        att = diff_attention(qa, qb, kr, vb, lambda_q[l], lambda_k[l], subln_w[l], lam_init)

        mix_bd = jax.scipy.linalg.block_diag(*[pool_mix[l, gi] for gi in range(len(POOL_WINDOWS))])
        pool = pool_mixer(proj, mix_bd, pool_scale[l])

        flat = lambda t: t.reshape(bsz * seq, -1)
        x2d = out_proj(x2d, y_scan, flat(bonus), flat(g), flat(att), flat(pool), lnx_w[l], lnx_b[l], bd,
                       w_out[l].astype(BF16))
        i = l // 2
        if l % 2 == 0:
            x2d = ffn_dense(x2d, norm_ffn[l], ffn_gate[i].astype(BF16), ffn_up[i].astype(BF16), ffn_down[i].astype(BF16))
            if l == depth - 1:
                raise NotImplementedError("final norm is fused into the routed mixer")
        else:
            wr_pad = jnp.concatenate([router[i], jnp.zeros((d, LANES - N_EXPERTS), F32)], axis=1)
            h, comb = moe_router(x2d, norm_ffn[l], wr_pad)
            x2d = moe_ffn(x2d, h, comb, exp_gate[i].astype(BF16), exp_up[i].astype(BF16), exp_down[i].astype(BF16), norm_out)
    return x2d.reshape(bsz, seq, d)
```

```python
import functools
import math

import jax
import jax.numpy as jnp
from jax import lax
from jax.experimental import pallas as pl
from jax.experimental.pallas import tpu as pltpu

F32 = jnp.float32
BF16 = jnp.bfloat16
HI = lax.Precision.HIGHEST

D_MODEL = 1024
RWKV_HEADS = 4
HEAD_DIM = 64
D_RWKV = RWKV_HEADS * HEAD_DIM
DECAY_LORA = 64
AAA_LORA = 64
MV_LORA = 32
GATE_LORA = 128
D_SHIFT = 3 * D_RWKV + DECAY_LORA + AAA_LORA + GATE_LORA
W_DECAY_SCALE = 0.6065306597126334
GN_EPS = 64e-5
DIFF_HEADS = 4
DIFF_QK = 64
DIFF_V = 128
D_DIFF_QK = DIFF_HEADS * 2 * DIFF_QK
D_DIFF = DIFF_HEADS * DIFF_V
ROT_DIM = DIFF_QK // 4
ROPE_THETA = 500000.0
SUBLN_EPS = 1e-5
POOL_WINDOWS = (2, 4, 8, 16)
D_POOL = 256
D_FF = 3584
N_EXPERTS = 8
NORM_EPS = 1e-6
P_IN = D_SHIFT + 2 * D_DIFF_QK + D_DIFF + D_POOL
COL_Q, COL_K, COL_V, COL_POOL, COL_VDOWN = 1024, 1536, 2048, 2560, 2816

LANES = 128
HALO = 8
VMEM_LIMIT = 48 * 1024 * 1024

SCAN_CHUNK = 128


def _cparams(sem):
    return pltpu.CompilerParams(dimension_semantics=sem, vmem_limit_bytes=VMEM_LIMIT)


def _sigmoid(x):
    return 1.0 / (1.0 + jnp.exp(-x))


def _dot(a, b, **kw):
    return jnp.dot(a, b, preferred_element_type=F32, **kw)


def _bdot(a, b):
    return jnp.dot(a.astype(BF16), b.astype(BF16), preferred_element_type=F32)


def _rms(x, g):
    return x * lax.rsqrt(jnp.mean(x * x, axis=-1, keepdims=True) + NORM_EPS) * g


def _norm_matmul_kernel(x_ref, g_ref, w_ref, o_ref):
    h = _rms(x_ref[...], g_ref[...]).astype(BF16)
    o_ref[...] = _dot(h, w_ref[...])


def norm_matmul(x2d, g, w_bf16, tm=512):
    t, d = x2d.shape
    n = w_bf16.shape[1]
    return pl.pallas_call(
        _norm_matmul_kernel,
        grid=(t // tm,),
        in_specs=[
            pl.BlockSpec((tm, d), lambda i: (i, 0)),
            pl.BlockSpec((1, d), lambda i: (0, 0)),
            pl.BlockSpec((d, n), lambda i: (0, 0)),
        ],
        out_specs=pl.BlockSpec((tm, n), lambda i: (i, 0)),
        out_shape=jax.ShapeDtypeStruct((t, n), F32),
        compiler_params=_cparams(("parallel",)),
        name="norm_matmul",
    )(x2d, g.reshape(1, d), w_bf16)


def _shifted(u, prev_row, next_row):
    ts = u.shape[0]
    row = lax.broadcasted_iota(jnp.int32, u.shape, 0)
    prev = jnp.where(row == 0, prev_row, pltpu.roll(u, 1, 0))
    nxt = jnp.where(row == ts - 1, next_row, pltpu.roll(u, ts - 1, 0))
    return prev, nxt


def _rwkv_prep_kernel(has_vres, *refs):
    if has_vres:
        (u_ref, up_ref, un_ref, vd_ref, vf_ref, vb_ref, vu_ref, mu_ref, wl_ref, bias_ref, gu_ref,
         kkw_ref, ka_ref, rk_ref, bd_ref,
         r_ref, v_ref, kk_ref, lw_ref, kd_ref, bb_ref, g_ref, bonus_ref) = refs
    else:
        (u_ref, up_ref, un_ref, mu_ref, wl_ref, bias_ref, gu_ref,
         kkw_ref, ka_ref, rk_ref, bd_ref,
         r_ref, v_ref, kk_ref, lw_ref, kd_ref, bb_ref, g_ref, bonus_ref) = refs
    i = pl.program_id(1)
    n = pl.num_programs(1)
    u = u_ref[0]
    prev_row = jnp.where(i > 0, up_ref[0, HALO - 1:HALO, :], 0.0)
    next_row = jnp.where(i < n - 1, un_ref[0, 0:1, :], 0.0)
    prev, nxt = _shifted(u, prev_row, next_row)
    us = u + mu_ref[0:1, :] * (prev - u) + mu_ref[1:2, :] * (nxt - u)
    r = us[:, 0:256]
    k = us[:, 256:512]
    v = us[:, 512:768]
    xwa = us[:, 768:896]
    xg = us[:, 896:1024]
    lhs = jnp.concatenate([jnp.tanh(xwa), xwa], axis=1)
    z = _dot(lhs, wl_ref[...], precision=HI) + bias_ref[...]
    if has_vres:
        mix = _sigmoid(vb_ref[...] + _dot(vd_ref[0], vu_ref[...], precision=HI))
        v = v + (vf_ref[0] - v) * mix
    g = _dot(_sigmoid(xg), gu_ref[...], precision=HI)
    bd = bd_ref[...]
    kkr = k * kkw_ref[...]
    ss = _dot(kkr * kkr, bd, precision=HI)
    kk = kkr * lax.rsqrt(jnp.maximum(ss, 1e-24))
    r_ref[0] = r
    v_ref[0] = v
    kk_ref[0] = kk
    g_ref[0] = g
    bonus = jnp.zeros_like(v)
    for d in range(2):
        lw = -W_DECAY_SCALE * _sigmoid(z[:, 256 * d:256 * d + 256])
        a = _sigmoid(z[:, 512 + 256 * d:768 + 256 * d])
        kd = k * (1.0 + (a - 1.0) * ka_ref[...])
        lw_ref[d, 0] = lw
        kd_ref[d, 0] = kd
        bb_ref[d, 0] = kk * a
        bonus = bonus + _dot(r * kd * rk_ref[...], bd, precision=HI) * v
    bonus_ref[0] = bonus


def rwkv_prep(proj, v_first, p, ts=512):
    b, s, _ = proj.shape
    has_vres = v_first is not None
    nblk = ts // HALO
    last = s // HALO - 1
    row = lambda shape: pl.BlockSpec(shape, lambda bi, i: (0,) * len(shape))
    in_specs = [
        pl.BlockSpec((1, ts, D_SHIFT), lambda bi, i: (bi, i, 0)),
        pl.BlockSpec((1, HALO, D_SHIFT), lambda bi, i: (bi, jnp.maximum(i * nblk - 1, 0), 0)),
        pl.BlockSpec((1, HALO, D_SHIFT), lambda bi, i: (bi, jnp.minimum((i + 1) * nblk, last), 0)),
    ]
    args = [proj, proj, proj]
    if has_vres:
        in_specs += [
            pl.BlockSpec((1, ts, LANES), lambda bi, i: (bi, i, COL_VDOWN // LANES)),
            pl.BlockSpec((1, ts, D_RWKV), lambda bi, i: (bi, i, 0)),
            row((1, D_RWKV)), row((LANES, D_RWKV)),
        ]
        args += [proj, v_first, p["vres_bias"], p["vres_up"]]
    in_specs += [row((2, D_SHIFT)), row((256, 1024)), row((1, 1024)), row((GATE_LORA, D_RWKV)),
                 row((1, D_RWKV)), row((1, D_RWKV)), row((1, D_RWKV)), row((D_RWKV, D_RWKV))]
    args += [p["tshift"], p["w_lora"], p["b_lora"], p["gate_up"], p["k_k"], p["k_a"], p["r_k"], p["bd"]]
    one = pl.BlockSpec((1, ts, D_RWKV), lambda bi, i: (bi, i, 0))
    two = pl.BlockSpec((2, 1, ts, D_RWKV), lambda bi, i: (0, bi, i, 0))
    s1 = jax.ShapeDtypeStruct((b, s, D_RWKV), F32)
    s2 = jax.ShapeDtypeStruct((2, b, s, D_RWKV), F32)
    return pl.pallas_call(
        functools.partial(_rwkv_prep_kernel, has_vres),
        grid=(b, s // ts),
        in_specs=in_specs,
        out_specs=[one, one, one, two, two, two, one, one],
        out_shape=[s1, s1, s1, s2, s2, s2, s1, s1],
        compiler_params=_cparams(("parallel", "parallel")),
        name="rwkv_prep",
    )(*args)


def _unit_lower_inverse(m, eye):
    x = eye - m
    p = m
    for _ in range(int(math.log2(m.shape[0])) - 1):
        p = _bdot(p, p)
        x = x + _bdot(x, p)
    return x


def _wkv_kernel(r_ref, v_ref, kk_ref, lw_ref, lwt_ref, kdt_ref, bbt_ref, y_ref, h_ref):
    c = SCAN_CHUNK
    rev = pl.program_id(1) == 1

    @pl.when(pl.program_id(2) == 0)
    def _():
        h_ref[...] = jnp.zeros_like(h_ref)

    ti = lax.broadcasted_iota(jnp.int32, (c, c), 0)
    tj = lax.broadcasted_iota(jnp.int32, (c, c), 1)
    delta = jnp.where(rev, ti - tj, tj - ti)
    strict = delta < 0
    incl = delta <= 0
    before = delta >= 0
    eye = (ti == tj).astype(F32)
    tri = incl.astype(F32)
    tri_t = jnp.concatenate([before.astype(F32), jnp.ones((c, LANES), F32)], axis=1)
    lane = lax.broadcasted_iota(jnp.int32, (c, LANES), 1)
    head_mask = (lane < HEAD_DIM, lane >= HEAD_DIM)
    bi = lax.broadcasted_iota(jnp.int32, (LANES, LANES), 0) // HEAD_DIM
    bj = lax.broadcasted_iota(jnp.int32, (LANES, LANES), 1) // HEAD_DIM
    same_head = bi == bj

    for p in range(2):
        sl = slice(LANES * p, LANES * p + LANES)
        lw = lw_ref[0, 0, :, sl]
        cum = _dot(tri, lw, precision=HI)
        a = kk_ref[0, :, sl] * jnp.exp(cum - lw)
        rt = r_ref[0, :, sl] * jnp.exp(cum)
        cum_t = _dot(lwt_ref[0, 0, sl, :], tri_t, precision=HI)
        einv = jnp.exp(-cum_t[:, :c])
        kt_t = kdt_ref[0, 0, sl, :] * einv
        bt_t = bbt_ref[0, 0, sl, :] * einv
        g_all = jnp.exp(cum_t[:, c:])
        v = v_ref[0, :, sl]
        h = h_ref[p]
        rhs = jnp.concatenate([bt_t, kt_t], axis=1).astype(BF16)
        t_inv, m_ak, l_o = [], [], []
        for hh in range(2):
            lhs = jnp.concatenate([jnp.where(head_mask[hh], a, 0.0), jnp.where(head_mask[hh], rt, 0.0)], axis=0)
            gm = _dot(lhs.astype(BF16), rhs)
            t_inv.append(_unit_lower_inverse(jnp.where(strict, gm[:c, :c], 0.0), eye))
            m_ak.append(jnp.where(strict, gm[:c, c:], 0.0))
            l_o.append(jnp.concatenate([jnp.where(incl, gm[c:, c:], 0.0), -jnp.where(incl, gm[c:, :c], 0.0)], axis=1))
        xr = _bdot(jnp.concatenate([a, rt], axis=0), h)
        vb = v.astype(BF16)
        z = xr[:c] + jnp.where(head_mask[0], _dot(m_ak[0].astype(BF16), vb), _dot(m_ak[1].astype(BF16), vb))
        zb = z.astype(BF16)
        u = jnp.where(head_mask[0], _dot(t_inv[0].astype(BF16), zb), _dot(t_inv[1].astype(BF16), zb))
        vu = jnp.concatenate([v, u], axis=0).astype(BF16)
        o = xr[c:] + jnp.where(head_mask[0], _dot(l_o[0].astype(BF16), vu), _dot(l_o[1].astype(BF16), vu))
        upd = _dot(jnp.concatenate([kt_t, -bt_t], axis=1).astype(BF16), vu)
        h_ref[p] = g_all * (h + jnp.where(same_head, upd, 0.0))
        y_ref[0, 0, :, sl] = o


def wkv_scan(r, v, kk, lw, lwt, kdt, bbt):
    b, s, _ = r.shape
    c = SCAN_CHUNK
    nc = s // c
    cidx = lambda d, ci: d * (nc - 1 - ci) + (1 - d) * ci
    row = pl.BlockSpec((1, c, D_RWKV), lambda bi, d, ci: (bi, cidx(d, ci), 0))
    rowd = pl.BlockSpec((1, 1, c, D_RWKV), lambda bi, d, ci: (d, bi, cidx(d, ci), 0))
    col = pl.BlockSpec((1, 1, D_RWKV, c), lambda bi, d, ci: (d, bi, 0, cidx(d, ci)))
    return pl.pallas_call(
        _wkv_kernel,
        grid=(b, 2, nc),
        in_specs=[row, row, row, rowd, col, col, col],
        out_specs=rowd,
        out_shape=jax.ShapeDtypeStruct((2, b, s, D_RWKV), F32),
        scratch_shapes=[pltpu.VMEM((2, LANES, LANES), F32)],
        compiler_params=_cparams(("parallel", "parallel", "arbitrary")),
        name="wkv_scan",
    )(r, v, kk, lw, lwt, kdt, bbt)


def _attn_prep_kernel(q_ref, k_ref, v_ref, c_ref, s1_ref, s2_ref, qa_ref, qb_ref, ko_ref, vo_ref):
    cs, s1, s2 = c_ref[0], s1_ref[0], s2_ref[0]
    lane = lax.broadcasted_iota(jnp.int32, cs.shape, 1)
    first = lane < DIFF_QK
    scale = DIFF_QK ** -0.5
    for j in range(D_DIFF_QK // LANES):
        sl = slice(LANES * j, LANES * j + LANES)

        def rope(t):
            return t * cs + pltpu.roll(t, ROT_DIM // 2, 1) * s1 + pltpu.roll(t, LANES - ROT_DIM // 2, 1) * s2

        q = rope(q_ref[0, :, sl]) * scale
        qa_ref[0, :, sl] = jnp.where(first, q, 0.0).astype(BF16)
        qb_ref[0, :, sl] = jnp.where(first, 0.0, q).astype(BF16)
        ko_ref[0, :, sl] = rope(k_ref[0, :, sl]).astype(BF16)
    vo_ref[0] = v_ref[0].astype(BF16)


def attn_prep(proj, cs, s1, s2, ts=512):
    b, s, _ = proj.shape
    col = lambda off: pl.BlockSpec((1, ts, 512), lambda bi, i: (bi, i, off // 512))
    tab = pl.BlockSpec((1, ts, LANES), lambda bi, i: (bi, i, 0))
    out = pl.BlockSpec((1, ts, 512), lambda bi, i: (bi, i, 0))
    sh = jax.ShapeDtypeStruct((b, s, 512), BF16)
    return pl.pallas_call(
        _attn_prep_kernel,
        grid=(b, s // ts),
        in_specs=[col(COL_Q), col(COL_K), col(COL_V), tab, tab, tab],
        out_specs=[out, out, out, out],
        out_shape=[sh, sh, sh, sh],
        compiler_params=_cparams(("parallel", "parallel")),
        name="attn_prep",
    )(proj, proj, proj, cs, s1, s2)


def _attn_kernel(lam_init, qa_ref, qb_ref, k_ref, v_ref, lq_ref, lk_ref, sw_ref, o_ref,
                 m0_ref, l0_ref, a0_ref, m1_ref, l1_ref, a1_ref):
    j = pl.program_id(3)

    @pl.when(j == 0)
    def _():
        for m_ref, l_ref, a_ref in ((m0_ref, l0_ref, a0_ref), (m1_ref, l1_ref, a1_ref)):
            m_ref[...] = jnp.full_like(m_ref, -jnp.inf)
            l_ref[...] = jnp.zeros_like(l_ref)
            a_ref[...] = jnp.zeros_like(a_ref)

    kt = k_ref[0]
    vt = v_ref[0]
    reps = kt.shape[0] // LANES
    for q_ref, m_ref, l_ref, a_ref in ((qa_ref, m0_ref, l0_ref, a0_ref), (qb_ref, m1_ref, l1_ref, a1_ref)):
        s = lax.dot_general(q_ref[0], kt, (((1,), (1,)), ((), ())), preferred_element_type=F32)
        m_prev = m_ref[...]
        m_new = jnp.maximum(m_prev, jnp.max(s, axis=1, keepdims=True))
        alpha = jnp.exp(m_prev - m_new)
        p = jnp.exp(s - jnp.concatenate([m_new] * reps, axis=1))
        l_ref[...] = alpha * l_ref[...] + jnp.sum(p, axis=1, keepdims=True)
        a_ref[...] = alpha * a_ref[...] + _dot(p.astype(BF16), vt)
        m_ref[...] = m_new

    @pl.when(j == pl.num_programs(3) - 1)
    def _():
        e = jnp.exp(jnp.sum(lq_ref[...] * lk_ref[...], axis=1, keepdims=True))
        lam = e[0:1] - e[1:2] + lam_init
        o = a0_ref[...] / l0_ref[...] - lam * (a1_ref[...] / l1_ref[...])
        o = o * lax.rsqrt(jnp.mean(o * o, axis=1, keepdims=True) + SUBLN_EPS) * sw_ref[...]
        o_ref[0] = o * (1.0 - lam_init)


def diff_attention(qa, qb, k, v, lq, lk, sw, lam_init, tq=512, tk=512):
    b, s, _ = qa.shape
    qspec = pl.BlockSpec((1, tq, LANES), lambda bi, h, i, j: (bi, i, h))
    kspec = pl.BlockSpec((1, tk, LANES), lambda bi, h, i, j: (bi, j, h))
    small = lambda shape: pl.BlockSpec(shape, lambda bi, h, i, j: (0, 0))
    acc = pltpu.VMEM((tq, LANES), F32)
    return pl.pallas_call(
        functools.partial(_attn_kernel, lam_init),
        grid=(b, DIFF_HEADS, s // tq, s // tk),
        in_specs=[qspec, qspec, kspec, kspec, small((2, DIFF_QK)), small((2, DIFF_QK)), small((1, DIFF_V))],
        out_specs=pl.BlockSpec((1, tq, LANES), lambda bi, h, i, j: (bi, i, h)),
        out_shape=jax.ShapeDtypeStruct((b, s, D_DIFF), F32),
        scratch_shapes=[acc] * 6,
        compiler_params=_cparams(("parallel", "parallel", "parallel", "arbitrary")),
        name="diff_attention",
    )(qa, qb, k, v, lq, lk, sw.reshape(1, DIFF_V))


def _pool_kernel(seq, u_ref, up_ref, un_ref, mix_ref, scale_ref, o_ref):
    i = pl.program_id(1)
    n = pl.num_programs(1)
    u = u_ref[0]
    ts = u.shape[0]
    prev = jnp.where(i > 0, up_ref[0], 0.0)
    nxt = jnp.where(i < n - 1, un_ref[0], 0.0)
    ext = jnp.concatenate([prev, u, nxt], axis=0)
    rows = ext.shape[0]
    back = lambda t, sft: pltpu.roll(t, sft, 0)
    fwd = lambda t, sft: pltpu.roll(t, rows - sft, 0)
    w2 = ext + back(ext, 1)
    w4 = back(w2, 1) + fwd(w2, 1)
    w8 = back(w4, 2) + fwd(w4, 2)
    w16 = back(w8, 4) + fwd(w8, 4)
    lane = lax.broadcasted_iota(jnp.int32, (ts, D_POOL), 1)
    t = lax.broadcasted_iota(jnp.int32, (ts, D_POOL), 0) + i * ts
    grp = lane // HEAD_DIM
    sel = lambda vals: jnp.where(grp == 0, vals[0], jnp.where(grp == 1, vals[1], jnp.where(grp == 2, vals[2], vals[3])))
    wsum = sel([w[HALO:HALO + ts] for w in (w2, w4, w8, w16)])
    half = sel([wd // 2 for wd in POOL_WINDOWS])
    width = sel(list(POOL_WINDOWS))
    cnt = jnp.minimum(t + width - half, seq) - jnp.maximum(t - half, 0)
    pooled = wsum / cnt.astype(F32) - u
    o_ref[0] = _dot(pooled, mix_ref[...], precision=HI) * scale_ref[...]


def pool_mixer(proj, mix_bd, scale, ts=512):
    b, s, _ = proj.shape
    nblk = ts // HALO
    last = s // HALO - 1
    cb = COL_POOL // D_POOL
    hb = COL_POOL // D_POOL
    return pl.pallas_call(
        functools.partial(_pool_kernel, s),
        grid=(b, s // ts),
        in_specs=[
            pl.BlockSpec((1, ts, D_POOL), lambda bi, i: (bi, i, cb)),
            pl.BlockSpec((1, HALO, D_POOL), lambda bi, i: (bi, jnp.maximum(i * nblk - 1, 0), hb)),
            pl.BlockSpec((1, HALO, D_POOL), lambda bi, i: (bi, jnp.minimum((i + 1) * nblk, last), hb)),
            pl.BlockSpec((D_POOL, D_POOL), lambda bi, i: (0, 0)),
            pl.BlockSpec((1, D_POOL), lambda bi, i: (0, 0)),
        ],
        out_specs=pl.BlockSpec((1, ts, D_POOL), lambda bi, i: (bi, i, 0)),
        out_shape=jax.ShapeDtypeStruct((b, s, D_POOL), F32),
        compiler_params=_cparams(("parallel", "parallel")),
        name="pool_mixer",
    )(proj, proj, proj, mix_bd, scale.reshape(1, D_POOL))


def _out_proj_kernel(x_ref, yf_ref, yb_ref, bonus_ref, g_ref, att_ref, pool_ref, lw_ref, lb_ref, bd_ref, w_ref, o_ref):
    y = yf_ref[0] + yb_ref[0]
    bd = bd_ref[...]
    mu = _dot(y, bd, precision=HI) * (1.0 / HEAD_DIM)
    yc = y - mu
    var = _dot(yc * yc, bd, precision=HI) * (1.0 / HEAD_DIM)
    ya = (yc * lax.rsqrt(var + GN_EPS) * lw_ref[...] + lb_ref[...] + bonus_ref[...]) * g_ref[...]
    cat = jnp.concatenate([ya, att_ref[...], pool_ref[...]], axis=1).astype(BF16)
    o_ref[...] = x_ref[...] + _dot(cat, w_ref[...])


def out_proj(x2d, y_scan, bonus, g, att, pool, lnx_w, lnx_b, bd, w_bf16, tm=512):
    t, d = x2d.shape
    rows = lambda n: pl.BlockSpec((tm, n), lambda i: (i, 0))
    fixed = lambda shape: pl.BlockSpec(shape, lambda i: (0, 0))
    return pl.pallas_call(
        _out_proj_kernel,
        grid=(t // tm,),
        in_specs=[rows(d),
                  pl.BlockSpec((1, tm, D_RWKV), lambda i: (0, i, 0)),
                  pl.BlockSpec((1, tm, D_RWKV), lambda i: (1, i, 0)),
                  rows(D_RWKV), rows(D_RWKV), rows(D_DIFF), rows(D_POOL),
                  fixed((1, D_RWKV)), fixed((1, D_RWKV)), fixed((D_RWKV, D_RWKV)), fixed((d, d))],
        out_specs=rows(d),
        out_shape=jax.ShapeDtypeStruct((t, d), F32),
        compiler_params=_cparams(("parallel",)),
        name="out_proj",
    )(x2d, y_scan, y_scan, bonus, g, att, pool, lnx_w.reshape(1, -1), lnx_b.reshape(1, -1), bd, w_bf16)


def _ffn_kernel(x_ref, g_ref, wg_ref, wu_ref, wd_ref, o_ref, h_ref, acc_ref):
    f = pl.program_id(1)

    @pl.when(f == 0)
    def _():
        h_ref[...] = _rms(x_ref[...], g_ref[...]).astype(BF16)
        acc_ref[...] = jnp.zeros_like(acc_ref)

    h = h_ref[...]
    gate = _dot(h, wg_ref[...])
    act = gate * _sigmoid(gate) * _dot(h, wu_ref[...])
    acc_ref[...] += _dot(act.astype(BF16), wd_ref[...])

    @pl.when(f == pl.num_programs(1) - 1)
    def _():
        o_ref[...] = x_ref[...] + acc_ref[...]


def ffn_dense(x2d, g, wg, wu, wd, tm=1024, tf=512):
    t, d = x2d.shape
    ff = wg.shape[1]
    return pl.pallas_call(
        _ffn_kernel,
        grid=(t // tm, ff // tf),
        in_specs=[pl.BlockSpec((tm, d), lambda i, f: (i, 0)),
                  pl.BlockSpec((1, d), lambda i, f: (0, 0)),
                  pl.BlockSpec((d, tf), lambda i, f: (0, f)),
                  pl.BlockSpec((d, tf), lambda i, f: (0, f)),
                  pl.BlockSpec((tf, d), lambda i, f: (f, 0))],
        out_specs=pl.BlockSpec((tm, d), lambda i, f: (i, 0)),
        out_shape=jax.ShapeDtypeStruct((t, d), F32),
        scratch_shapes=[pltpu.VMEM((tm, d), BF16), pltpu.VMEM((tm, d), F32)],
        compiler_params=_cparams(("parallel", "arbitrary")),
        name="ffn_dense",
    )(x2d, g.reshape(1, d), wg, wu, wd)


def _router_kernel(x_ref, g_ref, wr_ref, h_ref, comb_ref, rank_ref, cnt_ref):
    h = _rms(x_ref[...], g_ref[...])
    h_ref[...] = h.astype(BF16)
    logits = _dot(h, wr_ref[...], precision=HI)
    lane = lax.broadcasted_iota(jnp.int32, logits.shape, 1)
    logits = jnp.where(lane < N_EXPERTS, logits, -jnp.inf)
    v1 = jnp.max(logits, axis=1, keepdims=True)
    i1 = jnp.min(jnp.where(logits == v1, lane, LANES), axis=1, keepdims=True)
    rest = jnp.where(lane == i1, -jnp.inf, logits)
    v2 = jnp.max(rest, axis=1, keepdims=True)
    i2 = jnp.min(jnp.where(rest == v2, lane, LANES), axis=1, keepdims=True)
    e2 = jnp.exp(v2 - v1)
    g1 = 1.0 / (1.0 + e2)
    comb = jnp.where(lane == i1, g1, jnp.where(lane == i2, e2 * g1, 0.0))
    comb_ref[...] = comb
    tm = comb.shape[0]
    routed = comb > 0.0
    mask = jnp.where(routed, 1.0, 0.0).astype(BF16)
    earlier = lax.broadcasted_iota(jnp.int32, (tm, tm), 1) < lax.broadcasted_iota(jnp.int32, (tm, tm), 0)
    rank = _dot(jnp.where(earlier, 1.0, 0.0).astype(BF16), mask)
    rank_ref[...] = jnp.where(routed, rank, -1.0)
    cnt_ref[0] = _dot(jnp.ones((HALO, tm), BF16), mask)


def moe_router(x2d, g, wr_pad, tm):
    t, d = x2d.shape
    tok = pl.BlockSpec((tm, LANES), lambda i: (i, 0))
    return pl.pallas_call(
        _router_kernel,
        grid=(t // tm,),
        in_specs=[pl.BlockSpec((tm, d), lambda i: (i, 0)),
                  pl.BlockSpec((1, d), lambda i: (0, 0)),
                  pl.BlockSpec((d, LANES), lambda i: (0, 0))],
        out_specs=[pl.BlockSpec((tm, d), lambda i: (i, 0)), tok, tok,
                   pl.BlockSpec((1, HALO, LANES), lambda i: (i, 0, 0))],
        out_shape=[jax.ShapeDtypeStruct((t, d), BF16), jax.ShapeDtypeStruct((t, LANES), F32),
                   jax.ShapeDtypeStruct((t, LANES), F32), jax.ShapeDtypeStruct((t // tm, HALO, LANES), F32)],
        compiler_params=_cparams(("parallel",)),
        name="moe_router",
    )(x2d, g.reshape(1, d), wr_pad)


def _moe_kernel(cnt_ref, x_ref, h_ref, comb_ref, rank_ref, rank_t_ref, wg_ref, wu_ref, wd_ref, gout_ref, o_ref,
                acc_ref, xe_ref, ye_ref, gc_ref):
    i, e, f = pl.program_id(0), pl.program_id(1), pl.program_id(2)
    tm = h_ref.shape[0]
    cap = xe_ref.shape[1]
    nblk = (cnt_ref[i, e] + cap - 1) // cap
    onehot = (lax.broadcasted_iota(jnp.int32, (LANES, LANES), 0) == e).astype(F32)

    @pl.when(jnp.logical_and(e == 0, f == 0))
    def _():
        acc_ref[...] = jnp.zeros_like(acc_ref)

    @pl.when(f == 0)
    def _():
        gate_col = _dot(comb_ref[...], onehot, precision=HI)
        slot_row = rank_t_ref[0]

        def gather(j, carry):
            rid = lax.broadcasted_iota(jnp.int32, (cap, tm), 0) + j * cap
            pick = jnp.where(slot_row == rid.astype(F32), 1.0, 0.0)
            xe_ref[j] = _dot(pick.astype(BF16), h_ref[...]).astype(BF16)
            gc_ref[j] = _dot(pick, gate_col, precision=HI)
            return carry

        lax.fori_loop(0, nblk, gather, 0)

    def expert(j, carry):
        xe = xe_ref[j]
        gate = _dot(xe, wg_ref[0])
        act = gate * _sigmoid(gate) * _dot(xe, wu_ref[0])
        y = _dot(act.astype(BF16), wd_ref[0])
        ye_ref[j] = jnp.where(f == 0, y, ye_ref[j] + y)
        return carry

    lax.fori_loop(0, nblk, expert, 0)

    @pl.when(f == pl.num_programs(2) - 1)
    def _():
        slot_col = _dot(rank_ref[...], onehot, precision=HI)
        slot_col = jnp.concatenate([slot_col] * (cap // LANES), axis=1)

        def scatter(j, carry):
            cid = lax.broadcasted_iota(jnp.int32, (tm, cap), 1) + j * cap
            place = jnp.where(slot_col == cid.astype(F32), 1.0, 0.0).astype(BF16)
            ys = ye_ref[j] * jnp.concatenate([gc_ref[j]] * (ye_ref.shape[2] // LANES), axis=1)
            acc_ref[...] += _dot(place, ys.astype(BF16))
            return carry

        lax.fori_loop(0, nblk, scatter, 0)

    @pl.when(jnp.logical_and(e == pl.num_programs(1) - 1, f == pl.num_programs(2) - 1))
    def _():
        o_ref[...] = _rms(x_ref[...] + acc_ref[...], gout_ref[...])


def moe_ffn(x2d, h, comb, rank, rank_t, cnt, wg, wu, wd, g_out, tm, tf=512, cap=384):
    t, d = x2d.shape
    ne, _, ff = wg.shape
    max_blocks = -(-tm // cap)
    tok = lambda n: pl.BlockSpec((tm, n), lambda i, e, f, c: (i, 0))
    grid_spec = pltpu.PrefetchScalarGridSpec(
        num_scalar_prefetch=1,
        grid=(t // tm, ne, ff // tf),
        in_specs=[tok(d), tok(d), tok(LANES), tok(LANES),
                  pl.BlockSpec((1, 1, tm), lambda i, e, f, c: (e, 0, i)),
                  pl.BlockSpec((1, d, tf), lambda i, e, f, c: (e, 0, f)),
                  pl.BlockSpec((1, d, tf), lambda i, e, f, c: (e, 0, f)),
                  pl.BlockSpec((1, tf, d), lambda i, e, f, c: (e, f, 0)),
                  pl.BlockSpec((1, d), lambda i, e, f, c: (0, 0))],
        out_specs=tok(d),
        scratch_shapes=[pltpu.VMEM((tm, d), F32), pltpu.VMEM((max_blocks, cap, d), BF16),
                        pltpu.VMEM((max_blocks, cap, d), F32), pltpu.VMEM((max_blocks, cap, LANES), F32)],
    )
    return pl.pallas_call(
        _moe_kernel,
        grid_spec=grid_spec,
        out_shape=jax.ShapeDtypeStruct((t, d), F32),
        compiler_params=_cparams(("parallel", "arbitrary", "arbitrary")),
        name="moe_ffn",
    )(cnt, x2d, h, comb, rank, rank_t, wg, wu, wd, g_out.reshape(1, d))


def _block_diag_ones(n, blk):
    i = jnp.arange(n) // blk
    return (i[:, None] == i[None, :]).astype(F32)


def _rope_tables(positions):
    half = ROT_DIM // 2
    inv_freq = jnp.power(ROPE_THETA, -(jnp.arange(half, dtype=F32) * 2.0 / ROT_DIM))
    ang = positions.astype(F32)[..., None] * inv_freq
    cos, sin = jnp.cos(ang), jnp.sin(ang)
    one = jnp.ones(cos.shape[:-1] + (DIFF_QK - ROT_DIM,), F32)
    zero = jnp.zeros_like(one)
    z8 = jnp.zeros_like(sin)
    cs = jnp.concatenate([cos, cos, one], axis=-1)
    s1 = jnp.concatenate([z8, sin, zero], axis=-1)
    s2 = jnp.concatenate([-sin, z8, zero], axis=-1)
    return tuple(jnp.concatenate([t, t], axis=-1) for t in (cs, s1, s2))


def kernel(x, positions, norm_mix, w_in_first, w_in_rest, tshift, decay_bias, decay_up, iclr_bias, iclr_up, gate_up, k_k, k_a, r_k, lnx_w, lnx_b, vres_bias, vres_up, lambda_q, lambda_k, subln_w, pool_mix, pool_scale, w_out, norm_ffn, ffn_gate, ffn_up, ffn_down, router, exp_gate, exp_up, exp_down, norm_out):
    bsz, seq, d = x.shape
    depth = norm_mix.shape[0]
    x2d = x.reshape(bsz * seq, d)
    bd = _block_diag_ones(D_RWKV, HEAD_DIM)
    cs, s1, s2 = _rope_tables(positions)
    v_first = None
    for l in range(depth):
        if l == 0:
            w_in = w_in_first
        else:
            wr = w_in_rest[l - 1]
            w_in = jnp.concatenate([wr[:, :D_SHIFT], wr[:, D_SHIFT + MV_LORA:], wr[:, D_SHIFT:D_SHIFT + MV_LORA],
                                    jnp.zeros((d, LANES - MV_LORA), F32)], axis=1)
        proj = norm_matmul(x2d, norm_mix[l], w_in.astype(BF16)).reshape(bsz, seq, -1)

        zeros = jnp.zeros((DECAY_LORA, 2 * D_RWKV), F32)
        dec = jnp.concatenate([decay_up[l, 0], decay_up[l, 1]], axis=1)
        icl = jnp.concatenate([iclr_up[l, 0], iclr_up[l, 1]], axis=1)
        w_lora = jnp.concatenate([
            jnp.concatenate([dec, zeros], axis=1), jnp.zeros((AAA_LORA, 4 * D_RWKV), F32),
            jnp.zeros((DECAY_LORA, 4 * D_RWKV), F32), jnp.concatenate([zeros, icl], axis=1)], axis=0)
        prm = dict(tshift=tshift[l], w_lora=w_lora,
                   b_lora=jnp.concatenate([decay_bias[l, 0], decay_bias[l, 1], iclr_bias[l, 0], iclr_bias[l, 1]]).reshape(1, -1),
                   gate_up=gate_up[l], k_k=k_k[l].reshape(1, -1), k_a=k_a[l].reshape(1, -1),
                   r_k=r_k[l].reshape(1, -1), bd=bd)
        if l > 0:
            prm["vres_bias"] = vres_bias[l - 1].reshape(1, -1)
            prm["vres_up"] = jnp.concatenate([vres_up[l - 1], jnp.zeros((LANES - MV_LORA, D_RWKV), F32)], axis=0)
        r, v, kk, lw, kd, bb, g, bonus = rwkv_prep(proj, v_first, prm)
        if l == 0:
            v_first = v
        tr = lambda t: jnp.swapaxes(t, 2, 3)
        y_scan = wkv_scan(r, v, kk, lw, tr(lw), tr(kd), tr(bb)).reshape(2, bsz * seq, D_RWKV)

        lam_init = 0.8 - 0.6 * math.exp(-0.3 * l)
        qa, qb, kr, vb = attn_prep(proj, cs, s1, s2)
        att = diff_attention(qa, qb, kr, vb, lambda_q[l], lambda_k[l], subln_w[l], lam_init)

        mix_bd = jax.scipy.linalg.block_diag(*[pool_mix[l, gi] for gi in range(len(POOL_WINDOWS))])
        pool = pool_mixer(proj, mix_bd, pool_scale[l])

        flat = lambda t: t.reshape(bsz * seq, -1)
        x2d = out_proj(x2d, y_scan, flat(bonus), flat(g), flat(att), flat(pool), lnx_w[l], lnx_b[l], bd,
                       w_out[l].astype(BF16))
        i = l // 2
        if l % 2 == 0:
            x2d = ffn_dense(x2d, norm_ffn[l], ffn_gate[i].astype(BF16), ffn_up[i].astype(BF16), ffn_down[i].astype(BF16))
            if l == depth - 1:
                raise NotImplementedError("final norm is fused into the routed mixer")
        else:
            wr_pad = jnp.concatenate([router[i], jnp.zeros((d, LANES - N_EXPERTS), F32)], axis=1)
            tm = min(1024, bsz * seq)
            h, comb, rank, cnt = moe_router(x2d, norm_ffn[l], wr_pad, tm)
            rank_t = rank[:, :N_EXPERTS].T.reshape(N_EXPERTS, 1, -1)
            cnt = cnt[:, 0, :N_EXPERTS].astype(jnp.int32)
            x2d = moe_ffn(x2d, h, comb, rank, rank_t, cnt, exp_gate[i].astype(BF16), exp_up[i].astype(BF16),
                          exp_down[i].astype(BF16), norm_out, tm)
    return x2d.reshape(bsz, seq, d)
```

```python
import functools
import math

import jax
import jax.numpy as jnp
from jax import lax
from jax.experimental import pallas as pl
from jax.experimental.pallas import tpu as pltpu

F32 = jnp.float32
BF16 = jnp.bfloat16
HI = lax.Precision.HIGHEST

D_MODEL = 1024
RWKV_HEADS = 4
HEAD_DIM = 64
D_RWKV = RWKV_HEADS * HEAD_DIM
DECAY_LORA = 64
AAA_LORA = 64
MV_LORA = 32
GATE_LORA = 128
D_SHIFT = 3 * D_RWKV + DECAY_LORA + AAA_LORA + GATE_LORA
W_DECAY_SCALE = 0.6065306597126334
GN_EPS = 64e-5
DIFF_HEADS = 4
DIFF_QK = 64
DIFF_V = 128
D_DIFF_QK = DIFF_HEADS * 2 * DIFF_QK
D_DIFF = DIFF_HEADS * DIFF_V
ROT_DIM = DIFF_QK // 4
ROPE_THETA = 500000.0
SUBLN_EPS = 1e-5
POOL_WINDOWS = (2, 4, 8, 16)
D_POOL = 256
D_FF = 3584
N_EXPERTS = 8
NORM_EPS = 1e-6
P_IN = D_SHIFT + 2 * D_DIFF_QK + D_DIFF + D_POOL
COL_Q, COL_K, COL_V, COL_POOL, COL_VDOWN = 1024, 1536, 2048, 2560, 2816

LANES = 128
HALO = 8
VMEM_LIMIT = 48 * 1024 * 1024

SCAN_CHUNK = 128


def _cparams(sem):
    return pltpu.CompilerParams(dimension_semantics=sem, vmem_limit_bytes=VMEM_LIMIT)


def _sigmoid(x):
    return 1.0 / (1.0 + jnp.exp(-x))


def _dot(a, b, **kw):
    return jnp.dot(a, b, preferred_element_type=F32, **kw)


def _bdot(a, b):
    return jnp.dot(a.astype(BF16), b.astype(BF16), preferred_element_type=F32)


def _rms(x, g):
    return x * lax.rsqrt(jnp.mean(x * x, axis=-1, keepdims=True) + NORM_EPS) * g


def _norm_matmul_kernel(x_ref, g_ref, w_ref, o_ref):
    h = _rms(x_ref[...], g_ref[...]).astype(BF16)
    o_ref[...] = _dot(h, w_ref[...])


def norm_matmul(x2d, g, w_bf16, tm=512):
    t, d = x2d.shape
    n = w_bf16.shape[1]
    return pl.pallas_call(
        _norm_matmul_kernel,
        grid=(t // tm,),
        in_specs=[
            pl.BlockSpec((tm, d), lambda i: (i, 0)),
            pl.BlockSpec((1, d), lambda i: (0, 0)),
            pl.BlockSpec((d, n), lambda i: (0, 0)),
        ],
        out_specs=pl.BlockSpec((tm, n), lambda i: (i, 0)),
        out_shape=jax.ShapeDtypeStruct((t, n), F32),
        compiler_params=_cparams(("parallel",)),
        name="norm_matmul",
    )(x2d, g.reshape(1, d), w_bf16)


def _shifted(u, prev_row, next_row):
    ts = u.shape[0]
    row = lax.broadcasted_iota(jnp.int32, u.shape, 0)
    prev = jnp.where(row == 0, prev_row, pltpu.roll(u, 1, 0))
    nxt = jnp.where(row == ts - 1, next_row, pltpu.roll(u, ts - 1, 0))
    return prev, nxt


def _rwkv_prep_kernel(has_vres, *refs):
    if has_vres:
        (u_ref, up_ref, un_ref, vd_ref, vf_ref, vb_ref, vu_ref, mu_ref, wl_ref, bias_ref, gu_ref,
         kkw_ref, ka_ref, rk_ref, bd_ref,
         r_ref, v_ref, kk_ref, lw_ref, kd_ref, bb_ref, g_ref, bonus_ref) = refs
    else:
        (u_ref, up_ref, un_ref, mu_ref, wl_ref, bias_ref, gu_ref,
         kkw_ref, ka_ref, rk_ref, bd_ref,
         r_ref, v_ref, kk_ref, lw_ref, kd_ref, bb_ref, g_ref, bonus_ref) = refs
    i = pl.program_id(1)
    n = pl.num_programs(1)
    u = u_ref[0]
    prev_row = jnp.where(i > 0, up_ref[0, HALO - 1:HALO, :], 0.0)
    next_row = jnp.where(i < n - 1, un_ref[0, 0:1, :], 0.0)
    prev, nxt = _shifted(u, prev_row, next_row)
    us = u + mu_ref[0:1, :] * (prev - u) + mu_ref[1:2, :] * (nxt - u)
    r = us[:, 0:256]
    k = us[:, 256:512]
    v = us[:, 512:768]
    xwa = us[:, 768:896]
    xg = us[:, 896:1024]
    lhs = jnp.concatenate([jnp.tanh(xwa), xwa], axis=1)
    z = _dot(lhs, wl_ref[...], precision=HI) + bias_ref[...]
    if has_vres:
        mix = _sigmoid(vb_ref[...] + _dot(vd_ref[0], vu_ref[...], precision=HI))
        v = v + (vf_ref[0] - v) * mix
    g = _dot(_sigmoid(xg), gu_ref[...], precision=HI)
    bd = bd_ref[...]
    kkr = k * kkw_ref[...]
    ss = _dot(kkr * kkr, bd, precision=HI)
    kk = kkr * lax.rsqrt(jnp.maximum(ss, 1e-24))
    r_ref[0] = r
    v_ref[0] = v
    kk_ref[0] = kk
    g_ref[0] = g
    bonus = jnp.zeros_like(v)
    for d in range(2):
        lw = -W_DECAY_SCALE * _sigmoid(z[:, 256 * d:256 * d + 256])
        a = _sigmoid(z[:, 512 + 256 * d:768 + 256 * d])
        kd = k * (1.0 + (a - 1.0) * ka_ref[...])
        lw_ref[d, 0] = lw
        kd_ref[d, 0] = kd
        bb_ref[d, 0] = kk * a
        bonus = bonus + _dot(r * kd * rk_ref[...], bd, precision=HI) * v
    bonus_ref[0] = bonus


def rwkv_prep(proj, v_first, p, ts=512):
    b, s, _ = proj.shape
    has_vres = v_first is not None
    nblk = ts // HALO
    last = s // HALO - 1
    row = lambda shape: pl.BlockSpec(shape, lambda bi, i: (0,) * len(shape))
    in_specs = [
        pl.BlockSpec((1, ts, D_SHIFT), lambda bi, i: (bi, i, 0)),
        pl.BlockSpec((1, HALO, D_SHIFT), lambda bi, i: (bi, jnp.maximum(i * nblk - 1, 0), 0)),
        pl.BlockSpec((1, HALO, D_SHIFT), lambda bi, i: (bi, jnp.minimum((i + 1) * nblk, last), 0)),
    ]
    args = [proj, proj, proj]
    if has_vres:
        in_specs += [
            pl.BlockSpec((1, ts, LANES), lambda bi, i: (bi, i, COL_VDOWN // LANES)),
            pl.BlockSpec((1, ts, D_RWKV), lambda bi, i: (bi, i, 0)),
            row((1, D_RWKV)), row((LANES, D_RWKV)),
        ]
        args += [proj, v_first, p["vres_bias"], p["vres_up"]]
    in_specs += [row((2, D_SHIFT)), row((256, 1024)), row((1, 1024)), row((GATE_LORA, D_RWKV)),
                 row((1, D_RWKV)), row((1, D_RWKV)), row((1, D_RWKV)), row((D_RWKV, D_RWKV))]
    args += [p["tshift"], p["w_lora"], p["b_lora"], p["gate_up"], p["k_k"], p["k_a"], p["r_k"], p["bd"]]
    one = pl.BlockSpec((1, ts, D_RWKV), lambda bi, i: (bi, i, 0))
    two = pl.BlockSpec((2, 1, ts, D_RWKV), lambda bi, i: (0, bi, i, 0))
    s1 = jax.ShapeDtypeStruct((b, s, D_RWKV), F32)
    s2 = jax.ShapeDtypeStruct((2, b, s, D_RWKV), F32)
    return pl.pallas_call(
        functools.partial(_rwkv_prep_kernel, has_vres),
        grid=(b, s // ts),
        in_specs=in_specs,
        out_specs=[one, one, one, two, two, two, one, one],
        out_shape=[s1, s1, s1, s2, s2, s2, s1, s1],
        compiler_params=_cparams(("parallel", "parallel")),
        name="rwkv_prep",
    )(*args)


def _unit_lower_inverses(ms, eye):
    xs = [eye - m for m in ms]
    ps = [_bdot(m, m) for m in ms]
    levels = int(math.log2(ms[0].shape[0])) - 1
    for k in range(levels):
        nxt = [_bdot(p, p) for p in ps] if k < levels - 1 else None
        xs = [x + _bdot(x, p) for x, p in zip(xs, ps)]
        ps = nxt
    return xs


def _split3(x):
    hi = x.astype(BF16)
    r1 = x - hi.astype(F32)
    mid = r1.astype(BF16)
    lo = (r1 - mid.astype(F32)).astype(BF16)
    return hi, mid, lo


def _chunk_terms(rev, r, v, kk, lw, kd, bb):
    c = SCAN_CHUNK
    n = len(r)
    ti = lax.broadcasted_iota(jnp.int32, (c, c), 0)
    tj = lax.broadcasted_iota(jnp.int32, (c, c), 1)
    eye = (ti == tj).astype(F32)
    same_head = (ti // HEAD_DIM) == (tj // HEAD_DIM)
    first = lax.broadcasted_iota(jnp.int32, (c, LANES), 1) < HEAD_DIM
    first2 = jnp.concatenate([first, first], axis=1)
    strict = [(ti - tj if rv else tj - ti) < 0 for rv in rev]
    incl = [(ti - tj if rv else tj - ti) <= 0 for rv in rev]

    cum = []
    for i in range(n):
        cum3 = _dot(jnp.where(incl[i], 1.0, 0.0).astype(BF16), jnp.concatenate(_split3(lw[i]), axis=1))
        cum.append(cum3[:, :LANES] + cum3[:, LANES:2 * LANES] + cum3[:, 2 * LANES:])
    a = [kk[i] * jnp.exp(cum[i] - lw[i]) for i in range(n)]
    rt = [r[i] * jnp.exp(cum[i]) for i in range(n)]
    einv = [jnp.exp(-cm) for cm in cum]
    kt_t = [(kd[i] * einv[i]).T for i in range(n)]
    bt_t = [(bb[i] * einv[i]).T for i in range(n)]
    last = [cum[i][0:1] if rev[i] else cum[i][c - 1:c] for i in range(n)]
    g_all = [jnp.exp(jnp.broadcast_to(x, (c, LANES))).T for x in last]
    vb = [x.astype(BF16) for x in v]
    rhs = [jnp.concatenate([bt_t[i], kt_t[i]], axis=1).astype(BF16) for i in range(n)]
    heads = [(i, hm) for i in range(n) for hm in (first, jnp.logical_not(first))]
    gm = [_dot(jnp.concatenate([jnp.where(hm, a[i], 0.0), jnp.where(hm, rt[i], 0.0)], axis=0).astype(BF16), rhs[i])
          for i, hm in heads]
    t_inv = _unit_lower_inverses([jnp.where(strict[i], g[:c, :c], 0.0) for (i, _), g in zip(heads, gm)], eye)
    mv = [_dot(jnp.where(strict[i], g[:c, c:], 0.0).astype(BF16), vb[i]) for (i, _), g in zip(heads, gm)]
    tam = [_bdot(t, jnp.concatenate([a[i], m], axis=1)) for (i, _), t, m in zip(heads, t_inv, mv)]
    lt = [_bdot(jnp.where(incl[i], g[c:, :c], 0.0), x) for (i, _), g, x in zip(heads, gm, tam)]
    lv = [_dot(jnp.where(incl[i], g[c:, c:], 0.0).astype(BF16), vb[i]) for (i, _), g in zip(heads, gm)]
    tam = [jnp.where(first2, tam[2 * i], tam[2 * i + 1]) for i in range(n)]
    lt = [jnp.where(first2, lt[2 * i], lt[2 * i + 1]) for i in range(n)]
    lv = [jnp.where(first, lv[2 * i], lv[2 * i + 1]) for i in range(n)]
    btx = [_bdot(bt_t[i], tam[i]) for i in range(n)]
    kv = [_dot(kt_t[i].astype(BF16), vb[i]) for i in range(n)]
    phi = [g_all[i] * (eye - jnp.where(same_head, btx[i][:, :LANES], 0.0)) for i in range(n)]
    psi = [g_all[i] * jnp.where(same_head, kv[i] - btx[i][:, LANES:], 0.0) for i in range(n)]
    ra = [rt[i] - lt[i][:, :LANES] for i in range(n)]
    oc = [lv[i] - lt[i][:, LANES:] for i in range(n)]
    return phi, psi, ra, oc


def _wkv_kernel(rf_ref, vf_ref, kkf_ref, lwf_ref, kdf_ref, bbf_ref,
                rb_ref, vb_ref, kkb_ref, lwb_ref, kdb_ref, bbb_ref, yf_ref, yb_ref, h_ref):
    @pl.when(pl.program_id(1) == 0)
    def _():
        h_ref[...] = jnp.zeros_like(h_ref)

    pairs = (slice(0, LANES), slice(LANES, 2 * LANES))
    probs = [(False, sl, (rf_ref, vf_ref, kkf_ref), (lwf_ref, kdf_ref, bbf_ref), yf_ref) for sl in pairs]
    probs += [(True, sl, (rb_ref, vb_ref, kkb_ref), (lwb_ref, kdb_ref, bbb_ref), yb_ref) for sl in pairs]
    rev = [pr[0] for pr in probs]
    r, v, kk = ([pr[2][j][0, :, pr[1]] for pr in probs] for j in range(3))
    lw, kd, bb = ([pr[3][j][0, 0, :, pr[1]] for pr in probs] for j in range(3))
    phi, psi, ra, oc = _chunk_terms(rev, r, v, kk, lw, kd, bb)
    res = []
    for i in range(len(probs)):
        h = h_ref[i]
        h_hi = h.astype(BF16)
        h_lo = (h - h_hi.astype(F32)).astype(BF16)
        lhs = jnp.concatenate([phi[i], ra[i]], axis=0).astype(BF16)
        res.append(_dot(jnp.concatenate([lhs, lhs], axis=1), jnp.concatenate([h_hi, h_lo], axis=0)))
    for i, pr in enumerate(probs):
        h_ref[i] = res[i][:SCAN_CHUNK] + psi[i]
        pr[4][0, :, pr[1]] = res[i][SCAN_CHUNK:] + oc[i]


def wkv_scan(r, v, kk, lw, kd, bb):
    b, s, _ = r.shape
    c = SCAN_CHUNK
    nc = s // c
    fwd = pl.BlockSpec((1, c, D_RWKV), lambda bi, ci: (bi, ci, 0))
    bwd = pl.BlockSpec((1, c, D_RWKV), lambda bi, ci: (bi, nc - 1 - ci, 0))
    fwd_d = pl.BlockSpec((1, 1, c, D_RWKV), lambda bi, ci: (0, bi, ci, 0))
    bwd_d = pl.BlockSpec((1, 1, c, D_RWKV), lambda bi, ci: (1, bi, nc - 1 - ci, 0))
    out = jax.ShapeDtypeStruct((b, s, D_RWKV), F32)
    return pl.pallas_call(
        _wkv_kernel,
        grid=(b, nc),
        in_specs=[fwd, fwd, fwd, fwd_d, fwd_d, fwd_d, bwd, bwd, bwd, bwd_d, bwd_d, bwd_d],
        out_specs=[fwd, bwd],
        out_shape=[out, out],
        scratch_shapes=[pltpu.VMEM((4, LANES, LANES), F32)],
        compiler_params=_cparams(("parallel", "arbitrary")),
        name="wkv_scan",
    )(r, v, kk, lw, kd, bb, r, v, kk, lw, kd, bb)


def _attn_prep_kernel(q_ref, k_ref, v_ref, c_ref, s1_ref, s2_ref, qa_ref, qb_ref, ko_ref, vo_ref):
    cs, s1, s2 = c_ref[0], s1_ref[0], s2_ref[0]
    lane = lax.broadcasted_iota(jnp.int32, cs.shape, 1)
    first = lane < DIFF_QK
    scale = DIFF_QK ** -0.5 * math.log2(math.e)
    for j in range(D_DIFF_QK // LANES):
        sl = slice(LANES * j, LANES * j + LANES)

        def rope(t):
            return t * cs + pltpu.roll(t, ROT_DIM // 2, 1) * s1 + pltpu.roll(t, LANES - ROT_DIM // 2, 1) * s2

        q = rope(q_ref[0, :, sl]) * scale
        qa_ref[0, :, sl] = jnp.where(first, q, 0.0).astype(BF16)
        qb_ref[0, :, sl] = jnp.where(first, 0.0, q).astype(BF16)
        ko_ref[0, :, sl] = rope(k_ref[0, :, sl]).astype(BF16)
    vo_ref[0] = v_ref[0].astype(BF16)


def attn_prep(proj, cs, s1, s2, ts=512):
    b, s, _ = proj.shape
    col = lambda off: pl.BlockSpec((1, ts, 512), lambda bi, i: (bi, i, off // 512))
    tab = pl.BlockSpec((1, ts, LANES), lambda bi, i: (bi, i, 0))
    out = pl.BlockSpec((1, ts, 512), lambda bi, i: (bi, i, 0))
    sh = jax.ShapeDtypeStruct((b, s, 512), BF16)
    return pl.pallas_call(
        _attn_prep_kernel,
        grid=(b, s // ts),
        in_specs=[col(COL_Q), col(COL_K), col(COL_V), tab, tab, tab],
        out_specs=[out, out, out, out],
        out_shape=[sh, sh, sh, sh],
        compiler_params=_cparams(("parallel", "parallel")),
        name="attn_prep",
    )(proj, proj, proj, cs, s1, s2)


def _attn_kernel(lam_init, q_rows, qa_ref, qb_ref, k_ref, v_ref, lq_ref, lk_ref, sw_ref, o_ref,
                 m0_ref, l0_ref, a0_ref, m1_ref, l1_ref, a1_ref):
    j = pl.program_id(3)

    @pl.when(j == 0)
    def _():
        for m_ref, l_ref, a_ref in ((m0_ref, l0_ref, a0_ref), (m1_ref, l1_ref, a1_ref)):
            m_ref[...] = jnp.full_like(m_ref, -jnp.inf)
            l_ref[...] = jnp.zeros_like(l_ref)
            a_ref[...] = jnp.zeros_like(a_ref)

    kt = k_ref[0]
    vt = v_ref[0]
    reps = kt.shape[0] // LANES
    maps = ((qa_ref, m0_ref, l0_ref, a0_ref), (qb_ref, m1_ref, l1_ref, a1_ref))
    tq = qa_ref.shape[1]
    units = [(mp, pl.ds(r0, q_rows)) for r0 in range(0, tq, q_rows) for mp in maps]
    n = len(units)
    s, soft = [None] * n, [None] * n
    for step in range(n + 2):
        if step < n:
            (q_ref, _, _, _), rows = units[step]
            s[step] = lax.dot_general(q_ref[0, rows, :], kt, (((1,), (1,)), ((), ())), preferred_element_type=F32)
        if 0 <= step - 1 < n:
            (_, m_ref, _, _), rows = units[step - 1]
            m_prev = m_ref[rows, :]
            m_new = jnp.maximum(m_prev, jnp.max(s[step - 1], axis=1, keepdims=True))
            p = jnp.exp2(s[step - 1] - jnp.concatenate([m_new] * reps, axis=1))
            soft[step - 1] = (jnp.exp2(m_prev - m_new), p.astype(BF16), jnp.sum(p, axis=1, keepdims=True), m_new)
            s[step - 1] = None
        if 0 <= step - 2 < n:
            (_, m_ref, l_ref, a_ref), rows = units[step - 2]
            alpha, p, psum, m_new = soft[step - 2]
            l_ref[rows, :] = alpha * l_ref[rows, :] + psum
            a_ref[rows, :] = alpha * a_ref[rows, :] + _dot(p, vt)
            m_ref[rows, :] = m_new
            soft[step - 2] = None

    @pl.when(j == pl.num_programs(3) - 1)
    def _():
        e = jnp.exp(jnp.sum(lq_ref[...] * lk_ref[...], axis=1, keepdims=True))
        lam = e[0:1] - e[1:2] + lam_init
        o = a0_ref[...] / l0_ref[...] - lam * (a1_ref[...] / l1_ref[...])
        o = o * lax.rsqrt(jnp.mean(o * o, axis=1, keepdims=True) + SUBLN_EPS) * sw_ref[...]
        o_ref[0] = o * (1.0 - lam_init)


def diff_attention(qa, qb, k, v, lq, lk, sw, lam_init, tq=512, tk=4096, q_rows=128):
    b, s, _ = qa.shape
    tq, tk = min(tq, s), min(tk, s)
    qspec = pl.BlockSpec((1, tq, LANES), lambda bi, h, i, j: (bi, i, h))
    kspec = pl.BlockSpec((1, tk, LANES), lambda bi, h, i, j: (bi, j, h))
    small = lambda shape: pl.BlockSpec(shape, lambda bi, h, i, j: (0, 0))
    acc = pltpu.VMEM((tq, LANES), F32)
    return pl.pallas_call(
        functools.partial(_attn_kernel, lam_init, min(q_rows, tq)),
        grid=(b, DIFF_HEADS, s // tq, s // tk),
        in_specs=[qspec, qspec, kspec, kspec, small((2, DIFF_QK)), small((2, DIFF_QK)), small((1, DIFF_V))],
        out_specs=pl.BlockSpec((1, tq, LANES), lambda bi, h, i, j: (bi, i, h)),
        out_shape=jax.ShapeDtypeStruct((b, s, D_DIFF), F32),
        scratch_shapes=[acc] * 6,
        compiler_params=_cparams(("parallel", "parallel", "parallel", "arbitrary")),
        name="diff_attention",
    )(qa, qb, k, v, lq, lk, sw.reshape(1, DIFF_V))


def _pool_kernel(seq, u_ref, up_ref, un_ref, mix_ref, scale_ref, o_ref):
    i = pl.program_id(1)
    n = pl.num_programs(1)
    u = u_ref[0]
    ts = u.shape[0]
    prev = jnp.where(i > 0, up_ref[0], 0.0)
    nxt = jnp.where(i < n - 1, un_ref[0], 0.0)
    ext = jnp.concatenate([prev, u, nxt], axis=0)
    rows = ext.shape[0]
    back = lambda t, sft: pltpu.roll(t, sft, 0)
    fwd = lambda t, sft: pltpu.roll(t, rows - sft, 0)
    w2 = ext + back(ext, 1)
    w4 = back(w2, 1) + fwd(w2, 1)
    w8 = back(w4, 2) + fwd(w4, 2)
    w16 = back(w8, 4) + fwd(w8, 4)
    lane = lax.broadcasted_iota(jnp.int32, (ts, D_POOL), 1)
    t = lax.broadcasted_iota(jnp.int32, (ts, D_POOL), 0) + i * ts
    grp = lane // HEAD_DIM
    sel = lambda vals: jnp.where(grp == 0, vals[0], jnp.where(grp == 1, vals[1], jnp.where(grp == 2, vals[2], vals[3])))
    wsum = sel([w[HALO:HALO + ts] for w in (w2, w4, w8, w16)])
    half = sel([wd // 2 for wd in POOL_WINDOWS])
    width = sel(list(POOL_WINDOWS))
    cnt = jnp.minimum(t + width - half, seq) - jnp.maximum(t - half, 0)
    pooled = wsum / cnt.astype(F32) - u
    o_ref[0] = _dot(pooled, mix_ref[...], precision=HI) * scale_ref[...]


def pool_mixer(proj, mix_bd, scale, ts=512):
    b, s, _ = proj.shape
    nblk = ts // HALO
    last = s // HALO - 1
    cb = COL_POOL // D_POOL
    hb = COL_POOL // D_POOL
    return pl.pallas_call(
        functools.partial(_pool_kernel, s),
        grid=(b, s // ts),
        in_specs=[
            pl.BlockSpec((1, ts, D_POOL), lambda bi, i: (bi, i, cb)),
            pl.BlockSpec((1, HALO, D_POOL), lambda bi, i: (bi, jnp.maximum(i * nblk - 1, 0), hb)),
            pl.BlockSpec((1, HALO, D_POOL), lambda bi, i: (bi, jnp.minimum((i + 1) * nblk, last), hb)),
            pl.BlockSpec((D_POOL, D_POOL), lambda bi, i: (0, 0)),
            pl.BlockSpec((1, D_POOL), lambda bi, i: (0, 0)),
        ],
        out_specs=pl.BlockSpec((1, ts, D_POOL), lambda bi, i: (bi, i, 0)),
        out_shape=jax.ShapeDtypeStruct((b, s, D_POOL), F32),
        compiler_params=_cparams(("parallel", "parallel")),
        name="pool_mixer",
    )(proj, proj, proj, mix_bd, scale.reshape(1, D_POOL))


def _out_proj_kernel(x_ref, yf_ref, yb_ref, bonus_ref, g_ref, att_ref, pool_ref, lw_ref, lb_ref, bd_ref, w_ref, o_ref):
    y = yf_ref[...] + yb_ref[...]
    bd = bd_ref[...]
    mu = _dot(y, bd, precision=HI) * (1.0 / HEAD_DIM)
    yc = y - mu
    var = _dot(yc * yc, bd, precision=HI) * (1.0 / HEAD_DIM)
    ya = (yc * lax.rsqrt(var + GN_EPS) * lw_ref[...] + lb_ref[...] + bonus_ref[...]) * g_ref[...]
    cat = jnp.concatenate([ya, att_ref[...], pool_ref[...]], axis=1).astype(BF16)
    o_ref[...] = x_ref[...] + _dot(cat, w_ref[...])


def out_proj(x2d, yf, yb, bonus, g, att, pool, lnx_w, lnx_b, bd, w_bf16, tm=512):
    t, d = x2d.shape
    rows = lambda n: pl.BlockSpec((tm, n), lambda i: (i, 0))
    fixed = lambda shape: pl.BlockSpec(shape, lambda i: (0, 0))
    return pl.pallas_call(
        _out_proj_kernel,
        grid=(t // tm,),
        in_specs=[rows(d),
                  rows(D_RWKV), rows(D_RWKV), rows(D_RWKV), rows(D_RWKV), rows(D_DIFF), rows(D_POOL),
                  fixed((1, D_RWKV)), fixed((1, D_RWKV)), fixed((D_RWKV, D_RWKV)), fixed((d, d))],
        out_specs=rows(d),
        out_shape=jax.ShapeDtypeStruct((t, d), F32),
        compiler_params=_cparams(("parallel",)),
        name="out_proj",
    )(x2d, yf, yb, bonus, g, att, pool, lnx_w.reshape(1, -1), lnx_b.reshape(1, -1), bd, w_bf16)


FF_TILE = 512


def _gate_up_blocks(w_gate, w_up):
    *lead, d, ff = w_gate.shape
    both = jnp.stack([w_gate.reshape(*lead, d, ff // FF_TILE, FF_TILE), w_up.reshape(*lead, d, ff // FF_TILE, FF_TILE)], axis=-2)
    return jnp.swapaxes(both.reshape(*lead, d, ff // FF_TILE, 2 * FF_TILE), -3, -2).astype(BF16)


def _swiglu_tile(x, wgu, wd):
    gu = _dot(x, wgu)
    gate, up = gu[:, :FF_TILE], gu[:, FF_TILE:]
    return _dot((gate * _sigmoid(gate) * up).astype(BF16), wd)


def _ffn_kernel(x_ref, g_ref, wgu_ref, wd_ref, o_ref, h_ref, acc_ref):
    f = pl.program_id(1)

    @pl.when(f == 0)
    def _():
        h_ref[...] = _rms(x_ref[...], g_ref[...]).astype(BF16)
        acc_ref[...] = jnp.zeros_like(acc_ref)

    acc_ref[...] += _swiglu_tile(h_ref[...], wgu_ref[0], wd_ref[...])

    @pl.when(f == pl.num_programs(1) - 1)
    def _():
        o_ref[...] = x_ref[...] + acc_ref[...]


def ffn_dense(x2d, g, wgu, wd, tm=1024):
    t, d = x2d.shape
    nf = wgu.shape[0]
    return pl.pallas_call(
        _ffn_kernel,
        grid=(t // tm, nf),
        in_specs=[pl.BlockSpec((tm, d), lambda i, f: (i, 0)),
                  pl.BlockSpec((1, d), lambda i, f: (0, 0)),
                  pl.BlockSpec((1, d, 2 * FF_TILE), lambda i, f: (f, 0, 0)),
                  pl.BlockSpec((FF_TILE, d), lambda i, f: (f, 0))],
        out_specs=pl.BlockSpec((tm, d), lambda i, f: (i, 0)),
        out_shape=jax.ShapeDtypeStruct((t, d), F32),
        scratch_shapes=[pltpu.VMEM((tm, d), BF16), pltpu.VMEM((tm, d), F32)],
        compiler_params=_cparams(("parallel", "arbitrary")),
        name="ffn_dense",
    )(x2d, g.reshape(1, d), wgu, wd)


def _router_kernel(x_ref, g_ref, wr_ref, h_ref, comb_ref, rank_ref, cnt_ref):
    h = _rms(x_ref[...], g_ref[...])
    h_ref[...] = h.astype(BF16)
    logits = _dot(h, wr_ref[...], precision=HI)
    lane = lax.broadcasted_iota(jnp.int32, logits.shape, 1)
    logits = jnp.where(lane < N_EXPERTS, logits, -jnp.inf)
    v1 = jnp.max(logits, axis=1, keepdims=True)
    i1 = jnp.min(jnp.where(logits == v1, lane, LANES), axis=1, keepdims=True)
    rest = jnp.where(lane == i1, -jnp.inf, logits)
    v2 = jnp.max(rest, axis=1, keepdims=True)
    i2 = jnp.min(jnp.where(rest == v2, lane, LANES), axis=1, keepdims=True)
    e2 = jnp.exp(v2 - v1)
    g1 = 1.0 / (1.0 + e2)
    comb = jnp.where(lane == i1, g1, jnp.where(lane == i2, e2 * g1, 0.0))
    comb_ref[...] = comb
    tm = comb.shape[0]
    routed = comb > 0.0
    mask = jnp.where(routed, 1.0, 0.0).astype(BF16)
    earlier = lax.broadcasted_iota(jnp.int32, (tm, tm), 1) < lax.broadcasted_iota(jnp.int32, (tm, tm), 0)
    rank = _dot(jnp.where(earlier, 1.0, 0.0).astype(BF16), mask)
    rank_ref[...] = jnp.where(routed, rank, -1.0)
    cnt_ref[0] = _dot(jnp.ones((HALO, tm), BF16), mask)


def moe_router(x2d, g, wr_pad, tm):
    t, d = x2d.shape
    tok = pl.BlockSpec((tm, LANES), lambda i: (i, 0))
    return pl.pallas_call(
        _router_kernel,
        grid=(t // tm,),
        in_specs=[pl.BlockSpec((tm, d), lambda i: (i, 0)),
                  pl.BlockSpec((1, d), lambda i: (0, 0)),
                  pl.BlockSpec((d, LANES), lambda i: (0, 0))],
        out_specs=[pl.BlockSpec((tm, d), lambda i: (i, 0)), tok, tok,
                   pl.BlockSpec((1, HALO, LANES), lambda i: (i, 0, 0))],
        out_shape=[jax.ShapeDtypeStruct((t, d), BF16), jax.ShapeDtypeStruct((t, LANES), F32),
                   jax.ShapeDtypeStruct((t, LANES), F32), jax.ShapeDtypeStruct((t // tm, HALO, LANES), F32)],
        compiler_params=_cparams(("parallel",)),
        name="moe_router",
    )(x2d, g.reshape(1, d), wr_pad)


def _moe_kernel(cnt_ref, x_ref, h_ref, comb_ref, rank_ref, rank_t_ref, wgu_ref, wd_ref, gout_ref, o_ref,
                acc_ref, xe_ref, ye_ref, gc_ref):
    i, e, f = pl.program_id(0), pl.program_id(1), pl.program_id(2)
    tm = h_ref.shape[0]
    cap = xe_ref.shape[1]
    nblk = (cnt_ref[i, e] + cap - 1) // cap
    onehot = (lax.broadcasted_iota(jnp.int32, (LANES, LANES), 0) == e).astype(F32)

    @pl.when(jnp.logical_and(e == 0, f == 0))
    def _():
        acc_ref[...] = jnp.zeros_like(acc_ref)

    @pl.when(f == 0)
    def _():
        gate_col = _dot(comb_ref[...], onehot, precision=HI)
        slot_row = rank_t_ref[0]

        def gather(j, carry):
            rid = lax.broadcasted_iota(jnp.int32, (cap, tm), 0) + j * cap
            pick = jnp.where(slot_row == rid.astype(F32), 1.0, 0.0)
            xe_ref[j] = _dot(pick.astype(BF16), h_ref[...]).astype(BF16)
            gc_ref[j] = _dot(pick, gate_col, precision=HI)
            return carry

        lax.fori_loop(0, nblk, gather, 0)

    def expert(j, carry):
        y = _swiglu_tile(xe_ref[j], wgu_ref[0, 0], wd_ref[0])
        ye_ref[j] = jnp.where(f == 0, y, ye_ref[j] + y)
        return carry

    lax.fori_loop(0, nblk, expert, 0)

    @pl.when(f == pl.num_programs(2) - 1)
    def _():
        slot_col = _dot(rank_ref[...], onehot, precision=HI)
        slot_col = jnp.concatenate([slot_col] * (cap // LANES), axis=1)

        def scatter(j, carry):
            cid = lax.broadcasted_iota(jnp.int32, (tm, cap), 1) + j * cap
            place = jnp.where(slot_col == cid.astype(F32), 1.0, 0.0).astype(BF16)
            ys = ye_ref[j] * jnp.concatenate([gc_ref[j]] * (ye_ref.shape[2] // LANES), axis=1)
            acc_ref[...] += _dot(place, ys.astype(BF16))
            return carry

        lax.fori_loop(0, nblk, scatter, 0)

    @pl.when(jnp.logical_and(e == pl.num_programs(1) - 1, f == pl.num_programs(2) - 1))
    def _():
        o_ref[...] = _rms(x_ref[...] + acc_ref[...], gout_ref[...])


def moe_ffn(x2d, h, comb, rank, rank_t, cnt, wgu, wd, g_out, tm, cap=384):
    t, d = x2d.shape
    ne, nf = wgu.shape[:2]
    max_blocks = -(-tm // cap)
    tok = lambda n: pl.BlockSpec((tm, n), lambda i, e, f, c: (i, 0))
    grid_spec = pltpu.PrefetchScalarGridSpec(
        num_scalar_prefetch=1,
        grid=(t // tm, ne, nf),
        in_specs=[tok(d), tok(d), tok(LANES), tok(LANES),
                  pl.BlockSpec((1, 1, tm), lambda i, e, f, c: (e, 0, i)),
                  pl.BlockSpec((1, 1, d, 2 * FF_TILE), lambda i, e, f, c: (e, f, 0, 0)),
                  pl.BlockSpec((1, FF_TILE, d), lambda i, e, f, c: (e, f, 0)),
                  pl.BlockSpec((1, d), lambda i, e, f, c: (0, 0))],
        out_specs=tok(d),
        scratch_shapes=[pltpu.VMEM((tm, d), F32), pltpu.VMEM((max_blocks, cap, d), BF16),
                        pltpu.VMEM((max_blocks, cap, d), F32), pltpu.VMEM((max_blocks, cap, LANES), F32)],
    )
    return pl.pallas_call(
        _moe_kernel,
        grid_spec=grid_spec,
        out_shape=jax.ShapeDtypeStruct((t, d), F32),
        compiler_params=_cparams(("parallel", "arbitrary", "arbitrary")),
        name="moe_ffn",
    )(cnt, x2d, h, comb, rank, rank_t, wgu, wd, g_out.reshape(1, d))


def _block_diag_ones(n, blk):
    i = jnp.arange(n) // blk
    return (i[:, None] == i[None, :]).astype(F32)


def _rope_tables(positions):
    half = ROT_DIM // 2
    inv_freq = jnp.power(ROPE_THETA, -(jnp.arange(half, dtype=F32) * 2.0 / ROT_DIM))
    ang = positions.astype(F32)[..., None] * inv_freq
    cos, sin = jnp.cos(ang), jnp.sin(ang)
    one = jnp.ones(cos.shape[:-1] + (DIFF_QK - ROT_DIM,), F32)
    zero = jnp.zeros_like(one)
    z8 = jnp.zeros_like(sin)
    cs = jnp.concatenate([cos, cos, one], axis=-1)
    s1 = jnp.concatenate([z8, sin, zero], axis=-1)
    s2 = jnp.concatenate([-sin, z8, zero], axis=-1)
    return tuple(jnp.concatenate([t, t], axis=-1) for t in (cs, s1, s2))


def kernel(x, positions, norm_mix, w_in_first, w_in_rest, tshift, decay_bias, decay_up, iclr_bias, iclr_up, gate_up, k_k, k_a, r_k, lnx_w, lnx_b, vres_bias, vres_up, lambda_q, lambda_k, subln_w, pool_mix, pool_scale, w_out, norm_ffn, ffn_gate, ffn_up, ffn_down, router, exp_gate, exp_up, exp_down, norm_out):
    bsz, seq, d = x.shape
    depth = norm_mix.shape[0]
    x2d = x.reshape(bsz * seq, d)
    bd = _block_diag_ones(D_RWKV, HEAD_DIM)
    cs, s1, s2 = _rope_tables(positions)
    v_first = None
    for l in range(depth):
        if l == 0:
            w_in = w_in_first
        else:
            wr = w_in_rest[l - 1]
            w_in = jnp.concatenate([wr[:, :D_SHIFT], wr[:, D_SHIFT + MV_LORA:], wr[:, D_SHIFT:D_SHIFT + MV_LORA],
                                    jnp.zeros((d, LANES - MV_LORA), F32)], axis=1)
        proj = norm_matmul(x2d, norm_mix[l], w_in.astype(BF16)).reshape(bsz, seq, -1)

        zeros = jnp.zeros((DECAY_LORA, 2 * D_RWKV), F32)
        dec = jnp.concatenate([decay_up[l, 0], decay_up[l, 1]], axis=1)
        icl = jnp.concatenate([iclr_up[l, 0], iclr_up[l, 1]], axis=1)
        w_lora = jnp.concatenate([
            jnp.concatenate([dec, zeros], axis=1), jnp.zeros((AAA_LORA, 4 * D_RWKV), F32),
            jnp.zeros((DECAY_LORA, 4 * D_RWKV), F32), jnp.concatenate([zeros, icl], axis=1)], axis=0)
        prm = dict(tshift=tshift[l], w_lora=w_lora,
                   b_lora=jnp.concatenate([decay_bias[l, 0], decay_bias[l, 1], iclr_bias[l, 0], iclr_bias[l, 1]]).reshape(1, -1),
                   gate_up=gate_up[l], k_k=k_k[l].reshape(1, -1), k_a=k_a[l].reshape(1, -1),
                   r_k=r_k[l].reshape(1, -1), bd=bd)
        if l > 0:
            prm["vres_bias"] = vres_bias[l - 1].reshape(1, -1)
            prm["vres_up"] = jnp.concatenate([vres_up[l - 1], jnp.zeros((LANES - MV_LORA, D_RWKV), F32)], axis=0)
        r, v, kk, lw, kd, bb, g, bonus = rwkv_prep(proj, v_first, prm)
        if l == 0:
            v_first = v
        yf, yb = wkv_scan(r, v, kk, lw, kd, bb)

        lam_init = 0.8 - 0.6 * math.exp(-0.3 * l)
        qa, qb, kr, vb = attn_prep(proj, cs, s1, s2)
        att = diff_attention(qa, qb, kr, vb, lambda_q[l], lambda_k[l], subln_w[l], lam_init)

        mix_bd = jax.scipy.linalg.block_diag(*[pool_mix[l, gi] for gi in range(len(POOL_WINDOWS))])
        pool = pool_mixer(proj, mix_bd, pool_scale[l])

        flat = lambda t: t.reshape(bsz * seq, -1)
        x2d = out_proj(x2d, flat(yf), flat(yb), flat(bonus), flat(g), flat(att), flat(pool), lnx_w[l], lnx_b[l], bd,
                       w_out[l].astype(BF16))
        i = l // 2
        if l % 2 == 0:
            x2d = ffn_dense(x2d, norm_ffn[l], _gate_up_blocks(ffn_gate[i], ffn_up[i]), ffn_down[i].astype(BF16))
            if l == depth - 1:
                raise NotImplementedError("final norm is fused into the routed mixer")
        else:
            wr_pad = jnp.concatenate([router[i], jnp.zeros((d, LANES - N_EXPERTS), F32)], axis=1)
            tm = min(1024, bsz * seq)
            h, comb, rank, cnt = moe_router(x2d, norm_ffn[l], wr_pad, tm)
            rank_t = rank[:, :N_EXPERTS].T.reshape(N_EXPERTS, 1, -1)
            cnt = cnt[:, 0, :N_EXPERTS].astype(jnp.int32)
            x2d = moe_ffn(x2d, h, comb, rank, rank_t, cnt, _gate_up_blocks(exp_gate[i], exp_up[i]),
                          exp_down[i].astype(BF16), norm_out, tm)
    return x2d.reshape(bsz, seq, d)
```

```python
import functools
import math

import jax
import jax.numpy as jnp
from jax import lax
from jax.experimental import pallas as pl
from jax.experimental.pallas import tpu as pltpu

F32 = jnp.float32
BF16 = jnp.bfloat16
HI = lax.Precision.HIGHEST

D_MODEL = 1024
RWKV_HEADS = 4
HEAD_DIM = 64
D_RWKV = RWKV_HEADS * HEAD_DIM
DECAY_LORA = 64
AAA_LORA = 64
MV_LORA = 32
GATE_LORA = 128
D_SHIFT = 3 * D_RWKV + DECAY_LORA + AAA_LORA + GATE_LORA
W_DECAY_SCALE = 0.6065306597126334
GN_EPS = 64e-5
DIFF_HEADS = 4
DIFF_QK = 64
DIFF_V = 128
D_DIFF_QK = DIFF_HEADS * 2 * DIFF_QK
D_DIFF = DIFF_HEADS * DIFF_V
ROT_DIM = DIFF_QK // 4
ROPE_THETA = 500000.0
SUBLN_EPS = 1e-5
POOL_WINDOWS = (2, 4, 8, 16)
D_POOL = 256
D_FF = 3584
N_EXPERTS = 8
NORM_EPS = 1e-6
P_IN = D_SHIFT + 2 * D_DIFF_QK + D_DIFF + D_POOL
COL_Q, COL_K, COL_V, COL_POOL, COL_VDOWN = 1024, 1536, 2048, 2560, 2816

LANES = 128
HALO = 8
VMEM_LIMIT = 56 * 1024 * 1024

SCAN_CHUNK = 128


def _cparams(sem):
    return pltpu.CompilerParams(dimension_semantics=sem, vmem_limit_bytes=VMEM_LIMIT)


def _sigmoid(x):
    return 1.0 / (1.0 + jnp.exp(-x))


def _dot(a, b, **kw):
    return jnp.dot(a, b, preferred_element_type=F32, **kw)


def _bdot(a, b):
    return jnp.dot(a.astype(BF16), b.astype(BF16), preferred_element_type=F32)


def _rms(x, g):
    return x * lax.rsqrt(jnp.mean(x * x, axis=-1, keepdims=True) + NORM_EPS) * g


def _norm_matmul_kernel(x_ref, g_ref, w_ref, o_ref):
    h = _rms(x_ref[...], g_ref[...]).astype(BF16)
    o_ref[...] = _dot(h, w_ref[...])


def norm_matmul(x2d, g, w_bf16, tm=512):
    t, d = x2d.shape
    n = w_bf16.shape[1]
    return pl.pallas_call(
        _norm_matmul_kernel,
        grid=(t // tm,),
        in_specs=[
            pl.BlockSpec((tm, d), lambda i: (i, 0)),
            pl.BlockSpec((1, d), lambda i: (0, 0)),
            pl.BlockSpec((d, n), lambda i: (0, 0)),
        ],
        out_specs=pl.BlockSpec((tm, n), lambda i: (i, 0)),
        out_shape=jax.ShapeDtypeStruct((t, n), F32),
        compiler_params=_cparams(("parallel",)),
        name="norm_matmul",
    )(x2d, g.reshape(1, d), w_bf16)


def _shifted(u, prev_row, next_row):
    ts = u.shape[0]
    row = lax.broadcasted_iota(jnp.int32, u.shape, 0)
    prev = jnp.where(row == 0, prev_row, pltpu.roll(u, 1, 0))
    nxt = jnp.where(row == ts - 1, next_row, pltpu.roll(u, ts - 1, 0))
    return prev, nxt


def _rwkv_prep_kernel(has_vres, *refs):
    if has_vres:
        (u_ref, up_ref, un_ref, vd_ref, vf_ref, vb_ref, vu_ref, mu_ref, wl_ref, bias_ref, gu_ref,
         kkw_ref, ka_ref, rk_ref, bd_ref,
         r_ref, v_ref, kk_ref, lw_ref, kd_ref, bb_ref, g_ref, bonus_ref) = refs
    else:
        (u_ref, up_ref, un_ref, mu_ref, wl_ref, bias_ref, gu_ref,
         kkw_ref, ka_ref, rk_ref, bd_ref,
         r_ref, v_ref, kk_ref, lw_ref, kd_ref, bb_ref, g_ref, bonus_ref) = refs
    i = pl.program_id(1)
    n = pl.num_programs(1)
    u = u_ref[0]
    prev_row = jnp.where(i > 0, up_ref[0, HALO - 1:HALO, :], 0.0)
    next_row = jnp.where(i < n - 1, un_ref[0, 0:1, :], 0.0)
    prev, nxt = _shifted(u, prev_row, next_row)
    us = u + mu_ref[0:1, :] * (prev - u) + mu_ref[1:2, :] * (nxt - u)
    r = us[:, 0:256]
    k = us[:, 256:512]
    v = us[:, 512:768]
    xwa = us[:, 768:896]
    xg = us[:, 896:1024]
    lhs = jnp.concatenate([jnp.tanh(xwa), xwa], axis=1)
    z = _dot(lhs, wl_ref[...], precision=HI) + bias_ref[...]
    if has_vres:
        mix = _sigmoid(vb_ref[...] + _dot(vd_ref[0], vu_ref[...], precision=HI))
        v = v + (vf_ref[0] - v) * mix
    g = _dot(_sigmoid(xg), gu_ref[...], precision=HI)
    bd = bd_ref[...]
    kkr = k * kkw_ref[...]
    ss = _dot(kkr * kkr, bd, precision=HI)
    kk = kkr * lax.rsqrt(jnp.maximum(ss, 1e-24))
    r_ref[0] = r
    v_ref[0] = v
    kk_ref[0] = kk
    g_ref[0] = g
    bonus = jnp.zeros_like(v)
    for d in range(2):
        lw = -W_DECAY_SCALE * _sigmoid(z[:, 256 * d:256 * d + 256])
        a = _sigmoid(z[:, 512 + 256 * d:768 + 256 * d])
        kd = k * (1.0 + (a - 1.0) * ka_ref[...])
        lw_ref[d, 0] = lw
        kd_ref[d, 0] = kd
        bb_ref[d, 0] = kk * a
        bonus = bonus + _dot(r * kd * rk_ref[...], bd, precision=HI) * v
    bonus_ref[0] = bonus


def rwkv_prep(proj, v_first, p, ts=512):
    b, s, _ = proj.shape
    has_vres = v_first is not None
    nblk = ts // HALO
    last = s // HALO - 1
    row = lambda shape: pl.BlockSpec(shape, lambda bi, i: (0,) * len(shape))
    in_specs = [
        pl.BlockSpec((1, ts, D_SHIFT), lambda bi, i: (bi, i, 0)),
        pl.BlockSpec((1, HALO, D_SHIFT), lambda bi, i: (bi, jnp.maximum(i * nblk - 1, 0), 0)),
        pl.BlockSpec((1, HALO, D_SHIFT), lambda bi, i: (bi, jnp.minimum((i + 1) * nblk, last), 0)),
    ]
    args = [proj, proj, proj]
    if has_vres:
        in_specs += [
            pl.BlockSpec((1, ts, LANES), lambda bi, i: (bi, i, COL_VDOWN // LANES)),
            pl.BlockSpec((1, ts, D_RWKV), lambda bi, i: (bi, i, 0)),
            row((1, D_RWKV)), row((LANES, D_RWKV)),
        ]
        args += [proj, v_first, p["vres_bias"], p["vres_up"]]
    in_specs += [row((2, D_SHIFT)), row((256, 1024)), row((1, 1024)), row((GATE_LORA, D_RWKV)),
                 row((1, D_RWKV)), row((1, D_RWKV)), row((1, D_RWKV)), row((D_RWKV, D_RWKV))]
    args += [p["tshift"], p["w_lora"], p["b_lora"], p["gate_up"], p["k_k"], p["k_a"], p["r_k"], p["bd"]]
    one = pl.BlockSpec((1, ts, D_RWKV), lambda bi, i: (bi, i, 0))
    two = pl.BlockSpec((2, 1, ts, D_RWKV), lambda bi, i: (0, bi, i, 0))
    s1 = jax.ShapeDtypeStruct((b, s, D_RWKV), F32)
    s2 = jax.ShapeDtypeStruct((2, b, s, D_RWKV), F32)
    return pl.pallas_call(
        functools.partial(_rwkv_prep_kernel, has_vres),
        grid=(b, s // ts),
        in_specs=in_specs,
        out_specs=[one, one, one, two, two, two, one, one],
        out_shape=[s1, s1, s1, s2, s2, s2, s1, s1],
        compiler_params=_cparams(("parallel", "parallel")),
        name="rwkv_prep",
    )(*args)


def _unit_lower_inverses(ms, eye):
    xs = [eye - m for m in ms]
    ps = [_bdot(m, m) for m in ms]
    levels = int(math.log2(ms[0].shape[0])) - 1
    for k in range(levels):
        nxt = [_bdot(p, p) for p in ps] if k < levels - 1 else None
        xs = [x + _bdot(x, p) for x, p in zip(xs, ps)]
        ps = nxt
    return xs


def _split3(x):
    hi = x.astype(BF16)
    r1 = x - hi.astype(F32)
    mid = r1.astype(BF16)
    lo = (r1 - mid.astype(F32)).astype(BF16)
    return hi, mid, lo


def _chunk_terms(rev, r, v, kk, lw, kd, bb):
    c = SCAN_CHUNK
    n = len(r)
    ti = lax.broadcasted_iota(jnp.int32, (c, c), 0)
    tj = lax.broadcasted_iota(jnp.int32, (c, c), 1)
    eye = (ti == tj).astype(F32)
    same_head = (ti // HEAD_DIM) == (tj // HEAD_DIM)
    first = lax.broadcasted_iota(jnp.int32, (c, LANES), 1) < HEAD_DIM
    first2 = jnp.concatenate([first, first], axis=1)
    strict = [(ti - tj if rv else tj - ti) < 0 for rv in rev]
    incl = [(ti - tj if rv else tj - ti) <= 0 for rv in rev]

    cum = []
    for i in range(n):
        cum3 = _dot(jnp.where(incl[i], 1.0, 0.0).astype(BF16), jnp.concatenate(_split3(lw[i]), axis=1))
        cum.append(cum3[:, :LANES] + cum3[:, LANES:2 * LANES] + cum3[:, 2 * LANES:])
    a = [kk[i] * jnp.exp(cum[i] - lw[i]) for i in range(n)]
    rt = [r[i] * jnp.exp(cum[i]) for i in range(n)]
    einv = [jnp.exp(-cm) for cm in cum]
    kt_t = [(kd[i] * einv[i]).T for i in range(n)]
    bt_t = [(bb[i] * einv[i]).T for i in range(n)]
    last = [cum[i][0:1] if rev[i] else cum[i][c - 1:c] for i in range(n)]
    g_all = [jnp.exp(jnp.broadcast_to(x, (c, LANES))).T for x in last]
    vb = [x.astype(BF16) for x in v]
    rhs = [jnp.concatenate([bt_t[i], kt_t[i]], axis=1).astype(BF16) for i in range(n)]
    heads = [(i, hm) for i in range(n) for hm in (first, jnp.logical_not(first))]
    gm = [_dot(jnp.concatenate([jnp.where(hm, a[i], 0.0), jnp.where(hm, rt[i], 0.0)], axis=0).astype(BF16), rhs[i])
          for i, hm in heads]
    t_inv = _unit_lower_inverses([jnp.where(strict[i], g[:c, :c], 0.0) for (i, _), g in zip(heads, gm)], eye)
    mv = [_dot(jnp.where(strict[i], g[:c, c:], 0.0).astype(BF16), vb[i]) for (i, _), g in zip(heads, gm)]
    tam = [_bdot(t, jnp.concatenate([a[i], m], axis=1)) for (i, _), t, m in zip(heads, t_inv, mv)]
    lt = [_bdot(jnp.where(incl[i], g[c:, :c], 0.0), x) for (i, _), g, x in zip(heads, gm, tam)]
    lv = [_dot(jnp.where(incl[i], g[c:, c:], 0.0).astype(BF16), vb[i]) for (i, _), g in zip(heads, gm)]
    tam = [jnp.where(first2, tam[2 * i], tam[2 * i + 1]) for i in range(n)]
    lt = [jnp.where(first2, lt[2 * i], lt[2 * i + 1]) for i in range(n)]
    lv = [jnp.where(first, lv[2 * i], lv[2 * i + 1]) for i in range(n)]
    btx = [_bdot(bt_t[i], tam[i]) for i in range(n)]
    kv = [_dot(kt_t[i].astype(BF16), vb[i]) for i in range(n)]
    phi = [g_all[i] * (eye - jnp.where(same_head, btx[i][:, :LANES], 0.0)) for i in range(n)]
    psi = [g_all[i] * jnp.where(same_head, kv[i] - btx[i][:, LANES:], 0.0) for i in range(n)]
    ra = [rt[i] - lt[i][:, :LANES] for i in range(n)]
    oc = [lv[i] - lt[i][:, LANES:] for i in range(n)]
    return phi, psi, ra, oc


def _wkv_kernel(rf_ref, vf_ref, kkf_ref, lwf_ref, kdf_ref, bbf_ref,
                rb_ref, vb_ref, kkb_ref, lwb_ref, kdb_ref, bbb_ref, yf_ref, yb_ref, h_ref):
    @pl.when(pl.program_id(1) == 0)
    def _():
        h_ref[...] = jnp.zeros_like(h_ref)

    pairs = (slice(0, LANES), slice(LANES, 2 * LANES))
    probs = [(False, sl, (rf_ref, vf_ref, kkf_ref), (lwf_ref, kdf_ref, bbf_ref), yf_ref) for sl in pairs]
    probs += [(True, sl, (rb_ref, vb_ref, kkb_ref), (lwb_ref, kdb_ref, bbb_ref), yb_ref) for sl in pairs]
    rev = [pr[0] for pr in probs]
    r, v, kk = ([pr[2][j][0, :, pr[1]] for pr in probs] for j in range(3))
    lw, kd, bb = ([pr[3][j][0, 0, :, pr[1]] for pr in probs] for j in range(3))
    phi, psi, ra, oc = _chunk_terms(rev, r, v, kk, lw, kd, bb)
    res = []
    for i in range(len(probs)):
        h = h_ref[i]
        h_hi = h.astype(BF16)
        h_lo = (h - h_hi.astype(F32)).astype(BF16)
        lhs = jnp.concatenate([phi[i], ra[i]], axis=0).astype(BF16)
        res.append(_dot(jnp.concatenate([lhs, lhs], axis=1), jnp.concatenate([h_hi, h_lo], axis=0)))
    for i, pr in enumerate(probs):
        h_ref[i] = res[i][:SCAN_CHUNK] + psi[i]
        pr[4][0, :, pr[1]] = res[i][SCAN_CHUNK:] + oc[i]


def wkv_scan(r, v, kk, lw, kd, bb):
    b, s, _ = r.shape
    c = SCAN_CHUNK
    nc = s // c
    fwd = pl.BlockSpec((1, c, D_RWKV), lambda bi, ci: (bi, ci, 0))
    bwd = pl.BlockSpec((1, c, D_RWKV), lambda bi, ci: (bi, nc - 1 - ci, 0))
    fwd_d = pl.BlockSpec((1, 1, c, D_RWKV), lambda bi, ci: (0, bi, ci, 0))
    bwd_d = pl.BlockSpec((1, 1, c, D_RWKV), lambda bi, ci: (1, bi, nc - 1 - ci, 0))
    out = jax.ShapeDtypeStruct((b, s, D_RWKV), F32)
    return pl.pallas_call(
        _wkv_kernel,
        grid=(b, nc),
        in_specs=[fwd, fwd, fwd, fwd_d, fwd_d, fwd_d, bwd, bwd, bwd, bwd_d, bwd_d, bwd_d],
        out_specs=[fwd, bwd],
        out_shape=[out, out],
        scratch_shapes=[pltpu.VMEM((4, LANES, LANES), F32)],
        compiler_params=_cparams(("parallel", "arbitrary")),
        name="wkv_scan",
    )(r, v, kk, lw, kd, bb, r, v, kk, lw, kd, bb)


def _attn_prep_kernel(q_ref, k_ref, v_ref, c_ref, s1_ref, s2_ref, qa_ref, qb_ref, ko_ref, vo_ref):
    cs, s1, s2 = c_ref[0], s1_ref[0], s2_ref[0]
    lane = lax.broadcasted_iota(jnp.int32, cs.shape, 1)
    first = lane < DIFF_QK
    scale = DIFF_QK ** -0.5 * math.log2(math.e)
    for j in range(D_DIFF_QK // LANES):
        sl = slice(LANES * j, LANES * j + LANES)

        def rope(t):
            return t * cs + pltpu.roll(t, ROT_DIM // 2, 1) * s1 + pltpu.roll(t, LANES - ROT_DIM // 2, 1) * s2

        q = rope(q_ref[0, :, sl]) * scale
        qa_ref[0, :, sl] = jnp.where(first, q, 0.0).astype(BF16)
        qb_ref[0, :, sl] = jnp.where(first, 0.0, q).astype(BF16)
        ko_ref[0, :, sl] = rope(k_ref[0, :, sl]).astype(BF16)
    ones = jnp.ones((v_ref.shape[1], DIFF_V), BF16)
    for h in range(DIFF_HEADS):
        vo_ref[0, :, 2 * DIFF_V * h:2 * DIFF_V * h + DIFF_V] = v_ref[0, :, DIFF_V * h:DIFF_V * h + DIFF_V].astype(BF16)
        vo_ref[0, :, 2 * DIFF_V * h + DIFF_V:2 * DIFF_V * (h + 1)] = ones


def attn_prep(proj, cs, s1, s2, ts=512):
    b, s, _ = proj.shape
    col = lambda off: pl.BlockSpec((1, ts, 512), lambda bi, i: (bi, i, off // 512))
    tab = pl.BlockSpec((1, ts, LANES), lambda bi, i: (bi, i, 0))
    out = pl.BlockSpec((1, ts, 512), lambda bi, i: (bi, i, 0))
    sh = jax.ShapeDtypeStruct((b, s, 512), BF16)
    return pl.pallas_call(
        _attn_prep_kernel,
        grid=(b, s // ts),
        in_specs=[col(COL_Q), col(COL_K), col(COL_V), tab, tab, tab],
        out_specs=[out, out, out, pl.BlockSpec((1, ts, 2 * D_DIFF), lambda bi, i: (bi, i, 0))],
        out_shape=[sh, sh, sh, jax.ShapeDtypeStruct((b, s, 2 * D_DIFF), BF16)],
        compiler_params=_cparams(("parallel", "parallel")),
        name="attn_prep",
    )(proj, proj, proj, cs, s1, s2)


def _attn_kernel(lam_init, q_rows, qa_ref, qb_ref, k_ref, v_ref, lq_ref, lk_ref, sw_ref, o_ref):
    kt = k_ref[0]
    vt = v_ref[0]
    e = jnp.exp(jnp.sum(lq_ref[...] * lk_ref[...], axis=1, keepdims=True))
    lam = e[0:1] - e[1:2] + lam_init
    blocks = [pl.ds(r0, q_rows) for r0 in range(0, qa_ref.shape[1], q_rows)]
    n = len(blocks)
    s, p = [None] * n, [None] * n
    for step in range(n + 2):
        if step < n:
            q2 = jnp.concatenate([qa_ref[0, blocks[step], :], qb_ref[0, blocks[step], :]], axis=0)
            s[step] = lax.dot_general(q2, kt, (((1,), (1,)), ((), ())), preferred_element_type=F32)
        if 0 <= step - 1 < n:
            p[step - 1] = jnp.exp2(s[step - 1] - jnp.max(s[step - 1], axis=1, keepdims=True)).astype(BF16)
            s[step - 1] = None
        if 0 <= step - 2 < n:
            pv = _dot(p[step - 2], vt)
            ratio = pv[:, :DIFF_V] / pv[:, DIFF_V:]
            o = ratio[:q_rows] - lam * ratio[q_rows:]
            o = o * lax.rsqrt(jnp.mean(o * o, axis=1, keepdims=True) + SUBLN_EPS) * sw_ref[...]
            o_ref[0, blocks[step - 2], :] = o * (1.0 - lam_init)
            p[step - 2] = None


def diff_attention(qa, qb, k, v, lq, lk, sw, lam_init, tq=512, q_rows=128):
    b, s, _ = qa.shape
    tq = min(tq, s)
    qspec = pl.BlockSpec((1, tq, LANES), lambda bi, h, i: (bi, i, h))
    kspec = pl.BlockSpec((1, s, LANES), lambda bi, h, i: (bi, 0, h))
    small = lambda shape: pl.BlockSpec(shape, lambda bi, h, i: (0, 0))
    return pl.pallas_call(
        functools.partial(_attn_kernel, lam_init, min(q_rows, tq)),
        grid=(b, DIFF_HEADS, s // tq),
        in_specs=[qspec, qspec, kspec, pl.BlockSpec((1, s, 2 * DIFF_V), lambda bi, h, i: (bi, 0, h)),
                  small((2, DIFF_QK)), small((2, DIFF_QK)), small((1, DIFF_V))],
        out_specs=qspec,
        out_shape=jax.ShapeDtypeStruct((b, s, D_DIFF), F32),
        compiler_params=_cparams(("parallel", "parallel", "parallel")),
        name="diff_attention",
    )(qa, qb, k, v, lq, lk, sw.reshape(1, DIFF_V))


def _pool_kernel(seq, u_ref, up_ref, un_ref, mix_ref, scale_ref, o_ref):
    i = pl.program_id(1)
    n = pl.num_programs(1)
    u = u_ref[0]
    ts = u.shape[0]
    prev = jnp.where(i > 0, up_ref[0], 0.0)
    nxt = jnp.where(i < n - 1, un_ref[0], 0.0)
    ext = jnp.concatenate([prev, u, nxt], axis=0)
    rows = ext.shape[0]
    back = lambda t, sft: pltpu.roll(t, sft, 0)
    fwd = lambda t, sft: pltpu.roll(t, rows - sft, 0)
    w2 = ext + back(ext, 1)
    w4 = back(w2, 1) + fwd(w2, 1)
    w8 = back(w4, 2) + fwd(w4, 2)
    w16 = back(w8, 4) + fwd(w8, 4)
    lane = lax.broadcasted_iota(jnp.int32, (ts, D_POOL), 1)
    t = lax.broadcasted_iota(jnp.int32, (ts, D_POOL), 0) + i * ts
    grp = lane // HEAD_DIM
    sel = lambda vals: jnp.where(grp == 0, vals[0], jnp.where(grp == 1, vals[1], jnp.where(grp == 2, vals[2], vals[3])))
    wsum = sel([w[HALO:HALO + ts] for w in (w2, w4, w8, w16)])
    half = sel([wd // 2 for wd in POOL_WINDOWS])
    width = sel(list(POOL_WINDOWS))
    cnt = jnp.minimum(t + width - half, seq) - jnp.maximum(t - half, 0)
    pooled = wsum / cnt.astype(F32) - u
    o_ref[0] = _dot(pooled, mix_ref[...], precision=HI) * scale_ref[...]


def pool_mixer(proj, mix_bd, scale, ts=512):
    b, s, _ = proj.shape
    nblk = ts // HALO
    last = s // HALO - 1
    cb = COL_POOL // D_POOL
    hb = COL_POOL // D_POOL
    return pl.pallas_call(
        functools.partial(_pool_kernel, s),
        grid=(b, s // ts),
        in_specs=[
            pl.BlockSpec((1, ts, D_POOL), lambda bi, i: (bi, i, cb)),
            pl.BlockSpec((1, HALO, D_POOL), lambda bi, i: (bi, jnp.maximum(i * nblk - 1, 0), hb)),
            pl.BlockSpec((1, HALO, D_POOL), lambda bi, i: (bi, jnp.minimum((i + 1) * nblk, last), hb)),
            pl.BlockSpec((D_POOL, D_POOL), lambda bi, i: (0, 0)),
            pl.BlockSpec((1, D_POOL), lambda bi, i: (0, 0)),
        ],
        out_specs=pl.BlockSpec((1, ts, D_POOL), lambda bi, i: (bi, i, 0)),
        out_shape=jax.ShapeDtypeStruct((b, s, D_POOL), F32),
        compiler_params=_cparams(("parallel", "parallel")),
        name="pool_mixer",
    )(proj, proj, proj, mix_bd, scale.reshape(1, D_POOL))


def _out_proj_kernel(x_ref, yf_ref, yb_ref, bonus_ref, g_ref, att_ref, pool_ref, lw_ref, lb_ref, bd_ref, w_ref, o_ref):
    y = yf_ref[...] + yb_ref[...]
    bd = bd_ref[...]
    mu = _dot(y, bd, precision=HI) * (1.0 / HEAD_DIM)
    yc = y - mu
    var = _dot(yc * yc, bd, precision=HI) * (1.0 / HEAD_DIM)
    ya = (yc * lax.rsqrt(var + GN_EPS) * lw_ref[...] + lb_ref[...] + bonus_ref[...]) * g_ref[...]
    cat = jnp.concatenate([ya, att_ref[...], pool_ref[...]], axis=1).astype(BF16)
    o_ref[...] = x_ref[...] + _dot(cat, w_ref[...])


def out_proj(x2d, yf, yb, bonus, g, att, pool, lnx_w, lnx_b, bd, w_bf16, tm=512):
    t, d = x2d.shape
    rows = lambda n: pl.BlockSpec((tm, n), lambda i: (i, 0))
    fixed = lambda shape: pl.BlockSpec(shape, lambda i: (0, 0))
    return pl.pallas_call(
        _out_proj_kernel,
        grid=(t // tm,),
        in_specs=[rows(d),
                  rows(D_RWKV), rows(D_RWKV), rows(D_RWKV), rows(D_RWKV), rows(D_DIFF), rows(D_POOL),
                  fixed((1, D_RWKV)), fixed((1, D_RWKV)), fixed((D_RWKV, D_RWKV)), fixed((d, d))],
        out_specs=rows(d),
        out_shape=jax.ShapeDtypeStruct((t, d), F32),
        compiler_params=_cparams(("parallel",)),
        name="out_proj",
    )(x2d, yf, yb, bonus, g, att, pool, lnx_w.reshape(1, -1), lnx_b.reshape(1, -1), bd, w_bf16)


FF_TILE = 896


def _swiglu_tile(x, wg, wu, wd):
    gate = _dot(x, wg)
    return _dot((gate * _sigmoid(gate) * _dot(x, wu)).astype(BF16), wd)


def _ffn_kernel(x_ref, g_ref, wg_ref, wu_ref, wd_ref, o_ref, h_ref, acc_ref):
    f = pl.program_id(1)

    @pl.when(f == 0)
    def _():
        h_ref[...] = _rms(x_ref[...], g_ref[...]).astype(BF16)
        acc_ref[...] = jnp.zeros_like(acc_ref)

    acc_ref[...] += _swiglu_tile(h_ref[...], wg_ref[...], wu_ref[...], wd_ref[...])

    @pl.when(f == pl.num_programs(1) - 1)
    def _():
        o_ref[...] = x_ref[...] + acc_ref[...]


def ffn_dense(x2d, g, wg, wu, wd, tm=1024):
    t, d = x2d.shape
    ff = wg.shape[1]
    return pl.pallas_call(
        _ffn_kernel,
        grid=(t // tm, ff // FF_TILE),
        in_specs=[pl.BlockSpec((tm, d), lambda i, f: (i, 0)),
                  pl.BlockSpec((1, d), lambda i, f: (0, 0)),
                  pl.BlockSpec((d, FF_TILE), lambda i, f: (0, f)),
                  pl.BlockSpec((d, FF_TILE), lambda i, f: (0, f)),
                  pl.BlockSpec((FF_TILE, d), lambda i, f: (f, 0))],
        out_specs=pl.BlockSpec((tm, d), lambda i, f: (i, 0)),
        out_shape=jax.ShapeDtypeStruct((t, d), F32),
        scratch_shapes=[pltpu.VMEM((tm, d), BF16), pltpu.VMEM((tm, d), F32)],
        compiler_params=_cparams(("parallel", "arbitrary")),
        name="ffn_dense",
    )(x2d, g.reshape(1, d), wg, wu, wd)


def _router_kernel(x_ref, g_ref, wr_ref, h_ref, comb_ref, rank_ref, cnt_ref):
    h = _rms(x_ref[...], g_ref[...])
    h_ref[...] = h.astype(BF16)
    logits = _dot(h, wr_ref[...], precision=HI)
    lane = lax.broadcasted_iota(jnp.int32, logits.shape, 1)
    logits = jnp.where(lane < N_EXPERTS, logits, -jnp.inf)
    v1 = jnp.max(logits, axis=1, keepdims=True)
    i1 = jnp.min(jnp.where(logits == v1, lane, LANES), axis=1, keepdims=True)
    rest = jnp.where(lane == i1, -jnp.inf, logits)
    v2 = jnp.max(rest, axis=1, keepdims=True)
    i2 = jnp.min(jnp.where(rest == v2, lane, LANES), axis=1, keepdims=True)
    e2 = jnp.exp(v2 - v1)
    g1 = 1.0 / (1.0 + e2)
    comb = jnp.where(lane == i1, g1, jnp.where(lane == i2, e2 * g1, 0.0))
    comb_ref[...] = comb
    tm = comb.shape[0]
    routed = comb > 0.0
    mask = jnp.where(routed, 1.0, 0.0).astype(BF16)
    earlier = lax.broadcasted_iota(jnp.int32, (tm, tm), 1) < lax.broadcasted_iota(jnp.int32, (tm, tm), 0)
    rank = _dot(jnp.where(earlier, 1.0, 0.0).astype(BF16), mask)
    rank_ref[...] = jnp.where(routed, rank, -1.0)
    cnt_ref[0] = _dot(jnp.ones((HALO, tm), BF16), mask)


def moe_router(x2d, g, wr_pad, tm):
    t, d = x2d.shape
    tok = pl.BlockSpec((tm, LANES), lambda i: (i, 0))
    return pl.pallas_call(
        _router_kernel,
        grid=(t // tm,),
        in_specs=[pl.BlockSpec((tm, d), lambda i: (i, 0)),
                  pl.BlockSpec((1, d), lambda i: (0, 0)),
                  pl.BlockSpec((d, LANES), lambda i: (0, 0))],
        out_specs=[pl.BlockSpec((tm, d), lambda i: (i, 0)), tok, tok,
                   pl.BlockSpec((1, HALO, LANES), lambda i: (i, 0, 0))],
        out_shape=[jax.ShapeDtypeStruct((t, d), BF16), jax.ShapeDtypeStruct((t, LANES), F32),
                   jax.ShapeDtypeStruct((t, LANES), F32), jax.ShapeDtypeStruct((t // tm, HALO, LANES), F32)],
        compiler_params=_cparams(("parallel",)),
        name="moe_router",
    )(x2d, g.reshape(1, d), wr_pad)


def _moe_kernel(cnt_ref, x_ref, h_ref, comb_ref, rank_ref, rank_t_ref, wg_ref, wu_ref, wd_ref, gout_ref, o_ref,
                acc_ref, xe_ref, ye_ref, gc_ref):
    i, e, f = pl.program_id(0), pl.program_id(1), pl.program_id(2)
    tm = h_ref.shape[0]
    cap = xe_ref.shape[1]
    nblk = (cnt_ref[i, e] + cap - 1) // cap
    onehot = (lax.broadcasted_iota(jnp.int32, (LANES, LANES), 0) == e).astype(F32)

    @pl.when(jnp.logical_and(e == 0, f == 0))
    def _():
        acc_ref[...] = jnp.zeros_like(acc_ref)

    @pl.when(f == 0)
    def _():
        gate_col = _dot(comb_ref[...], onehot, precision=HI)
        slot_row = rank_t_ref[0]

        def gather(j, carry):
            rid = lax.broadcasted_iota(jnp.int32, (cap, tm), 0) + j * cap
            pick = jnp.where(slot_row == rid.astype(F32), 1.0, 0.0)
            xe_ref[j] = _dot(pick.astype(BF16), h_ref[...]).astype(BF16)
            gc_ref[j] = _dot(pick, gate_col, precision=HI)
            return carry

        lax.fori_loop(0, nblk, gather, 0)

    def expert(j, carry):
        y = _swiglu_tile(xe_ref[j], wg_ref[0], wu_ref[0], wd_ref[0])
        ye_ref[j] = jnp.where(f == 0, y, ye_ref[j] + y)
        return carry

    lax.fori_loop(0, nblk, expert, 0)

    @pl.when(f == pl.num_programs(2) - 1)
    def _():
        slot_col = _dot(rank_ref[...], onehot, precision=HI)
        slot_col = jnp.concatenate([slot_col] * (-(-cap // LANES)), axis=1)[:, :cap]

        def scatter(j, carry):
            cid = lax.broadcasted_iota(jnp.int32, (tm, cap), 1) + j * cap
            place = jnp.where(slot_col == cid.astype(F32), 1.0, 0.0).astype(BF16)
            ys = ye_ref[j] * jnp.concatenate([gc_ref[j]] * (ye_ref.shape[2] // LANES), axis=1)
            acc_ref[...] += _dot(place, ys.astype(BF16))
            return carry

        lax.fori_loop(0, nblk, scatter, 0)

    @pl.when(jnp.logical_and(e == pl.num_programs(1) - 1, f == pl.num_programs(2) - 1))
    def _():
        o_ref[...] = _rms(x_ref[...] + acc_ref[...], gout_ref[...])


def moe_ffn(x2d, h, comb, rank, rank_t, cnt, wg, wu, wd, g_out, tm, cap=320):
    t, d = x2d.shape
    ne, _, ff = wg.shape
    max_blocks = -(-tm // cap)
    tok = lambda n: pl.BlockSpec((tm, n), lambda i, e, f, c: (i, 0))
    grid_spec = pltpu.PrefetchScalarGridSpec(
        num_scalar_prefetch=1,
        grid=(t // tm, ne, ff // FF_TILE),
        in_specs=[tok(d), tok(d), tok(LANES), tok(LANES),
                  pl.BlockSpec((1, 1, tm), lambda i, e, f, c: (e, 0, i)),
                  pl.BlockSpec((1, d, FF_TILE), lambda i, e, f, c: (e, 0, f)),
                  pl.BlockSpec((1, d, FF_TILE), lambda i, e, f, c: (e, 0, f)),
                  pl.BlockSpec((1, FF_TILE, d), lambda i, e, f, c: (e, f, 0)),
                  pl.BlockSpec((1, d), lambda i, e, f, c: (0, 0))],
        out_specs=tok(d),
        scratch_shapes=[pltpu.VMEM((tm, d), F32), pltpu.VMEM((max_blocks, cap, d), BF16),
                        pltpu.VMEM((max_blocks, cap, d), F32), pltpu.VMEM((max_blocks, cap, LANES), F32)],
    )
    return pl.pallas_call(
        _moe_kernel,
        grid_spec=grid_spec,
        out_shape=jax.ShapeDtypeStruct((t, d), F32),
        compiler_params=_cparams(("parallel", "arbitrary", "arbitrary")),
        name="moe_ffn",
    )(cnt, x2d, h, comb, rank, rank_t, wg, wu, wd, g_out.reshape(1, d))


def _block_diag_ones(n, blk):
    i = jnp.arange(n) // blk
    return (i[:, None] == i[None, :]).astype(F32)


def _rope_tables(positions):
    half = ROT_DIM // 2
    inv_freq = jnp.power(ROPE_THETA, -(jnp.arange(half, dtype=F32) * 2.0 / ROT_DIM))
    ang = positions.astype(F32)[..., None] * inv_freq
    cos, sin = jnp.cos(ang), jnp.sin(ang)
    one = jnp.ones(cos.shape[:-1] + (DIFF_QK - ROT_DIM,), F32)
    zero = jnp.zeros_like(one)
    z8 = jnp.zeros_like(sin)
    cs = jnp.concatenate([cos, cos, one], axis=-1)
    s1 = jnp.concatenate([z8, sin, zero], axis=-1)
    s2 = jnp.concatenate([-sin, z8, zero], axis=-1)
    return tuple(jnp.concatenate([t, t], axis=-1) for t in (cs, s1, s2))


def kernel(x, positions, norm_mix, w_in_first, w_in_rest, tshift, decay_bias, decay_up, iclr_bias, iclr_up, gate_up, k_k, k_a, r_k, lnx_w, lnx_b, vres_bias, vres_up, lambda_q, lambda_k, subln_w, pool_mix, pool_scale, w_out, norm_ffn, ffn_gate, ffn_up, ffn_down, router, exp_gate, exp_up, exp_down, norm_out):
    bsz, seq, d = x.shape
    depth = norm_mix.shape[0]
    x2d = x.reshape(bsz * seq, d)
    bd = _block_diag_ones(D_RWKV, HEAD_DIM)
    cs, s1, s2 = _rope_tables(positions)
    v_first = None
    for l in range(depth):
        if l == 0:
            w_in = w_in_first
        else:
            wr = w_in_rest[l - 1]
            w_in = jnp.concatenate([wr[:, :D_SHIFT], wr[:, D_SHIFT + MV_LORA:], wr[:, D_SHIFT:D_SHIFT + MV_LORA],
                                    jnp.zeros((d, LANES - MV_LORA), F32)], axis=1)
        proj = norm_matmul(x2d, norm_mix[l], w_in.astype(BF16)).reshape(bsz, seq, -1)

        zeros = jnp.zeros((DECAY_LORA, 2 * D_RWKV), F32)
        dec = jnp.concatenate([decay_up[l, 0], decay_up[l, 1]], axis=1)
        icl = jnp.concatenate([iclr_up[l, 0], iclr_up[l, 1]], axis=1)
        w_lora = jnp.concatenate([
            jnp.concatenate([dec, zeros], axis=1), jnp.zeros((AAA_LORA, 4 * D_RWKV), F32),
            jnp.zeros((DECAY_LORA, 4 * D_RWKV), F32), jnp.concatenate([zeros, icl], axis=1)], axis=0)
        prm = dict(tshift=tshift[l], w_lora=w_lora,
                   b_lora=jnp.concatenate([decay_bias[l, 0], decay_bias[l, 1], iclr_bias[l, 0], iclr_bias[l, 1]]).reshape(1, -1),
                   gate_up=gate_up[l], k_k=k_k[l].reshape(1, -1), k_a=k_a[l].reshape(1, -1),
                   r_k=r_k[l].reshape(1, -1), bd=bd)
        if l > 0:
            prm["vres_bias"] = vres_bias[l - 1].reshape(1, -1)
            prm["vres_up"] = jnp.concatenate([vres_up[l - 1], jnp.zeros((LANES - MV_LORA, D_RWKV), F32)], axis=0)
        r, v, kk, lw, kd, bb, g, bonus = rwkv_prep(proj, v_first, prm)
        if l == 0:
            v_first = v
        yf, yb = wkv_scan(r, v, kk, lw, kd, bb)

        lam_init = 0.8 - 0.6 * math.exp(-0.3 * l)
        qa, qb, kr, vb = attn_prep(proj, cs, s1, s2)
        att = diff_attention(qa, qb, kr, vb, lambda_q[l], lambda_k[l], subln_w[l], lam_init)

        mix_bd = jax.scipy.linalg.block_diag(*[pool_mix[l, gi] for gi in range(len(POOL_WINDOWS))])
        pool = pool_mixer(proj, mix_bd, pool_scale[l])

        flat = lambda t: t.reshape(bsz * seq, -1)
        x2d = out_proj(x2d, flat(yf), flat(yb), flat(bonus), flat(g), flat(att), flat(pool), lnx_w[l], lnx_b[l], bd,
                       w_out[l].astype(BF16))
        i = l // 2
        if l % 2 == 0:
            x2d = ffn_dense(x2d, norm_ffn[l], ffn_gate[i].astype(BF16), ffn_up[i].astype(BF16), ffn_down[i].astype(BF16))
            if l == depth - 1:
                raise NotImplementedError("final norm is fused into the routed mixer")
        else:
            wr_pad = jnp.concatenate([router[i], jnp.zeros((d, LANES - N_EXPERTS), F32)], axis=1)
            tm = min(1024, bsz * seq)
            h, comb, rank, cnt = moe_router(x2d, norm_ffn[l], wr_pad, tm)
            rank_t = rank[:, :N_EXPERTS].T.reshape(N_EXPERTS, 1, -1)
            cnt = cnt[:, 0, :N_EXPERTS].astype(jnp.int32)
            x2d = moe_ffn(x2d, h, comb, rank, rank_t, cnt, exp_gate[i].astype(BF16), exp_up[i].astype(BF16),
                          exp_down[i].astype(BF16), norm_out, tm)
    return x2d.reshape(bsz, seq, d)
```

```python
import functools
import math

import jax
import jax.numpy as jnp
from jax import lax
from jax.experimental import pallas as pl
from jax.experimental.pallas import tpu as pltpu

F32 = jnp.float32
BF16 = jnp.bfloat16
HI = lax.Precision.HIGHEST

D_MODEL = 1024
RWKV_HEADS = 4
HEAD_DIM = 64
D_RWKV = RWKV_HEADS * HEAD_DIM
DECAY_LORA = 64
AAA_LORA = 64
MV_LORA = 32
GATE_LORA = 128
D_SHIFT = 3 * D_RWKV + DECAY_LORA + AAA_LORA + GATE_LORA
W_DECAY_SCALE = 0.6065306597126334
GN_EPS = 64e-5
DIFF_HEADS = 4
DIFF_QK = 64
DIFF_V = 128
D_DIFF_QK = DIFF_HEADS * 2 * DIFF_QK
D_DIFF = DIFF_HEADS * DIFF_V
ROT_DIM = DIFF_QK // 4
ROPE_THETA = 500000.0
SUBLN_EPS = 1e-5
POOL_WINDOWS = (2, 4, 8, 16)
D_POOL = 256
D_FF = 3584
N_EXPERTS = 8
NORM_EPS = 1e-6
P_IN = D_SHIFT + 2 * D_DIFF_QK + D_DIFF + D_POOL
COL_Q, COL_K, COL_V, COL_POOL, COL_VDOWN = 1024, 1536, 2048, 2560, 2816

LANES = 128
HALO = 8
VMEM_LIMIT = 56 * 1024 * 1024

SCAN_CHUNK = 128


def _cparams(sem):
    return pltpu.CompilerParams(dimension_semantics=sem, vmem_limit_bytes=VMEM_LIMIT)


def _sigmoid(x):
    return 1.0 / (1.0 + jnp.exp(-x))


def _dot(a, b, **kw):
    return jnp.dot(a, b, preferred_element_type=F32, **kw)


def _bdot(a, b):
    return jnp.dot(a.astype(BF16), b.astype(BF16), preferred_element_type=F32)


def _rms(x, g):
    return x * lax.rsqrt(jnp.mean(x * x, axis=-1, keepdims=True) + NORM_EPS) * g


def _norm_matmul_kernel(x_ref, g_ref, w_ref, o_ref):
    h = _rms(x_ref[...], g_ref[...]).astype(BF16)
    o_ref[...] = _dot(h, w_ref[...])


def norm_matmul(x2d, g, w_bf16, tm=512):
    t, d = x2d.shape
    n = w_bf16.shape[1]
    return pl.pallas_call(
        _norm_matmul_kernel,
        grid=(t // tm,),
        in_specs=[
            pl.BlockSpec((tm, d), lambda i: (i, 0)),
            pl.BlockSpec((1, d), lambda i: (0, 0)),
            pl.BlockSpec((d, n), lambda i: (0, 0)),
        ],
        out_specs=pl.BlockSpec((tm, n), lambda i: (i, 0)),
        out_shape=jax.ShapeDtypeStruct((t, n), F32),
        compiler_params=_cparams(("parallel",)),
        name="norm_matmul",
    )(x2d, g.reshape(1, d), w_bf16)


def _shifted(u, prev_row, next_row):
    ts = u.shape[0]
    row = lax.broadcasted_iota(jnp.int32, u.shape, 0)
    prev = jnp.where(row == 0, prev_row, pltpu.roll(u, 1, 0))
    nxt = jnp.where(row == ts - 1, next_row, pltpu.roll(u, ts - 1, 0))
    return prev, nxt


def _rwkv_prep_kernel(has_vres, *refs):
    if has_vres:
        (u_ref, up_ref, un_ref, vd_ref, vf_ref, vb_ref, vu_ref, mu_ref, wl_ref, bias_ref, gu_ref,
         kkw_ref, ka_ref, rk_ref, bd_ref,
         r_ref, v_ref, kk_ref, lw_ref, kd_ref, bb_ref, g_ref, bonus_ref) = refs
    else:
        (u_ref, up_ref, un_ref, mu_ref, wl_ref, bias_ref, gu_ref,
         kkw_ref, ka_ref, rk_ref, bd_ref,
         r_ref, v_ref, kk_ref, lw_ref, kd_ref, bb_ref, g_ref, bonus_ref) = refs
    i = pl.program_id(1)
    n = pl.num_programs(1)
    u = u_ref[0]
    prev_row = jnp.where(i > 0, up_ref[0, HALO - 1:HALO, :], 0.0)
    next_row = jnp.where(i < n - 1, un_ref[0, 0:1, :], 0.0)
    prev, nxt = _shifted(u, prev_row, next_row)
    us = u + mu_ref[0:1, :] * (prev - u) + mu_ref[1:2, :] * (nxt - u)
    r = us[:, 0:256]
    k = us[:, 256:512]
    v = us[:, 512:768]
    xwa = us[:, 768:896]
    xg = us[:, 896:1024]
    lhs = jnp.concatenate([jnp.tanh(xwa), xwa], axis=1)
    z = _dot(lhs, wl_ref[...], precision=HI) + bias_ref[...]
    if has_vres:
        mix = _sigmoid(vb_ref[...] + _dot(vd_ref[0], vu_ref[...], precision=HI))
        v = v + (vf_ref[0] - v) * mix
    g = _dot(_sigmoid(xg), gu_ref[...], precision=HI)
    bd = bd_ref[...]
    kkr = k * kkw_ref[...]
    ss = _dot(kkr * kkr, bd, precision=HI)
    kk = kkr * lax.rsqrt(jnp.maximum(ss, 1e-24))
    r_ref[0] = r
    v_ref[0] = v
    kk_ref[0] = kk
    g_ref[0] = g
    bonus = jnp.zeros_like(v)
    for d in range(2):
        lw = -W_DECAY_SCALE * _sigmoid(z[:, 256 * d:256 * d + 256])
        a = _sigmoid(z[:, 512 + 256 * d:768 + 256 * d])
        kd = k * (1.0 + (a - 1.0) * ka_ref[...])
        lw_ref[d, 0] = lw
        kd_ref[d, 0] = kd
        bb_ref[d, 0] = kk * a
        bonus = bonus + _dot(r * kd * rk_ref[...], bd, precision=HI) * v
    bonus_ref[0] = bonus


def rwkv_prep(proj, v_first, p, ts=512):
    b, s, _ = proj.shape
    has_vres = v_first is not None
    nblk = ts // HALO
    last = s // HALO - 1
    row = lambda shape: pl.BlockSpec(shape, lambda bi, i: (0,) * len(shape))
    in_specs = [
        pl.BlockSpec((1, ts, D_SHIFT), lambda bi, i: (bi, i, 0)),
        pl.BlockSpec((1, HALO, D_SHIFT), lambda bi, i: (bi, jnp.maximum(i * nblk - 1, 0), 0)),
        pl.BlockSpec((1, HALO, D_SHIFT), lambda bi, i: (bi, jnp.minimum((i + 1) * nblk, last), 0)),
    ]
    args = [proj, proj, proj]
    if has_vres:
        in_specs += [
            pl.BlockSpec((1, ts, LANES), lambda bi, i: (bi, i, COL_VDOWN // LANES)),
            pl.BlockSpec((1, ts, D_RWKV), lambda bi, i: (bi, i, 0)),
            row((1, D_RWKV)), row((LANES, D_RWKV)),
        ]
        args += [proj, v_first, p["vres_bias"], p["vres_up"]]
    in_specs += [row((2, D_SHIFT)), row((256, 1024)), row((1, 1024)), row((GATE_LORA, D_RWKV)),
                 row((1, D_RWKV)), row((1, D_RWKV)), row((1, D_RWKV)), row((D_RWKV, D_RWKV))]
    args += [p["tshift"], p["w_lora"], p["b_lora"], p["gate_up"], p["k_k"], p["k_a"], p["r_k"], p["bd"]]
    one = pl.BlockSpec((1, ts, D_RWKV), lambda bi, i: (bi, i, 0))
    two = pl.BlockSpec((2, 1, ts, D_RWKV), lambda bi, i: (0, bi, i, 0))
    s1 = jax.ShapeDtypeStruct((b, s, D_RWKV), F32)
    s2 = jax.ShapeDtypeStruct((2, b, s, D_RWKV), F32)
    return pl.pallas_call(
        functools.partial(_rwkv_prep_kernel, has_vres),
        grid=(b, s // ts),
        in_specs=in_specs,
        out_specs=[one, one, one, two, two, two, one, one],
        out_shape=[s1, s1, s1, s2, s2, s2, s1, s1],
        compiler_params=_cparams(("parallel", "parallel")),
        name="rwkv_prep",
    )(*args)


def _unit_lower_inverses(ms, eye):
    xs = [eye - m for m in ms]
    ps = [_bdot(m, m) for m in ms]
    levels = int(math.log2(ms[0].shape[0])) - 1
    for k in range(levels):
        nxt = [_bdot(p, p) for p in ps] if k < levels - 1 else None
        xs = [x + _bdot(x, p) for x, p in zip(xs, ps)]
        ps = nxt
    return xs


def _split3(x):
    hi = x.astype(BF16)
    r1 = x - hi.astype(F32)
    mid = r1.astype(BF16)
    lo = (r1 - mid.astype(F32)).astype(BF16)
    return hi, mid, lo


def _chunk_terms(rev, r, v, kk, lw, kd, bb):
    c = SCAN_CHUNK
    n = len(r)
    ti = lax.broadcasted_iota(jnp.int32, (c, c), 0)
    tj = lax.broadcasted_iota(jnp.int32, (c, c), 1)
    eye = (ti == tj).astype(F32)
    same_head = (ti // HEAD_DIM) == (tj // HEAD_DIM)
    first = lax.broadcasted_iota(jnp.int32, (c, LANES), 1) < HEAD_DIM
    first2 = jnp.concatenate([first, first], axis=1)
    strict = [(ti - tj if rv else tj - ti) < 0 for rv in rev]
    incl = [(ti - tj if rv else tj - ti) <= 0 for rv in rev]

    cum = []
    for i in range(n):
        cum3 = _dot(jnp.where(incl[i], 1.0, 0.0).astype(BF16), jnp.concatenate(_split3(lw[i]), axis=1))
        cum.append(cum3[:, :LANES] + cum3[:, LANES:2 * LANES] + cum3[:, 2 * LANES:])
    a = [kk[i] * jnp.exp(cum[i] - lw[i]) for i in range(n)]
    rt = [r[i] * jnp.exp(cum[i]) for i in range(n)]
    einv = [jnp.exp(-cm) for cm in cum]
    kt_t = [(kd[i] * einv[i]).T for i in range(n)]
    bt_t = [(bb[i] * einv[i]).T for i in range(n)]
    last = [cum[i][0:1] if rev[i] else cum[i][c - 1:c] for i in range(n)]
    g_all = [jnp.exp(jnp.broadcast_to(x, (c, LANES))).T for x in last]
    vb = [x.astype(BF16) for x in v]
    rhs = [jnp.concatenate([bt_t[i], kt_t[i]], axis=1).astype(BF16) for i in range(n)]
    heads = [(i, hm) for i in range(n) for hm in (first, jnp.logical_not(first))]
    gm = [_dot(jnp.concatenate([jnp.where(hm, a[i], 0.0), jnp.where(hm, rt[i], 0.0)], axis=0).astype(BF16), rhs[i])
          for i, hm in heads]
    t_inv = _unit_lower_inverses([jnp.where(strict[i], g[:c, :c], 0.0) for (i, _), g in zip(heads, gm)], eye)
    mv = [_dot(jnp.where(strict[i], g[:c, c:], 0.0).astype(BF16), vb[i]) for (i, _), g in zip(heads, gm)]
    tam = [_bdot(t, jnp.concatenate([a[i], m], axis=1)) for (i, _), t, m in zip(heads, t_inv, mv)]
    lt = [_bdot(jnp.where(incl[i], g[c:, :c], 0.0), x) for (i, _), g, x in zip(heads, gm, tam)]
    lv = [_dot(jnp.where(incl[i], g[c:, c:], 0.0).astype(BF16), vb[i]) for (i, _), g in zip(heads, gm)]
    tam = [jnp.where(first2, tam[2 * i], tam[2 * i + 1]) for i in range(n)]
    lt = [jnp.where(first2, lt[2 * i], lt[2 * i + 1]) for i in range(n)]
    lv = [jnp.where(first, lv[2 * i], lv[2 * i + 1]) for i in range(n)]
    btx = [_bdot(bt_t[i], tam[i]) for i in range(n)]
    kv = [_dot(kt_t[i].astype(BF16), vb[i]) for i in range(n)]
    phi = [g_all[i] * (eye - jnp.where(same_head, btx[i][:, :LANES], 0.0)) for i in range(n)]
    psi = [g_all[i] * jnp.where(same_head, kv[i] - btx[i][:, LANES:], 0.0) for i in range(n)]
    ra = [rt[i] - lt[i][:, :LANES] for i in range(n)]
    oc = [lv[i] - lt[i][:, LANES:] for i in range(n)]
    return phi, psi, ra, oc


def _wkv_kernel(rf_ref, vf_ref, kkf_ref, lwf_ref, kdf_ref, bbf_ref,
                rb_ref, vb_ref, kkb_ref, lwb_ref, kdb_ref, bbb_ref, yf_ref, yb_ref, h_ref):
    @pl.when(pl.program_id(1) == 0)
    def _():
        h_ref[...] = jnp.zeros_like(h_ref)

    pairs = (slice(0, LANES), slice(LANES, 2 * LANES))
    probs = [(False, sl, (rf_ref, vf_ref, kkf_ref), (lwf_ref, kdf_ref, bbf_ref), yf_ref) for sl in pairs]
    probs += [(True, sl, (rb_ref, vb_ref, kkb_ref), (lwb_ref, kdb_ref, bbb_ref), yb_ref) for sl in pairs]
    rev = [pr[0] for pr in probs]
    r, v, kk = ([pr[2][j][0, :, pr[1]] for pr in probs] for j in range(3))
    lw, kd, bb = ([pr[3][j][0, 0, :, pr[1]] for pr in probs] for j in range(3))
    phi, psi, ra, oc = _chunk_terms(rev, r, v, kk, lw, kd, bb)
    res = []
    for i in range(len(probs)):
        h = h_ref[i]
        h_hi = h.astype(BF16)
        h_lo = (h - h_hi.astype(F32)).astype(BF16)
        lhs = jnp.concatenate([phi[i], ra[i]], axis=0).astype(BF16)
        res.append(_dot(jnp.concatenate([lhs, lhs], axis=1), jnp.concatenate([h_hi, h_lo], axis=0)))
    for i, pr in enumerate(probs):
        h_ref[i] = res[i][:SCAN_CHUNK] + psi[i]
        pr[4][0, :, pr[1]] = res[i][SCAN_CHUNK:] + oc[i]


def wkv_scan(r, v, kk, lw, kd, bb):
    b, s, _ = r.shape
    c = SCAN_CHUNK
    nc = s // c
    fwd = pl.BlockSpec((1, c, D_RWKV), lambda bi, ci: (bi, ci, 0))
    bwd = pl.BlockSpec((1, c, D_RWKV), lambda bi, ci: (bi, nc - 1 - ci, 0))
    fwd_d = pl.BlockSpec((1, 1, c, D_RWKV), lambda bi, ci: (0, bi, ci, 0))
    bwd_d = pl.BlockSpec((1, 1, c, D_RWKV), lambda bi, ci: (1, bi, nc - 1 - ci, 0))
    out = jax.ShapeDtypeStruct((b, s, D_RWKV), F32)
    return pl.pallas_call(
        _wkv_kernel,
        grid=(b, nc),
        in_specs=[fwd, fwd, fwd, fwd_d, fwd_d, fwd_d, bwd, bwd, bwd, bwd_d, bwd_d, bwd_d],
        out_specs=[fwd, bwd],
        out_shape=[out, out],
        scratch_shapes=[pltpu.VMEM((4, LANES, LANES), F32)],
        compiler_params=_cparams(("parallel", "arbitrary")),
        name="wkv_scan",
    )(r, v, kk, lw, kd, bb, r, v, kk, lw, kd, bb)


def _attn_prep_kernel(q_ref, k_ref, v_ref, c_ref, s1_ref, s2_ref, qa_ref, qb_ref, ko_ref, vo_ref):
    cs, s1, s2 = c_ref[0], s1_ref[0], s2_ref[0]
    lane = lax.broadcasted_iota(jnp.int32, cs.shape, 1)
    first = lane < DIFF_QK
    scale = DIFF_QK ** -0.5 * math.log2(math.e)
    for j in range(D_DIFF_QK // LANES):
        sl = slice(LANES * j, LANES * j + LANES)

        def rope(t):
            return t * cs + pltpu.roll(t, ROT_DIM // 2, 1) * s1 + pltpu.roll(t, LANES - ROT_DIM // 2, 1) * s2

        q = rope(q_ref[0, :, sl]) * scale
        qa_ref[0, :, sl] = jnp.where(first, q, 0.0).astype(BF16)
        qb_ref[0, :, sl] = jnp.where(first, 0.0, q).astype(BF16)
        ko_ref[0, :, sl] = rope(k_ref[0, :, sl]).astype(BF16)
    ones = jnp.ones((v_ref.shape[1], DIFF_V), BF16)
    for h in range(DIFF_HEADS):
        vo_ref[0, :, 2 * DIFF_V * h:2 * DIFF_V * h + DIFF_V] = v_ref[0, :, DIFF_V * h:DIFF_V * h + DIFF_V].astype(BF16)
        vo_ref[0, :, 2 * DIFF_V * h + DIFF_V:2 * DIFF_V * (h + 1)] = ones


def attn_prep(proj, cs, s1, s2, ts=512):
    b, s, _ = proj.shape
    col = lambda off: pl.BlockSpec((1, ts, 512), lambda bi, i: (bi, i, off // 512))
    tab = pl.BlockSpec((1, ts, LANES), lambda bi, i: (bi, i, 0))
    out = pl.BlockSpec((1, ts, 512), lambda bi, i: (bi, i, 0))
    sh = jax.ShapeDtypeStruct((b, s, 512), BF16)
    return pl.pallas_call(
        _attn_prep_kernel,
        grid=(b, s // ts),
        in_specs=[col(COL_Q), col(COL_K), col(COL_V), tab, tab, tab],
        out_specs=[out, out, out, pl.BlockSpec((1, ts, 2 * D_DIFF), lambda bi, i: (bi, i, 0))],
        out_shape=[sh, sh, sh, jax.ShapeDtypeStruct((b, s, 2 * D_DIFF), BF16)],
        compiler_params=_cparams(("parallel", "parallel")),
        name="attn_prep",
    )(proj, proj, proj, cs, s1, s2)


def _attn_kernel(lam_init, q_rows, qa_ref, qb_ref, k_ref, v_ref, lq_ref, lk_ref, sw_ref, o_ref):
    kt = k_ref[0]
    vt = v_ref[0]
    e = jnp.exp(jnp.sum(lq_ref[...] * lk_ref[...], axis=1, keepdims=True))
    lam = e[0:1] - e[1:2] + lam_init
    blocks = [pl.ds(r0, q_rows) for r0 in range(0, qa_ref.shape[1], q_rows)]
    n = len(blocks)
    s, p = [None] * n, [None] * n
    for step in range(n + 2):
        if step < n:
            q2 = jnp.concatenate([qa_ref[0, blocks[step], :], qb_ref[0, blocks[step], :]], axis=0)
            s[step] = lax.dot_general(q2, kt, (((1,), (1,)), ((), ())), preferred_element_type=F32)
        if 0 <= step - 1 < n:
            p[step - 1] = jnp.exp2(s[step - 1] - jnp.max(s[step - 1], axis=1, keepdims=True)).astype(BF16)
            s[step - 1] = None
        if 0 <= step - 2 < n:
            pv = _dot(p[step - 2], vt)
            ratio = pv[:, :DIFF_V] / pv[:, DIFF_V:]
            o = ratio[:q_rows] - lam * ratio[q_rows:]
            o = o * lax.rsqrt(jnp.mean(o * o, axis=1, keepdims=True) + SUBLN_EPS) * sw_ref[...]
            o_ref[0, blocks[step - 2], :] = o * (1.0 - lam_init)
            p[step - 2] = None


def diff_attention(qa, qb, k, v, lq, lk, sw, lam_init, tq=512, q_rows=128):
    b, s, _ = qa.shape
    tq = min(tq, s)
    qspec = pl.BlockSpec((1, tq, LANES), lambda bi, h, i: (bi, i, h))
    kspec = pl.BlockSpec((1, s, LANES), lambda bi, h, i: (bi, 0, h))
    small = lambda shape: pl.BlockSpec(shape, lambda bi, h, i: (0, 0))
    return pl.pallas_call(
        functools.partial(_attn_kernel, lam_init, min(q_rows, tq)),
        grid=(b, DIFF_HEADS, s // tq),
        in_specs=[qspec, qspec, kspec, pl.BlockSpec((1, s, 2 * DIFF_V), lambda bi, h, i: (bi, 0, h)),
                  small((2, DIFF_QK)), small((2, DIFF_QK)), small((1, DIFF_V))],
        out_specs=qspec,
        out_shape=jax.ShapeDtypeStruct((b, s, D_DIFF), F32),
        compiler_params=_cparams(("parallel", "parallel", "parallel")),
        name="diff_attention",
    )(qa, qb, k, v, lq, lk, sw.reshape(1, DIFF_V))


def _pool_kernel(seq, u_ref, up_ref, un_ref, mix_ref, scale_ref, o_ref):
    i = pl.program_id(1)
    n = pl.num_programs(1)
    u = u_ref[0]
    ts = u.shape[0]
    prev = jnp.where(i > 0, up_ref[0], 0.0)
    nxt = jnp.where(i < n - 1, un_ref[0], 0.0)
    ext = jnp.concatenate([prev, u, nxt], axis=0)
    rows = ext.shape[0]
    back = lambda t, sft: pltpu.roll(t, sft, 0)
    fwd = lambda t, sft: pltpu.roll(t, rows - sft, 0)
    w2 = ext + back(ext, 1)
    w4 = back(w2, 1) + fwd(w2, 1)
    w8 = back(w4, 2) + fwd(w4, 2)
    w16 = back(w8, 4) + fwd(w8, 4)
    lane = lax.broadcasted_iota(jnp.int32, (ts, D_POOL), 1)
    t = lax.broadcasted_iota(jnp.int32, (ts, D_POOL), 0) + i * ts
    grp = lane // HEAD_DIM
    sel = lambda vals: jnp.where(grp == 0, vals[0], jnp.where(grp == 1, vals[1], jnp.where(grp == 2, vals[2], vals[3])))
    wsum = sel([w[HALO:HALO + ts] for w in (w2, w4, w8, w16)])
    half = sel([wd // 2 for wd in POOL_WINDOWS])
    width = sel(list(POOL_WINDOWS))
    cnt = jnp.minimum(t + width - half, seq) - jnp.maximum(t - half, 0)
    pooled = wsum / cnt.astype(F32) - u
    o_ref[0] = _dot(pooled, mix_ref[...], precision=HI) * scale_ref[...]


def pool_mixer(proj, mix_bd, scale, ts=512):
    b, s, _ = proj.shape
    nblk = ts // HALO
    last = s // HALO - 1
    cb = COL_POOL // D_POOL
    hb = COL_POOL // D_POOL
    return pl.pallas_call(
        functools.partial(_pool_kernel, s),
        grid=(b, s // ts),
        in_specs=[
            pl.BlockSpec((1, ts, D_POOL), lambda bi, i: (bi, i, cb)),
            pl.BlockSpec((1, HALO, D_POOL), lambda bi, i: (bi, jnp.maximum(i * nblk - 1, 0), hb)),
            pl.BlockSpec((1, HALO, D_POOL), lambda bi, i: (bi, jnp.minimum((i + 1) * nblk, last), hb)),
            pl.BlockSpec((D_POOL, D_POOL), lambda bi, i: (0, 0)),
            pl.BlockSpec((1, D_POOL), lambda bi, i: (0, 0)),
        ],
        out_specs=pl.BlockSpec((1, ts, D_POOL), lambda bi, i: (bi, i, 0)),
        out_shape=jax.ShapeDtypeStruct((b, s, D_POOL), F32),
        compiler_params=_cparams(("parallel", "parallel")),
        name="pool_mixer",
    )(proj, proj, proj, mix_bd, scale.reshape(1, D_POOL))


def _out_proj_kernel(x_ref, yf_ref, yb_ref, bonus_ref, g_ref, att_ref, pool_ref, lw_ref, lb_ref, bd_ref, w_ref, o_ref):
    y = yf_ref[...] + yb_ref[...]
    bd = bd_ref[...]
    mu = _dot(y, bd, precision=HI) * (1.0 / HEAD_DIM)
    yc = y - mu
    var = _dot(yc * yc, bd, precision=HI) * (1.0 / HEAD_DIM)
    ya = (yc * lax.rsqrt(var + GN_EPS) * lw_ref[...] + lb_ref[...] + bonus_ref[...]) * g_ref[...]
    cat = jnp.concatenate([ya, att_ref[...], pool_ref[...]], axis=1).astype(BF16)
    o_ref[...] = x_ref[...] + _dot(cat, w_ref[...])


def out_proj(x2d, yf, yb, bonus, g, att, pool, lnx_w, lnx_b, bd, w_bf16, tm=512):
    t, d = x2d.shape
    rows = lambda n: pl.BlockSpec((tm, n), lambda i: (i, 0))
    fixed = lambda shape: pl.BlockSpec(shape, lambda i: (0, 0))
    return pl.pallas_call(
        _out_proj_kernel,
        grid=(t // tm,),
        in_specs=[rows(d),
                  rows(D_RWKV), rows(D_RWKV), rows(D_RWKV), rows(D_RWKV), rows(D_DIFF), rows(D_POOL),
                  fixed((1, D_RWKV)), fixed((1, D_RWKV)), fixed((D_RWKV, D_RWKV)), fixed((d, d))],
        out_specs=rows(d),
        out_shape=jax.ShapeDtypeStruct((t, d), F32),
        compiler_params=_cparams(("parallel",)),
        name="out_proj",
    )(x2d, yf, yb, bonus, g, att, pool, lnx_w.reshape(1, -1), lnx_b.reshape(1, -1), bd, w_bf16)


FF_TILE = 512
TOP_K = 2


def _swiglu_tile(x, wg, wu, wd):
    gate = _dot(x, wg)
    return _dot((gate * _sigmoid(gate) * _dot(x, wu)).astype(BF16), wd)


def _ffn_kernel(x_ref, g_ref, wg_ref, wu_ref, wd_ref, o_ref, h_ref, acc_ref):
    f = pl.program_id(1)

    @pl.when(f == 0)
    def _():
        h_ref[...] = _rms(x_ref[...], g_ref[...]).astype(BF16)
        acc_ref[...] = jnp.zeros_like(acc_ref)

    acc_ref[...] += _swiglu_tile(h_ref[...], wg_ref[...], wu_ref[...], wd_ref[...])

    @pl.when(f == pl.num_programs(1) - 1)
    def _():
        o_ref[...] = x_ref[...] + acc_ref[...]


def ffn_dense(x2d, g, wg, wu, wd, tm=1024):
    t, d = x2d.shape
    ff = wg.shape[1]
    return pl.pallas_call(
        _ffn_kernel,
        grid=(t // tm, ff // FF_TILE),
        in_specs=[pl.BlockSpec((tm, d), lambda i, f: (i, 0)),
                  pl.BlockSpec((1, d), lambda i, f: (0, 0)),
                  pl.BlockSpec((d, FF_TILE), lambda i, f: (0, f)),
                  pl.BlockSpec((d, FF_TILE), lambda i, f: (0, f)),
                  pl.BlockSpec((FF_TILE, d), lambda i, f: (f, 0))],
        out_specs=pl.BlockSpec((tm, d), lambda i, f: (i, 0)),
        out_shape=jax.ShapeDtypeStruct((t, d), F32),
        scratch_shapes=[pltpu.VMEM((tm, d), BF16), pltpu.VMEM((tm, d), F32)],
        compiler_params=_cparams(("parallel", "arbitrary")),
        name="ffn_dense",
    )(x2d, g.reshape(1, d), wg, wu, wd)


def _router_kernel(x_ref, g_ref, wr_ref, h_ref, comb_ref, rank_ref, cnt_ref):
    h = _rms(x_ref[...], g_ref[...])
    h_ref[...] = h.astype(BF16)
    logits = _dot(h, wr_ref[...], precision=HI)
    lane = lax.broadcasted_iota(jnp.int32, logits.shape, 1)
    logits = jnp.where(lane < N_EXPERTS, logits, -jnp.inf)
    v1 = jnp.max(logits, axis=1, keepdims=True)
    i1 = jnp.min(jnp.where(logits == v1, lane, LANES), axis=1, keepdims=True)
    rest = jnp.where(lane == i1, -jnp.inf, logits)
    v2 = jnp.max(rest, axis=1, keepdims=True)
    i2 = jnp.min(jnp.where(rest == v2, lane, LANES), axis=1, keepdims=True)
    e2 = jnp.exp(v2 - v1)
    g1 = 1.0 / (1.0 + e2)
    comb = jnp.where(lane == i1, g1, jnp.where(lane == i2, e2 * g1, 0.0))
    comb_ref[...] = comb
    tm = comb.shape[0]
    routed = comb > 0.0
    mask = jnp.where(routed, 1.0, 0.0).astype(BF16)
    earlier = lax.broadcasted_iota(jnp.int32, (tm, tm), 1) < lax.broadcasted_iota(jnp.int32, (tm, tm), 0)
    rank = _dot(jnp.where(earlier, 1.0, 0.0).astype(BF16), mask)
    rank_ref[...] = jnp.where(routed, rank, -1.0)
    cnt_ref[0] = _dot(jnp.ones((HALO, tm), BF16), mask)


def moe_router(x2d, g, wr_pad, tm):
    t, d = x2d.shape
    tok = pl.BlockSpec((tm, LANES), lambda i: (i, 0))
    return pl.pallas_call(
        _router_kernel,
        grid=(t // tm,),
        in_specs=[pl.BlockSpec((tm, d), lambda i: (i, 0)),
                  pl.BlockSpec((1, d), lambda i: (0, 0)),
                  pl.BlockSpec((d, LANES), lambda i: (0, 0))],
        out_specs=[pl.BlockSpec((tm, d), lambda i: (i, 0)), tok, tok,
                   pl.BlockSpec((1, HALO, LANES), lambda i: (i, 0, 0))],
        out_shape=[jax.ShapeDtypeStruct((t, d), BF16), jax.ShapeDtypeStruct((t, LANES), F32),
                   jax.ShapeDtypeStruct((t, LANES), F32), jax.ShapeDtypeStruct((t // tm, HALO, LANES), F32)],
        compiler_params=_cparams(("parallel",)),
        name="moe_router",
    )(x2d, g.reshape(1, d), wr_pad)


def _moe_kernel(item_ref, h_ref, comb_ref, rank_ref, rank_t_ref, wg_ref, wu_ref, wd_ref, o_ref,
                xe_ref, ye_ref, gc_ref):
    i, k, f = pl.program_id(0), pl.program_id(1), pl.program_id(2)
    tm = h_ref.shape[0]
    cap = xe_ref.shape[0]
    e, j, on = item_ref[0, i, k], item_ref[1, i, k], item_ref[2, i, k] > 0
    first, last = f == 0, f == pl.num_programs(2) - 1

    @pl.when(jnp.logical_and(k == 0, first))
    def _():
        o_ref[...] = jnp.zeros_like(o_ref)

    def lane_column(x):
        onehot = jnp.where(lax.broadcasted_iota(jnp.int32, (LANES, LANES), 0) == e, 1.0, 0.0).astype(BF16)
        hi = x.astype(BF16)
        lo = (x - hi.astype(F32)).astype(BF16)
        both = _dot(jnp.concatenate([hi, lo], axis=0), onehot)
        return both[:x.shape[0]] + both[x.shape[0]:]

    @pl.when(jnp.logical_and(on, first))
    def _():
        gate = lane_column(comb_ref[...])
        g_hi = gate.astype(BF16)
        g_lo = (gate - g_hi.astype(F32)).astype(BF16)
        rid = lax.broadcasted_iota(jnp.int32, (cap, tm), 0) + j * cap
        pick = jnp.where(rank_t_ref[0] == rid.astype(F32), 1.0, 0.0).astype(BF16)
        xe_ref[...] = _dot(pick, h_ref[...]).astype(BF16)
        gc2 = _dot(pick, jnp.concatenate([g_hi, g_lo], axis=1))
        gc_ref[...] = gc2[:, :LANES] + gc2[:, LANES:]

    @pl.when(on)
    def _():
        y = _swiglu_tile(xe_ref[...], wg_ref[0], wu_ref[0], wd_ref[0])
        ye_ref[...] = jnp.where(first, y, ye_ref[...] + y)

    @pl.when(jnp.logical_and(on, last))
    def _():
        slot = jnp.concatenate([lane_column(rank_ref[...])] * (cap // LANES), axis=1)
        cid = lax.broadcasted_iota(jnp.int32, (tm, cap), 1) + j * cap
        place = jnp.where(slot == cid.astype(F32), 1.0, 0.0).astype(BF16)
        ys = ye_ref[...] * jnp.concatenate([gc_ref[...]] * (ye_ref.shape[1] // LANES), axis=1)
        o_ref[...] += _dot(place, ys.astype(BF16))


def _moe_items(cnt, cap, n_items):
    nblk = (cnt + cap - 1) // cap
    ends = jnp.cumsum(nblk, axis=1)
    k = jnp.arange(n_items, dtype=jnp.int32)
    total = ends[:, -1:]
    kk = jnp.minimum(k[None, :], total - 1)
    e = jnp.sum((kk[:, :, None] >= ends[:, None, :]).astype(jnp.int32), axis=2)
    start = jnp.take_along_axis(ends - nblk, e, axis=1)
    return jnp.stack([e, kk - start, (k[None, :] < total).astype(jnp.int32)]).astype(jnp.int32)


def moe_ffn(h, comb, rank, rank_t, cnt, wg, wu, wd, tm, cap):
    t, d = h.shape
    ne, _, ff = wg.shape
    nf = ff // FF_TILE
    n_items = ne + (TOP_K * tm) // cap
    items = _moe_items(cnt, cap, n_items)
    tok = lambda n: pl.BlockSpec((tm, n), lambda i, k, f, it: (i, 0))
    fidx = lambda k, f, it, i: jnp.where(it[2, i, k] > 0, f, nf - 1)
    grid_spec = pltpu.PrefetchScalarGridSpec(
        num_scalar_prefetch=1,
        grid=(t // tm, n_items, nf),
        in_specs=[tok(d), tok(LANES), tok(LANES),
                  pl.BlockSpec((1, 1, tm), lambda i, k, f, it: (it[0, i, k], 0, i)),
                  pl.BlockSpec((1, d, FF_TILE), lambda i, k, f, it: (it[0, i, k], 0, fidx(k, f, it, i))),
                  pl.BlockSpec((1, d, FF_TILE), lambda i, k, f, it: (it[0, i, k], 0, fidx(k, f, it, i))),
                  pl.BlockSpec((1, FF_TILE, d), lambda i, k, f, it: (it[0, i, k], fidx(k, f, it, i), 0))],
        out_specs=tok(d),
        scratch_shapes=[pltpu.VMEM((cap, d), BF16), pltpu.VMEM((cap, d), F32), pltpu.VMEM((cap, LANES), F32)],
    )
    return pl.pallas_call(
        _moe_kernel,
        grid_spec=grid_spec,
        out_shape=jax.ShapeDtypeStruct((t, d), F32),
        compiler_params=_cparams(("parallel", "arbitrary", "arbitrary")),
        name="moe_ffn",
    )(items, h, comb, rank, rank_t, wg, wu, wd)


def _add_norm_kernel(x_ref, y_ref, g_ref, o_ref):
    o_ref[...] = _rms(x_ref[...] + y_ref[...], g_ref[...])


def add_norm(x2d, y2d, g, tm=1024):
    t, d = x2d.shape
    row = pl.BlockSpec((tm, d), lambda i: (i, 0))
    return pl.pallas_call(
        _add_norm_kernel,
        grid=(t // tm,),
        in_specs=[row, row, pl.BlockSpec((1, d), lambda i: (0, 0))],
        out_specs=row,
        out_shape=jax.ShapeDtypeStruct((t, d), F32),
        compiler_params=_cparams(("parallel",)),
        name="add_norm",
    )(x2d, y2d, g.reshape(1, d))


def _block_diag_ones(n, blk):
    i = jnp.arange(n) // blk
    return (i[:, None] == i[None, :]).astype(F32)


def _rope_tables(positions):
    half = ROT_DIM // 2
    inv_freq = jnp.power(ROPE_THETA, -(jnp.arange(half, dtype=F32) * 2.0 / ROT_DIM))
    ang = positions.astype(F32)[..., None] * inv_freq
    cos, sin = jnp.cos(ang), jnp.sin(ang)
    one = jnp.ones(cos.shape[:-1] + (DIFF_QK - ROT_DIM,), F32)
    zero = jnp.zeros_like(one)
    z8 = jnp.zeros_like(sin)
    cs = jnp.concatenate([cos, cos, one], axis=-1)
    s1 = jnp.concatenate([z8, sin, zero], axis=-1)
    s2 = jnp.concatenate([-sin, z8, zero], axis=-1)
    return tuple(jnp.concatenate([t, t], axis=-1) for t in (cs, s1, s2))


def kernel(x, positions, norm_mix, w_in_first, w_in_rest, tshift, decay_bias, decay_up, iclr_bias, iclr_up, gate_up, k_k, k_a, r_k, lnx_w, lnx_b, vres_bias, vres_up, lambda_q, lambda_k, subln_w, pool_mix, pool_scale, w_out, norm_ffn, ffn_gate, ffn_up, ffn_down, router, exp_gate, exp_up, exp_down, norm_out):
    bsz, seq, d = x.shape
    depth = norm_mix.shape[0]
    x2d = x.reshape(bsz * seq, d)
    bd = _block_diag_ones(D_RWKV, HEAD_DIM)
    cs, s1, s2 = _rope_tables(positions)
    v_first = None
    for l in range(depth):
        if l == 0:
            w_in = w_in_first
        else:
            wr = w_in_rest[l - 1]
            w_in = jnp.concatenate([wr[:, :D_SHIFT], wr[:, D_SHIFT + MV_LORA:], wr[:, D_SHIFT:D_SHIFT + MV_LORA],
                                    jnp.zeros((d, LANES - MV_LORA), F32)], axis=1)
        proj = norm_matmul(x2d, norm_mix[l], w_in.astype(BF16)).reshape(bsz, seq, -1)

        zeros = jnp.zeros((DECAY_LORA, 2 * D_RWKV), F32)
        dec = jnp.concatenate([decay_up[l, 0], decay_up[l, 1]], axis=1)
        icl = jnp.concatenate([iclr_up[l, 0], iclr_up[l, 1]], axis=1)
        w_lora = jnp.concatenate([
            jnp.concatenate([dec, zeros], axis=1), jnp.zeros((AAA_LORA, 4 * D_RWKV), F32),
            jnp.zeros((DECAY_LORA, 4 * D_RWKV), F32), jnp.concatenate([zeros, icl], axis=1)], axis=0)
        prm = dict(tshift=tshift[l], w_lora=w_lora,
                   b_lora=jnp.concatenate([decay_bias[l, 0], decay_bias[l, 1], iclr_bias[l, 0], iclr_bias[l, 1]]).reshape(1, -1),
                   gate_up=gate_up[l], k_k=k_k[l].reshape(1, -1), k_a=k_a[l].reshape(1, -1),
                   r_k=r_k[l].reshape(1, -1), bd=bd)
        if l > 0:
            prm["vres_bias"] = vres_bias[l - 1].reshape(1, -1)
            prm["vres_up"] = jnp.concatenate([vres_up[l - 1], jnp.zeros((LANES - MV_LORA, D_RWKV), F32)], axis=0)
        r, v, kk, lw, kd, bb, g, bonus = rwkv_prep(proj, v_first, prm)
        if l == 0:
            v_first = v
        yf, yb = wkv_scan(r, v, kk, lw, kd, bb)

        lam_init = 0.8 - 0.6 * math.exp(-0.3 * l)
        qa, qb, kr, vb = attn_prep(proj, cs, s1, s2)
        att = diff_attention(qa, qb, kr, vb, lambda_q[l], lambda_k[l], subln_w[l], lam_init)

        mix_bd = jax.scipy.linalg.block_diag(*[pool_mix[l, gi] for gi in range(len(POOL_WINDOWS))])
        pool = pool_mixer(proj, mix_bd, pool_scale[l])

        flat = lambda t: t.reshape(bsz * seq, -1)
        x2d = out_proj(x2d, flat(yf), flat(yb), flat(bonus), flat(g), flat(att), flat(pool), lnx_w[l], lnx_b[l], bd,
                       w_out[l].astype(BF16))
        i = l // 2
        if l % 2 == 0:
            x2d = ffn_dense(x2d, norm_ffn[l], ffn_gate[i].astype(BF16), ffn_up[i].astype(BF16), ffn_down[i].astype(BF16))
        else:
            wr_pad = jnp.concatenate([router[i], jnp.zeros((d, LANES - N_EXPERTS), F32)], axis=1)
            tm = min(2048, bsz * seq)
            cap = 5 * tm // 16
            h, comb, rank, cnt = moe_router(x2d, norm_ffn[l], wr_pad, tm)
            rank_t = rank[:, :N_EXPERTS].T.reshape(N_EXPERTS, 1, -1)
            cnt = cnt[:, 0, :N_EXPERTS].astype(jnp.int32)
            y = moe_ffn(h, comb, rank, rank_t, cnt, exp_gate[i].astype(BF16), exp_up[i].astype(BF16),
                        exp_down[i].astype(BF16), tm, cap)
            assert l == depth - 1, "the routed mixer's residual add is fused with the final norm"
            x2d = add_norm(x2d, y, norm_out)
    return x2d.reshape(bsz, seq, d)
```

```python
import functools
import math

import jax
import jax.numpy as jnp
from jax import lax
from jax.experimental import pallas as pl
from jax.experimental.pallas import tpu as pltpu

F32 = jnp.float32
BF16 = jnp.bfloat16
HI = lax.Precision.HIGHEST

D_MODEL = 1024
RWKV_HEADS = 4
HEAD_DIM = 64
D_RWKV = RWKV_HEADS * HEAD_DIM
DECAY_LORA = 64
AAA_LORA = 64
MV_LORA = 32
GATE_LORA = 128
D_SHIFT = 3 * D_RWKV + DECAY_LORA + AAA_LORA + GATE_LORA
W_DECAY_SCALE = 0.6065306597126334
GN_EPS = 64e-5
DIFF_HEADS = 4
DIFF_QK = 64
DIFF_V = 128
D_DIFF_QK = DIFF_HEADS * 2 * DIFF_QK
D_DIFF = DIFF_HEADS * DIFF_V
ROT_DIM = DIFF_QK // 4
ROPE_THETA = 500000.0
SUBLN_EPS = 1e-5
POOL_WINDOWS = (2, 4, 8, 16)
D_POOL = 256
D_FF = 3584
N_EXPERTS = 8
NORM_EPS = 1e-6
P_IN = D_SHIFT + 2 * D_DIFF_QK + D_DIFF + D_POOL
COL_Q, COL_K, COL_V, COL_POOL, COL_VDOWN = 1024, 1536, 2048, 2560, 2816

LANES = 128
HALO = 8
VMEM_LIMIT = 56 * 1024 * 1024

SCAN_CHUNK = 128


def _cparams(sem):
    return pltpu.CompilerParams(dimension_semantics=sem, vmem_limit_bytes=VMEM_LIMIT)


def _sigmoid(x):
    return 1.0 / (1.0 + jnp.exp(-x))


def _dot(a, b, **kw):
    return jnp.dot(a, b, preferred_element_type=F32, **kw)


def _bdot(a, b):
    return jnp.dot(a.astype(BF16), b.astype(BF16), preferred_element_type=F32)


def _rms(x, g):
    return x * lax.rsqrt(jnp.mean(x * x, axis=-1, keepdims=True) + NORM_EPS) * g


def _norm_matmul_kernel(x_ref, g_ref, w_ref, o_ref):
    h = _rms(x_ref[...], g_ref[...]).astype(BF16)
    o_ref[...] = _dot(h, w_ref[...])


def norm_matmul(x2d, g, w_bf16, tm=512):
    t, d = x2d.shape
    n = w_bf16.shape[1]
    return pl.pallas_call(
        _norm_matmul_kernel,
        grid=(t // tm,),
        in_specs=[
            pl.BlockSpec((tm, d), lambda i: (i, 0)),
            pl.BlockSpec((1, d), lambda i: (0, 0)),
            pl.BlockSpec((d, n), lambda i: (0, 0)),
        ],
        out_specs=pl.BlockSpec((tm, n), lambda i: (i, 0)),
        out_shape=jax.ShapeDtypeStruct((t, n), F32),
        compiler_params=_cparams(("parallel",)),
        name="norm_matmul",
    )(x2d, g.reshape(1, d), w_bf16)


def _shifted(u, prev_row, next_row):
    ts = u.shape[0]
    row = lax.broadcasted_iota(jnp.int32, u.shape, 0)
    prev = jnp.where(row == 0, prev_row, pltpu.roll(u, 1, 0))
    nxt = jnp.where(row == ts - 1, next_row, pltpu.roll(u, ts - 1, 0))
    return prev, nxt


def _rwkv_prep_kernel(has_vres, *refs):
    if has_vres:
        (u_ref, up_ref, un_ref, vd_ref, vf_ref, vb_ref, vu_ref, mu_ref, wl_ref, bias_ref, gu_ref,
         kkw_ref, ka_ref, rk_ref, bd_ref,
         r_ref, v_ref, kk_ref, lw_ref, kd_ref, bb_ref, g_ref, bonus_ref) = refs
    else:
        (u_ref, up_ref, un_ref, mu_ref, wl_ref, bias_ref, gu_ref,
         kkw_ref, ka_ref, rk_ref, bd_ref,
         r_ref, v_ref, kk_ref, lw_ref, kd_ref, bb_ref, g_ref, bonus_ref) = refs
    i = pl.program_id(1)
    n = pl.num_programs(1)
    u = u_ref[0]
    prev_row = jnp.where(i > 0, up_ref[0, HALO - 1:HALO, :], 0.0)
    next_row = jnp.where(i < n - 1, un_ref[0, 0:1, :], 0.0)
    prev, nxt = _shifted(u, prev_row, next_row)
    us = u + mu_ref[0:1, :] * (prev - u) + mu_ref[1:2, :] * (nxt - u)
    r = us[:, 0:256]
    k = us[:, 256:512]
    v = us[:, 512:768]
    xwa = us[:, 768:896]
    xg = us[:, 896:1024]
    lhs = jnp.concatenate([jnp.tanh(xwa), xwa], axis=1)
    z = _dot(lhs, wl_ref[...], precision=HI) + bias_ref[...]
    if has_vres:
        mix = _sigmoid(vb_ref[...] + _dot(vd_ref[0], vu_ref[...], precision=HI))
        v = v + (vf_ref[0] - v) * mix
    g = _dot(_sigmoid(xg), gu_ref[...], precision=HI)
    bd = bd_ref[...]
    kkr = k * kkw_ref[...]
    ss = _dot(kkr * kkr, bd, precision=HI)
    kk = kkr * lax.rsqrt(jnp.maximum(ss, 1e-24))
    r_ref[0] = r
    v_ref[0] = v
    kk_ref[0] = kk
    g_ref[0] = g
    bonus = jnp.zeros_like(v)
    for d in range(2):
        lw = -W_DECAY_SCALE * _sigmoid(z[:, 256 * d:256 * d + 256])
        a = _sigmoid(z[:, 512 + 256 * d:768 + 256 * d])
        kd = k * (1.0 + (a - 1.0) * ka_ref[...])
        lw_ref[d, 0] = lw
        kd_ref[d, 0] = kd
        bb_ref[d, 0] = kk * a
        bonus = bonus + _dot(r * kd * rk_ref[...], bd, precision=HI) * v
    bonus_ref[0] = bonus


def rwkv_prep(proj, v_first, p, ts=512):
    b, s, _ = proj.shape
    has_vres = v_first is not None
    nblk = ts // HALO
    last = s // HALO - 1
    row = lambda shape: pl.BlockSpec(shape, lambda bi, i: (0,) * len(shape))
    in_specs = [
        pl.BlockSpec((1, ts, D_SHIFT), lambda bi, i: (bi, i, 0)),
        pl.BlockSpec((1, HALO, D_SHIFT), lambda bi, i: (bi, jnp.maximum(i * nblk - 1, 0), 0)),
        pl.BlockSpec((1, HALO, D_SHIFT), lambda bi, i: (bi, jnp.minimum((i + 1) * nblk, last), 0)),
    ]
    args = [proj, proj, proj]
    if has_vres:
        in_specs += [
            pl.BlockSpec((1, ts, LANES), lambda bi, i: (bi, i, COL_VDOWN // LANES)),
            pl.BlockSpec((1, ts, D_RWKV), lambda bi, i: (bi, i, 0)),
            row((1, D_RWKV)), row((LANES, D_RWKV)),
        ]
        args += [proj, v_first, p["vres_bias"], p["vres_up"]]
    in_specs += [row((2, D_SHIFT)), row((256, 1024)), row((1, 1024)), row((GATE_LORA, D_RWKV)),
                 row((1, D_RWKV)), row((1, D_RWKV)), row((1, D_RWKV)), row((D_RWKV, D_RWKV))]
    args += [p["tshift"], p["w_lora"], p["b_lora"], p["gate_up"], p["k_k"], p["k_a"], p["r_k"], p["bd"]]
    one = pl.BlockSpec((1, ts, D_RWKV), lambda bi, i: (bi, i, 0))
    two = pl.BlockSpec((2, 1, ts, D_RWKV), lambda bi, i: (0, bi, i, 0))
    s1 = jax.ShapeDtypeStruct((b, s, D_RWKV), F32)
    s2 = jax.ShapeDtypeStruct((2, b, s, D_RWKV), F32)
    return pl.pallas_call(
        functools.partial(_rwkv_prep_kernel, has_vres),
        grid=(b, s // ts),
        in_specs=in_specs,
        out_specs=[one, one, one, two, two, two, one, one],
        out_shape=[s1, s1, s1, s2, s2, s2, s1, s1],
        compiler_params=_cparams(("parallel", "parallel")),
        name="rwkv_prep",
    )(*args)


def _unit_lower_inverses(ms, eye):
    xs = [eye - m for m in ms]
    ps = [_bdot(m, m) for m in ms]
    levels = int(math.log2(ms[0].shape[0])) - 1
    for k in range(levels):
        nxt = [_bdot(p, p) for p in ps] if k < levels - 1 else None
        xs = [x + _bdot(x, p) for x, p in zip(xs, ps)]
        ps = nxt
    return xs


def _split3(x):
    hi = x.astype(BF16)
    r1 = x - hi.astype(F32)
    mid = r1.astype(BF16)
    lo = (r1 - mid.astype(F32)).astype(BF16)
    return hi, mid, lo


def _chunk_terms(rev, r, v, kk, lw, kd, bb):
    c = SCAN_CHUNK
    n = len(r)
    ti = lax.broadcasted_iota(jnp.int32, (c, c), 0)
    tj = lax.broadcasted_iota(jnp.int32, (c, c), 1)
    eye = (ti == tj).astype(F32)
    same_head = (ti // HEAD_DIM) == (tj // HEAD_DIM)
    first = lax.broadcasted_iota(jnp.int32, (c, LANES), 1) < HEAD_DIM
    first2 = jnp.concatenate([first, first], axis=1)
    strict = [(ti - tj if rv else tj - ti) < 0 for rv in rev]
    incl = [(ti - tj if rv else tj - ti) <= 0 for rv in rev]

    cum = []
    for i in range(n):
        cum3 = _dot(jnp.where(incl[i], 1.0, 0.0).astype(BF16), jnp.concatenate(_split3(lw[i]), axis=1))
        cum.append(cum3[:, :LANES] + cum3[:, LANES:2 * LANES] + cum3[:, 2 * LANES:])
    a = [kk[i] * jnp.exp(cum[i] - lw[i]) for i in range(n)]
    rt = [r[i] * jnp.exp(cum[i]) for i in range(n)]
    einv = [jnp.exp(-cm) for cm in cum]
    kt_t = [(kd[i] * einv[i]).T for i in range(n)]
    bt_t = [(bb[i] * einv[i]).T for i in range(n)]
    last = [cum[i][0:1] if rev[i] else cum[i][c - 1:c] for i in range(n)]
    g_all = [jnp.exp(jnp.broadcast_to(x, (c, LANES))).T for x in last]
    vb = [x.astype(BF16) for x in v]
    rhs = [jnp.concatenate([bt_t[i], kt_t[i]], axis=1).astype(BF16) for i in range(n)]
    heads = [(i, hm) for i in range(n) for hm in (first, jnp.logical_not(first))]
    gm = [_dot(jnp.concatenate([jnp.where(hm, a[i], 0.0), jnp.where(hm, rt[i], 0.0)], axis=0).astype(BF16), rhs[i])
          for i, hm in heads]
    t_inv = _unit_lower_inverses([jnp.where(strict[i], g[:c, :c], 0.0) for (i, _), g in zip(heads, gm)], eye)
    mv = [_dot(jnp.where(strict[i], g[:c, c:], 0.0).astype(BF16), vb[i]) for (i, _), g in zip(heads, gm)]
    tam = [_bdot(t, jnp.concatenate([a[i], m], axis=1)) for (i, _), t, m in zip(heads, t_inv, mv)]
    lt = [_bdot(jnp.where(incl[i], g[c:, :c], 0.0), x) for (i, _), g, x in zip(heads, gm, tam)]
    lv = [_dot(jnp.where(incl[i], g[c:, c:], 0.0).astype(BF16), vb[i]) for (i, _), g in zip(heads, gm)]
    tam = [jnp.where(first2, tam[2 * i], tam[2 * i + 1]) for i in range(n)]
    lt = [jnp.where(first2, lt[2 * i], lt[2 * i + 1]) for i in range(n)]
    lv = [jnp.where(first, lv[2 * i], lv[2 * i + 1]) for i in range(n)]
    btx = [_bdot(bt_t[i], tam[i]) for i in range(n)]
    kv = [_dot(kt_t[i].astype(BF16), vb[i]) for i in range(n)]
    phi = [g_all[i] * (eye - jnp.where(same_head, btx[i][:, :LANES], 0.0)) for i in range(n)]
    psi = [g_all[i] * jnp.where(same_head, kv[i] - btx[i][:, LANES:], 0.0) for i in range(n)]
    ra = [rt[i] - lt[i][:, :LANES] for i in range(n)]
    oc = [lv[i] - lt[i][:, LANES:] for i in range(n)]
    return phi, psi, ra, oc


def _wkv_kernel(rf_ref, vf_ref, kkf_ref, lwf_ref, kdf_ref, bbf_ref,
                rb_ref, vb_ref, kkb_ref, lwb_ref, kdb_ref, bbb_ref, yf_ref, yb_ref, h_ref):
    @pl.when(pl.program_id(1) == 0)
    def _():
        h_ref[...] = jnp.zeros_like(h_ref)

    c = SCAN_CHUNK
    nsub = rf_ref.shape[1] // c
    pairs = (slice(0, LANES), slice(LANES, 2 * LANES))
    fwd = ((rf_ref, vf_ref, kkf_ref), (lwf_ref, kdf_ref, bbf_ref), yf_ref)
    bwd = ((rb_ref, vb_ref, kkb_ref), (lwb_ref, kdb_ref, bbb_ref), yb_ref)
    probs = [(False, pl.ds(c * j, c), sl) + fwd for sl in pairs for j in range(nsub)]
    probs += [(True, pl.ds(c * j, c), sl) + bwd for sl in pairs for j in reversed(range(nsub))]
    rev = [pr[0] for pr in probs]
    r, v, kk = ([pr[3][j][0, pr[1], pr[2]] for pr in probs] for j in range(3))
    lw, kd, bb = ([pr[4][j][0, 0, pr[1], pr[2]] for pr in probs] for j in range(3))
    phi, psi, ra, oc = _chunk_terms(rev, r, v, kk, lw, kd, bb)
    h = [h_ref[i] for i in range(len(probs) // nsub)]
    for j in range(nsub):
        res = []
        for i in range(len(h)):
            h_hi = h[i].astype(BF16)
            h_lo = (h[i] - h_hi.astype(F32)).astype(BF16)
            lhs = jnp.concatenate([phi[i * nsub + j], ra[i * nsub + j]], axis=0).astype(BF16)
            res.append(_dot(jnp.concatenate([lhs, lhs], axis=1), jnp.concatenate([h_hi, h_lo], axis=0)))
        for i in range(len(h)):
            pr = probs[i * nsub + j]
            h[i] = res[i][:c] + psi[i * nsub + j]
            pr[5][0, pr[1], pr[2]] = res[i][c:] + oc[i * nsub + j]
    for i in range(len(h)):
        h_ref[i] = h[i]


def wkv_scan(r, v, kk, lw, kd, bb, chunks_per_step=2):
    b, s, _ = r.shape
    c = SCAN_CHUNK * chunks_per_step
    nc = s // c
    fwd = pl.BlockSpec((1, c, D_RWKV), lambda bi, ci: (bi, ci, 0))
    bwd = pl.BlockSpec((1, c, D_RWKV), lambda bi, ci: (bi, nc - 1 - ci, 0))
    fwd_d = pl.BlockSpec((1, 1, c, D_RWKV), lambda bi, ci: (0, bi, ci, 0))
    bwd_d = pl.BlockSpec((1, 1, c, D_RWKV), lambda bi, ci: (1, bi, nc - 1 - ci, 0))
    out = jax.ShapeDtypeStruct((b, s, D_RWKV), F32)
    return pl.pallas_call(
        _wkv_kernel,
        grid=(b, nc),
        in_specs=[fwd, fwd, fwd, fwd_d, fwd_d, fwd_d, bwd, bwd, bwd, bwd_d, bwd_d, bwd_d],
        out_specs=[fwd, bwd],
        out_shape=[out, out],
        scratch_shapes=[pltpu.VMEM((4, LANES, LANES), F32)],
        compiler_params=_cparams(("parallel", "arbitrary")),
        name="wkv_scan",
    )(r, v, kk, lw, kd, bb, r, v, kk, lw, kd, bb)


def _attn_prep_kernel(q_ref, k_ref, v_ref, c_ref, s1_ref, s2_ref, qa_ref, qb_ref, ko_ref, vo_ref):
    cs, s1, s2 = c_ref[0], s1_ref[0], s2_ref[0]
    lane = lax.broadcasted_iota(jnp.int32, cs.shape, 1)
    first = lane < DIFF_QK
    scale = DIFF_QK ** -0.5 * math.log2(math.e)
    for j in range(D_DIFF_QK // LANES):
        sl = slice(LANES * j, LANES * j + LANES)

        def rope(t):
            return t * cs + pltpu.roll(t, ROT_DIM // 2, 1) * s1 + pltpu.roll(t, LANES - ROT_DIM // 2, 1) * s2

        q = rope(q_ref[0, :, sl]) * scale
        qa_ref[0, :, sl] = jnp.where(first, q, 0.0).astype(BF16)
        qb_ref[0, :, sl] = jnp.where(first, 0.0, q).astype(BF16)
        ko_ref[0, :, sl] = rope(k_ref[0, :, sl]).astype(BF16)
    ones = jnp.ones((v_ref.shape[1], DIFF_V), BF16)
    for h in range(DIFF_HEADS):
        vo_ref[0, :, 2 * DIFF_V * h:2 * DIFF_V * h + DIFF_V] = v_ref[0, :, DIFF_V * h:DIFF_V * h + DIFF_V].astype(BF16)
        vo_ref[0, :, 2 * DIFF_V * h + DIFF_V:2 * DIFF_V * (h + 1)] = ones


def attn_prep(proj, cs, s1, s2, ts=512):
    b, s, _ = proj.shape
    col = lambda off: pl.BlockSpec((1, ts, 512), lambda bi, i: (bi, i, off // 512))
    tab = pl.BlockSpec((1, ts, LANES), lambda bi, i: (bi, i, 0))
    out = pl.BlockSpec((1, ts, 512), lambda bi, i: (bi, i, 0))
    sh = jax.ShapeDtypeStruct((b, s, 512), BF16)
    return pl.pallas_call(
        _attn_prep_kernel,
        grid=(b, s // ts),
        in_specs=[col(COL_Q), col(COL_K), col(COL_V), tab, tab, tab],
        out_specs=[out, out, out, pl.BlockSpec((1, ts, 2 * D_DIFF), lambda bi, i: (bi, i, 0))],
        out_shape=[sh, sh, sh, jax.ShapeDtypeStruct((b, s, 2 * D_DIFF), BF16)],
        compiler_params=_cparams(("parallel", "parallel")),
        name="attn_prep",
    )(proj, proj, proj, cs, s1, s2)


def _attn_kernel(lam_init, q_rows, qa_ref, qb_ref, k_ref, v_ref, lq_ref, lk_ref, sw_ref, o_ref):
    kt = k_ref[0]
    vt = v_ref[0]
    e = jnp.exp(jnp.sum(lq_ref[...] * lk_ref[...], axis=1, keepdims=True))
    lam = e[0:1] - e[1:2] + lam_init
    blocks = [pl.ds(r0, q_rows) for r0 in range(0, qa_ref.shape[1], q_rows)]
    n = len(blocks)
    s, p = [None] * n, [None] * n
    for step in range(n + 2):
        if step < n:
            q2 = jnp.concatenate([qa_ref[0, blocks[step], :], qb_ref[0, blocks[step], :]], axis=0)
            s[step] = lax.dot_general(q2, kt, (((1,), (1,)), ((), ())), preferred_element_type=F32)
        if 0 <= step - 1 < n:
            p[step - 1] = jnp.exp2(s[step - 1] - jnp.max(s[step - 1], axis=1, keepdims=True)).astype(BF16)
            s[step - 1] = None
        if 0 <= step - 2 < n:
            pv = _dot(p[step - 2], vt)
            ratio = pv[:, :DIFF_V] / pv[:, DIFF_V:]
            o = ratio[:q_rows] - lam * ratio[q_rows:]
            o = o * lax.rsqrt(jnp.mean(o * o, axis=1, keepdims=True) + SUBLN_EPS) * sw_ref[...]
            o_ref[0, blocks[step - 2], :] = o * (1.0 - lam_init)
            p[step - 2] = None


def diff_attention(qa, qb, k, v, lq, lk, sw, lam_init, tq=1024, q_rows=128):
    b, s, _ = qa.shape
    tq = min(tq, s)
    qspec = pl.BlockSpec((1, tq, LANES), lambda bi, h, i: (bi, i, h))
    kspec = pl.BlockSpec((1, s, LANES), lambda bi, h, i: (bi, 0, h))
    small = lambda shape: pl.BlockSpec(shape, lambda bi, h, i: (0, 0))
    return pl.pallas_call(
        functools.partial(_attn_kernel, lam_init, min(q_rows, tq)),
        grid=(b, DIFF_HEADS, s // tq),
        in_specs=[qspec, qspec, kspec, pl.BlockSpec((1, s, 2 * DIFF_V), lambda bi, h, i: (bi, 0, h)),
                  small((2, DIFF_QK)), small((2, DIFF_QK)), small((1, DIFF_V))],
        out_specs=qspec,
        out_shape=jax.ShapeDtypeStruct((b, s, D_DIFF), F32),
        compiler_params=_cparams(("parallel", "parallel", "parallel")),
        name="diff_attention",
    )(qa, qb, k, v, lq, lk, sw.reshape(1, DIFF_V))


def _pool_kernel(seq, u_ref, up_ref, un_ref, mix_ref, scale_ref, o_ref):
    i = pl.program_id(1)
    n = pl.num_programs(1)
    u = u_ref[0]
    ts = u.shape[0]
    prev = jnp.where(i > 0, up_ref[0], 0.0)
    nxt = jnp.where(i < n - 1, un_ref[0], 0.0)
    ext = jnp.concatenate([prev, u, nxt], axis=0)
    rows = ext.shape[0]
    back = lambda t, sft: pltpu.roll(t, sft, 0)
    fwd = lambda t, sft: pltpu.roll(t, rows - sft, 0)
    w2 = ext + back(ext, 1)
    w4 = back(w2, 1) + fwd(w2, 1)
    w8 = back(w4, 2) + fwd(w4, 2)
    w16 = back(w8, 4) + fwd(w8, 4)
    lane = lax.broadcasted_iota(jnp.int32, (ts, D_POOL), 1)
    t = lax.broadcasted_iota(jnp.int32, (ts, D_POOL), 0) + i * ts
    grp = lane // HEAD_DIM
    sel = lambda vals: jnp.where(grp == 0, vals[0], jnp.where(grp == 1, vals[1], jnp.where(grp == 2, vals[2], vals[3])))
    wsum = sel([w[HALO:HALO + ts] for w in (w2, w4, w8, w16)])
    half = sel([wd // 2 for wd in POOL_WINDOWS])
    width = sel(list(POOL_WINDOWS))
    cnt = jnp.minimum(t + width - half, seq) - jnp.maximum(t - half, 0)
    pooled = wsum / cnt.astype(F32) - u
    o_ref[0] = _dot(pooled, mix_ref[...], precision=HI) * scale_ref[...]


def pool_mixer(proj, mix_bd, scale, ts=512):
    b, s, _ = proj.shape
    nblk = ts // HALO
    last = s // HALO - 1
    cb = COL_POOL // D_POOL
    hb = COL_POOL // D_POOL
    return pl.pallas_call(
        functools.partial(_pool_kernel, s),
        grid=(b, s // ts),
        in_specs=[
            pl.BlockSpec((1, ts, D_POOL), lambda bi, i: (bi, i, cb)),
            pl.BlockSpec((1, HALO, D_POOL), lambda bi, i: (bi, jnp.maximum(i * nblk - 1, 0), hb)),
            pl.BlockSpec((1, HALO, D_POOL), lambda bi, i: (bi, jnp.minimum((i + 1) * nblk, last), hb)),
            pl.BlockSpec((D_POOL, D_POOL), lambda bi, i: (0, 0)),
            pl.BlockSpec((1, D_POOL), lambda bi, i: (0, 0)),
        ],
        out_specs=pl.BlockSpec((1, ts, D_POOL), lambda bi, i: (bi, i, 0)),
        out_shape=jax.ShapeDtypeStruct((b, s, D_POOL), F32),
        compiler_params=_cparams(("parallel", "parallel")),
        name="pool_mixer",
    )(proj, proj, proj, mix_bd, scale.reshape(1, D_POOL))


def _out_proj_kernel(x_ref, yf_ref, yb_ref, bonus_ref, g_ref, att_ref, pool_ref, lw_ref, lb_ref, bd_ref, w_ref, o_ref):
    y = yf_ref[...] + yb_ref[...]
    bd = bd_ref[...]
    mu = _dot(y, bd, precision=HI) * (1.0 / HEAD_DIM)
    yc = y - mu
    var = _dot(yc * yc, bd, precision=HI) * (1.0 / HEAD_DIM)
    ya = (yc * lax.rsqrt(var + GN_EPS) * lw_ref[...] + lb_ref[...] + bonus_ref[...]) * g_ref[...]
    cat = jnp.concatenate([ya, att_ref[...], pool_ref[...]], axis=1).astype(BF16)
    o_ref[...] = x_ref[...] + _dot(cat, w_ref[...])


def out_proj(x2d, yf, yb, bonus, g, att, pool, lnx_w, lnx_b, bd, w_bf16, tm=512):
    t, d = x2d.shape
    rows = lambda n: pl.BlockSpec((tm, n), lambda i: (i, 0))
    fixed = lambda shape: pl.BlockSpec(shape, lambda i: (0, 0))
    return pl.pallas_call(
        _out_proj_kernel,
        grid=(t // tm,),
        in_specs=[rows(d),
                  rows(D_RWKV), rows(D_RWKV), rows(D_RWKV), rows(D_RWKV), rows(D_DIFF), rows(D_POOL),
                  fixed((1, D_RWKV)), fixed((1, D_RWKV)), fixed((D_RWKV, D_RWKV)), fixed((d, d))],
        out_specs=rows(d),
        out_shape=jax.ShapeDtypeStruct((t, d), F32),
        compiler_params=_cparams(("parallel",)),
        name="out_proj",
    )(x2d, yf, yb, bonus, g, att, pool, lnx_w.reshape(1, -1), lnx_b.reshape(1, -1), bd, w_bf16)


FF_TILE = 512
MOE_FF_TILE = 1792
TOP_K = 2


def _swiglu_tile(x, wg, wu, wd):
    gate = _dot(x, wg)
    return _dot((gate * _sigmoid(gate) * _dot(x, wu)).astype(BF16), wd)


def _ffn_kernel(x_ref, g_ref, wg_ref, wu_ref, wd_ref, o_ref, h_ref, acc_ref):
    f = pl.program_id(1)

    @pl.when(f == 0)
    def _():
        h_ref[...] = _rms(x_ref[...], g_ref[...]).astype(BF16)
        acc_ref[...] = jnp.zeros_like(acc_ref)

    acc_ref[...] += _swiglu_tile(h_ref[...], wg_ref[...], wu_ref[...], wd_ref[...])

    @pl.when(f == pl.num_programs(1) - 1)
    def _():
        o_ref[...] = x_ref[...] + acc_ref[...]


def ffn_dense(x2d, g, wg, wu, wd, tm=1024):
    t, d = x2d.shape
    ff = wg.shape[1]
    return pl.pallas_call(
        _ffn_kernel,
        grid=(t // tm, ff // FF_TILE),
        in_specs=[pl.BlockSpec((tm, d), lambda i, f: (i, 0)),
                  pl.BlockSpec((1, d), lambda i, f: (0, 0)),
                  pl.BlockSpec((d, FF_TILE), lambda i, f: (0, f)),
                  pl.BlockSpec((d, FF_TILE), lambda i, f: (0, f)),
                  pl.BlockSpec((FF_TILE, d), lambda i, f: (f, 0))],
        out_specs=pl.BlockSpec((tm, d), lambda i, f: (i, 0)),
        out_shape=jax.ShapeDtypeStruct((t, d), F32),
        scratch_shapes=[pltpu.VMEM((tm, d), BF16), pltpu.VMEM((tm, d), F32)],
        compiler_params=_cparams(("parallel", "arbitrary")),
        name="ffn_dense",
    )(x2d, g.reshape(1, d), wg, wu, wd)


def _router_kernel(x_ref, g_ref, wr_ref, h_ref, comb_ref, rank_ref, cnt_ref):
    h = _rms(x_ref[...], g_ref[...])
    h_ref[...] = h.astype(BF16)
    logits = _dot(h, wr_ref[...], precision=HI)
    lane = lax.broadcasted_iota(jnp.int32, logits.shape, 1)
    logits = jnp.where(lane < N_EXPERTS, logits, -jnp.inf)
    v1 = jnp.max(logits, axis=1, keepdims=True)
    i1 = jnp.min(jnp.where(logits == v1, lane, LANES), axis=1, keepdims=True)
    rest = jnp.where(lane == i1, -jnp.inf, logits)
    v2 = jnp.max(rest, axis=1, keepdims=True)
    i2 = jnp.min(jnp.where(rest == v2, lane, LANES), axis=1, keepdims=True)
    e2 = jnp.exp(v2 - v1)
    g1 = 1.0 / (1.0 + e2)
    comb = jnp.where(lane == i1, g1, jnp.where(lane == i2, e2 * g1, 0.0))
    comb_ref[...] = comb
    tm = comb.shape[0]
    routed = comb > 0.0
    mask = jnp.where(routed, 1.0, 0.0).astype(BF16)
    earlier = lax.broadcasted_iota(jnp.int32, (tm, tm), 1) < lax.broadcasted_iota(jnp.int32, (tm, tm), 0)
    rank = _dot(jnp.where(earlier, 1.0, 0.0).astype(BF16), mask)
    rank_ref[...] = jnp.where(routed, rank, -1.0)
    cnt_ref[0] = _dot(jnp.ones((HALO, tm), BF16), mask)


def moe_router(x2d, g, wr_pad, tm):
    t, d = x2d.shape
    tok = pl.BlockSpec((tm, LANES), lambda i: (i, 0))
    return pl.pallas_call(
        _router_kernel,
        grid=(t // tm,),
        in_specs=[pl.BlockSpec((tm, d), lambda i: (i, 0)),
                  pl.BlockSpec((1, d), lambda i: (0, 0)),
                  pl.BlockSpec((d, LANES), lambda i: (0, 0))],
        out_specs=[pl.BlockSpec((tm, d), lambda i: (i, 0)), tok, tok,
                   pl.BlockSpec((1, HALO, LANES), lambda i: (i, 0, 0))],
        out_shape=[jax.ShapeDtypeStruct((t, d), BF16), jax.ShapeDtypeStruct((t, LANES), F32),
                   jax.ShapeDtypeStruct((t, LANES), F32), jax.ShapeDtypeStruct((t // tm, HALO, LANES), F32)],
        compiler_params=_cparams(("parallel",)),
        name="moe_router",
    )(x2d, g.reshape(1, d), wr_pad)


def _moe_kernel(item_ref, h_ref, comb_ref, rank_ref, rank_t_ref, wg_ref, wu_ref, wd_ref, o_ref,
                xe_ref, ye_ref, gc_ref):
    i, k, f = pl.program_id(0), pl.program_id(1), pl.program_id(2)
    tm = h_ref.shape[0]
    cap = xe_ref.shape[0]
    e, j, on = item_ref[0, i, k], item_ref[1, i, k], item_ref[2, i, k] > 0
    first, last = f == 0, f == pl.num_programs(2) - 1

    @pl.when(jnp.logical_and(k == 0, first))
    def _():
        o_ref[...] = jnp.zeros_like(o_ref)

    def lane_column(x):
        onehot = jnp.where(lax.broadcasted_iota(jnp.int32, (LANES, LANES), 0) == e, 1.0, 0.0).astype(BF16)
        hi = x.astype(BF16)
        lo = (x - hi.astype(F32)).astype(BF16)
        both = _dot(jnp.concatenate([hi, lo], axis=0), onehot)
        return both[:x.shape[0]] + both[x.shape[0]:]

    @pl.when(jnp.logical_and(on, first))
    def _():
        gate = lane_column(comb_ref[...])
        g_hi = gate.astype(BF16)
        g_lo = (gate - g_hi.astype(F32)).astype(BF16)
        rid = lax.broadcasted_iota(jnp.int32, (cap, tm), 0) + j * cap
        pick = jnp.where(rank_t_ref[0] == rid.astype(F32), 1.0, 0.0).astype(BF16)
        xe_ref[...] = _dot(pick, h_ref[...]).astype(BF16)
        gc2 = _dot(pick, jnp.concatenate([g_hi, g_lo], axis=1))
        gc_ref[...] = gc2[:, :LANES] + gc2[:, LANES:]

    @pl.when(on)
    def _():
        y = _swiglu_tile(xe_ref[...], wg_ref[0], wu_ref[0], wd_ref[0])
        ye_ref[...] = jnp.where(first, y, ye_ref[...] + y)

    @pl.when(jnp.logical_and(on, last))
    def _():
        slot = jnp.concatenate([lane_column(rank_ref[...])] * (cap // LANES), axis=1)
        cid = lax.broadcasted_iota(jnp.int32, (tm, cap), 1) + j * cap
        place = jnp.where(slot == cid.astype(F32), 1.0, 0.0).astype(BF16)
        ys = ye_ref[...] * jnp.concatenate([gc_ref[...]] * (ye_ref.shape[1] // LANES), axis=1)
        o_ref[...] += _dot(place, ys.astype(BF16))


def _moe_items(cnt, cap, n_items):
    nblk = (cnt + cap - 1) // cap
    ends = jnp.cumsum(nblk, axis=1)
    k = jnp.arange(n_items, dtype=jnp.int32)
    total = ends[:, -1:]
    kk = jnp.minimum(k[None, :], total - 1)
    e = jnp.sum((kk[:, :, None] >= ends[:, None, :]).astype(jnp.int32), axis=2)
    start = jnp.take_along_axis(ends - nblk, e, axis=1)
    return jnp.stack([e, kk - start, (k[None, :] < total).astype(jnp.int32)]).astype(jnp.int32)


def moe_ffn(h, comb, rank, rank_t, cnt, wg, wu, wd, tm, cap):
    t, d = h.shape
    ne, _, ff = wg.shape
    tf = MOE_FF_TILE
    nf = ff // tf
    n_items = ne + (TOP_K * tm) // cap
    items = _moe_items(cnt, cap, n_items)
    tok = lambda n: pl.BlockSpec((tm, n), lambda i, k, f, it: (i, 0), pipeline_mode=pl.Buffered(1))
    fidx = lambda k, f, it, i: jnp.where(it[2, i, k] > 0, f, nf - 1)
    grid_spec = pltpu.PrefetchScalarGridSpec(
        num_scalar_prefetch=1,
        grid=(t // tm, n_items, nf),
        in_specs=[tok(d), tok(LANES), tok(LANES),
                  pl.BlockSpec((1, 1, tm), lambda i, k, f, it: (it[0, i, k], 0, i)),
                  pl.BlockSpec((1, d, tf), lambda i, k, f, it: (it[0, i, k], 0, fidx(k, f, it, i))),
                  pl.BlockSpec((1, d, tf), lambda i, k, f, it: (it[0, i, k], 0, fidx(k, f, it, i))),
                  pl.BlockSpec((1, tf, d), lambda i, k, f, it: (it[0, i, k], fidx(k, f, it, i), 0))],
        out_specs=tok(d),
        scratch_shapes=[pltpu.VMEM((cap, d), BF16), pltpu.VMEM((cap, d), F32), pltpu.VMEM((cap, LANES), F32)],
    )
    return pl.pallas_call(
        _moe_kernel,
        grid_spec=grid_spec,
        out_shape=jax.ShapeDtypeStruct((t, d), F32),
        compiler_params=_cparams(("parallel", "arbitrary", "arbitrary")),
        name="moe_ffn",
    )(items, h, comb, rank, rank_t, wg, wu, wd)


def _add_norm_kernel(x_ref, y_ref, g_ref, o_ref):
    o_ref[...] = _rms(x_ref[...] + y_ref[...], g_ref[...])


def add_norm(x2d, y2d, g, tm=1024):
    t, d = x2d.shape
    row = pl.BlockSpec((tm, d), lambda i: (i, 0))
    return pl.pallas_call(
        _add_norm_kernel,
        grid=(t // tm,),
        in_specs=[row, row, pl.BlockSpec((1, d), lambda i: (0, 0))],
        out_specs=row,
        out_shape=jax.ShapeDtypeStruct((t, d), F32),
        compiler_params=_cparams(("parallel",)),
        name="add_norm",
    )(x2d, y2d, g.reshape(1, d))


def _block_diag_ones(n, blk):
    i = jnp.arange(n) // blk
    return (i[:, None] == i[None, :]).astype(F32)


def _rope_tables(positions):
    half = ROT_DIM // 2
    inv_freq = jnp.power(ROPE_THETA, -(jnp.arange(half, dtype=F32) * 2.0 / ROT_DIM))
    ang = positions.astype(F32)[..., None] * inv_freq
    cos, sin = jnp.cos(ang), jnp.sin(ang)
    one = jnp.ones(cos.shape[:-1] + (DIFF_QK - ROT_DIM,), F32)
    zero = jnp.zeros_like(one)
    z8 = jnp.zeros_like(sin)
    cs = jnp.concatenate([cos, cos, one], axis=-1)
    s1 = jnp.concatenate([z8, sin, zero], axis=-1)
    s2 = jnp.concatenate([-sin, z8, zero], axis=-1)
    return tuple(jnp.concatenate([t, t], axis=-1) for t in (cs, s1, s2))


def kernel(x, positions, norm_mix, w_in_first, w_in_rest, tshift, decay_bias, decay_up, iclr_bias, iclr_up, gate_up, k_k, k_a, r_k, lnx_w, lnx_b, vres_bias, vres_up, lambda_q, lambda_k, subln_w, pool_mix, pool_scale, w_out, norm_ffn, ffn_gate, ffn_up, ffn_down, router, exp_gate, exp_up, exp_down, norm_out):
    bsz, seq, d = x.shape
    depth = norm_mix.shape[0]
    x2d = x.reshape(bsz * seq, d)
    bd = _block_diag_ones(D_RWKV, HEAD_DIM)
    cs, s1, s2 = _rope_tables(positions)
    v_first = None
    for l in range(depth):
        if l == 0:
            w_in = w_in_first
        else:
            wr = w_in_rest[l - 1]
            w_in = jnp.concatenate([wr[:, :D_SHIFT], wr[:, D_SHIFT + MV_LORA:], wr[:, D_SHIFT:D_SHIFT + MV_LORA],
                                    jnp.zeros((d, LANES - MV_LORA), F32)], axis=1)
        proj = norm_matmul(x2d, norm_mix[l], w_in.astype(BF16)).reshape(bsz, seq, -1)

        zeros = jnp.zeros((DECAY_LORA, 2 * D_RWKV), F32)
        dec = jnp.concatenate([decay_up[l, 0], decay_up[l, 1]], axis=1)
        icl = jnp.concatenate([iclr_up[l, 0], iclr_up[l, 1]], axis=1)
        w_lora = jnp.concatenate([
            jnp.concatenate([dec, zeros], axis=1), jnp.zeros((AAA_LORA, 4 * D_RWKV), F32),
            jnp.zeros((DECAY_LORA, 4 * D_RWKV), F32), jnp.concatenate([zeros, icl], axis=1)], axis=0)
        prm = dict(tshift=tshift[l], w_lora=w_lora,
                   b_lora=jnp.concatenate([decay_bias[l, 0], decay_bias[l, 1], iclr_bias[l, 0], iclr_bias[l, 1]]).reshape(1, -1),
                   gate_up=gate_up[l], k_k=k_k[l].reshape(1, -1), k_a=k_a[l].reshape(1, -1),
                   r_k=r_k[l].reshape(1, -1), bd=bd)
        if l > 0:
            prm["vres_bias"] = vres_bias[l - 1].reshape(1, -1)
            prm["vres_up"] = jnp.concatenate([vres_up[l - 1], jnp.zeros((LANES - MV_LORA, D_RWKV), F32)], axis=0)
        r, v, kk, lw, kd, bb, g, bonus = rwkv_prep(proj, v_first, prm)
        if l == 0:
            v_first = v
        yf, yb = wkv_scan(r, v, kk, lw, kd, bb)

        lam_init = 0.8 - 0.6 * math.exp(-0.3 * l)
        qa, qb, kr, vb = attn_prep(proj, cs, s1, s2)
        att = diff_attention(qa, qb, kr, vb, lambda_q[l], lambda_k[l], subln_w[l], lam_init)

        mix_bd = jax.scipy.linalg.block_diag(*[pool_mix[l, gi] for gi in range(len(POOL_WINDOWS))])
        pool = pool_mixer(proj, mix_bd, pool_scale[l])

        flat = lambda t: t.reshape(bsz * seq, -1)
        x2d = out_proj(x2d, flat(yf), flat(yb), flat(bonus), flat(g), flat(att), flat(pool), lnx_w[l], lnx_b[l], bd,
                       w_out[l].astype(BF16))
        i = l // 2
        if l % 2 == 0:
            x2d = ffn_dense(x2d, norm_ffn[l], ffn_gate[i].astype(BF16), ffn_up[i].astype(BF16), ffn_down[i].astype(BF16))
        else:
            wr_pad = jnp.concatenate([router[i], jnp.zeros((d, LANES - N_EXPERTS), F32)], axis=1)
            tm = min(2048, bsz * seq)
            cap = 5 * tm // 16
            h, comb, rank, cnt = moe_router(x2d, norm_ffn[l], wr_pad, tm)
            rank_t = rank[:, :N_EXPERTS].T.reshape(N_EXPERTS, 1, -1)
            cnt = cnt[:, 0, :N_EXPERTS].astype(jnp.int32)
            y = moe_ffn(h, comb, rank, rank_t, cnt, exp_gate[i].astype(BF16), exp_up[i].astype(BF16),
                        exp_down[i].astype(BF16), tm, cap)
            assert l == depth - 1, "the routed mixer's residual add is fused with the final norm"
            x2d = add_norm(x2d, y, norm_out)
    return x2d.reshape(bsz, seq, d)
```

```python
import functools
import math

import jax
import jax.numpy as jnp
from jax import lax
from jax.experimental import pallas as pl
from jax.experimental.pallas import tpu as pltpu

F32 = jnp.float32
BF16 = jnp.bfloat16
HI = lax.Precision.HIGHEST

D_MODEL = 1024
RWKV_HEADS = 4
HEAD_DIM = 64
D_RWKV = RWKV_HEADS * HEAD_DIM
DECAY_LORA = 64
AAA_LORA = 64
MV_LORA = 32
GATE_LORA = 128
D_SHIFT = 3 * D_RWKV + DECAY_LORA + AAA_LORA + GATE_LORA
W_DECAY_SCALE = 0.6065306597126334
GN_EPS = 64e-5
DIFF_HEADS = 4
DIFF_QK = 64
DIFF_V = 128
D_DIFF_QK = DIFF_HEADS * 2 * DIFF_QK
D_DIFF = DIFF_HEADS * DIFF_V
ROT_DIM = DIFF_QK // 4
ROPE_THETA = 500000.0
SUBLN_EPS = 1e-5
POOL_WINDOWS = (2, 4, 8, 16)
D_POOL = 256
D_FF = 3584
N_EXPERTS = 8
NORM_EPS = 1e-6
P_IN = D_SHIFT + 2 * D_DIFF_QK + D_DIFF + D_POOL
COL_Q, COL_K, COL_V, COL_POOL = 1024, 1536, 2048, 2560
REST_POOL, REST_VDOWN = D_SHIFT, D_SHIFT + D_POOL

LANES = 128
HALO = 8
VMEM_LIMIT = 56 * 1024 * 1024

SCAN_CHUNK = 128


def _cparams(sem):
    return pltpu.CompilerParams(dimension_semantics=sem, vmem_limit_bytes=VMEM_LIMIT)


def _sigmoid(x):
    return 1.0 / (1.0 + jnp.exp(-x))


def _dot(a, b, **kw):
    return jnp.dot(a, b, preferred_element_type=F32, **kw)


def _bdot(a, b):
    return jnp.dot(a.astype(BF16), b.astype(BF16), preferred_element_type=F32)


def _rms(x, g):
    return x * lax.rsqrt(jnp.mean(x * x, axis=-1, keepdims=True) + NORM_EPS) * g


def _hi_lo(x):
    hi = x.astype(BF16)
    return hi, (x - hi.astype(F32)).astype(BF16)


def _dot_exact_rhs(a, w):
    hi, lo = _hi_lo(a)
    both = _dot(jnp.concatenate([hi, lo], axis=0), w.astype(BF16))
    return both[:a.shape[0]] + both[a.shape[0]:]


def _dot3(a, w):
    a_hi, a_lo = _hi_lo(a)
    w_hi, w_lo = _hi_lo(w)
    both = _dot(jnp.concatenate([a_hi, a_lo], axis=0), w_hi)
    return both[:a.shape[0]] + both[a.shape[0]:] + _dot(a_hi, w_lo)


def _in_proj_kernel(x_ref, g_ref, w_ref, c_ref, s1_ref, s2_ref, rest_ref, qa_ref, qb_ref, ko_ref, vo_ref):
    h = _rms(x_ref[...], g_ref[...]).astype(BF16)
    p = _dot(h, w_ref[...])
    rest_ref[:, :D_SHIFT] = p[:, :D_SHIFT]
    rest_ref[:, D_SHIFT:] = p[:, COL_POOL:]
    cs, s1, s2 = c_ref[...], s1_ref[...], s2_ref[...]
    first = lax.broadcasted_iota(jnp.int32, cs.shape, 1) < DIFF_QK
    scale = DIFF_QK ** -0.5 * math.log2(math.e)

    def rope(t):
        return t * cs + pltpu.roll(t, ROT_DIM // 2, 1) * s1 + pltpu.roll(t, LANES - ROT_DIM // 2, 1) * s2

    for j in range(D_DIFF_QK // LANES):
        sl = slice(LANES * j, LANES * j + LANES)
        q = rope(p[:, COL_Q + LANES * j:COL_Q + LANES * (j + 1)]) * scale
        qa_ref[:, sl] = jnp.where(first, q, 0.0).astype(BF16)
        qb_ref[:, sl] = jnp.where(first, 0.0, q).astype(BF16)
        ko_ref[:, sl] = rope(p[:, COL_K + LANES * j:COL_K + LANES * (j + 1)]).astype(BF16)
    ones = jnp.ones((p.shape[0], DIFF_V), BF16)
    for hd in range(DIFF_HEADS):
        vo_ref[:, 2 * DIFF_V * hd:2 * DIFF_V * hd + DIFF_V] = p[:, COL_V + DIFF_V * hd:COL_V + DIFF_V * (hd + 1)].astype(BF16)
        vo_ref[:, 2 * DIFF_V * hd + DIFF_V:2 * DIFF_V * (hd + 1)] = ones


def in_proj(x2d, g, w_bf16, cs, s1, s2, tm=512):
    t, d = x2d.shape
    n = w_bf16.shape[1]
    n_rest = n - (COL_POOL - D_SHIFT)
    rows = lambda width: pl.BlockSpec((tm, width), lambda i: (i, 0))
    fixed = lambda shape: pl.BlockSpec(shape, lambda i: (0, 0))
    qkv = jax.ShapeDtypeStruct((t, D_DIFF_QK), BF16)
    return pl.pallas_call(
        _in_proj_kernel,
        grid=(t // tm,),
        in_specs=[rows(d), fixed((1, d)), fixed((d, n)), rows(LANES), rows(LANES), rows(LANES)],
        out_specs=[rows(n_rest), rows(D_DIFF_QK), rows(D_DIFF_QK), rows(D_DIFF_QK), rows(2 * D_DIFF)],
        out_shape=[jax.ShapeDtypeStruct((t, n_rest), F32), qkv, qkv, qkv, jax.ShapeDtypeStruct((t, 2 * D_DIFF), BF16)],
        compiler_params=_cparams(("parallel",)),
        name="in_proj",
    )(x2d, g.reshape(1, d), w_bf16, cs, s1, s2)


def _shifted(u, prev_row, next_row):
    ts = u.shape[0]
    row = lax.broadcasted_iota(jnp.int32, u.shape, 0)
    prev = jnp.where(row == 0, prev_row, pltpu.roll(u, 1, 0))
    nxt = jnp.where(row == ts - 1, next_row, pltpu.roll(u, ts - 1, 0))
    return prev, nxt


def _rwkv_prep_kernel(has_vres, *refs):
    if has_vres:
        (u_ref, up_ref, un_ref, vd_ref, vf_ref, vb_ref, vu_ref, mu_ref, wl_ref, bias_ref, gu_ref,
         kkw_ref, ka_ref, rk_ref, bd_ref,
         r_ref, v_ref, kk_ref, lw_ref, kd_ref, bb_ref, g_ref, bonus_ref) = refs
    else:
        (u_ref, up_ref, un_ref, mu_ref, wl_ref, bias_ref, gu_ref,
         kkw_ref, ka_ref, rk_ref, bd_ref,
         r_ref, v_ref, kk_ref, lw_ref, kd_ref, bb_ref, g_ref, bonus_ref) = refs
    i = pl.program_id(1)
    n = pl.num_programs(1)
    u = u_ref[0]
    prev_row = jnp.where(i > 0, up_ref[0, HALO - 1:HALO, :], 0.0)
    next_row = jnp.where(i < n - 1, un_ref[0, 0:1, :], 0.0)
    prev, nxt = _shifted(u, prev_row, next_row)
    us = u + mu_ref[0:1, :] * (prev - u) + mu_ref[1:2, :] * (nxt - u)
    r = us[:, 0:256]
    k = us[:, 256:512]
    v = us[:, 512:768]
    xwa = us[:, 768:896]
    xg = us[:, 896:1024]
    lhs = jnp.concatenate([jnp.tanh(xwa), xwa], axis=1)
    z = _dot3(lhs, wl_ref[...]) + bias_ref[...]
    if has_vres:
        mix = _sigmoid(vb_ref[...] + _dot3(vd_ref[0], vu_ref[...]))
        v = v + (vf_ref[0] - v) * mix
    g = _dot3(_sigmoid(xg), gu_ref[...])
    bd = bd_ref[...]
    kkr = k * kkw_ref[...]
    ss = _dot_exact_rhs(kkr * kkr, bd)
    kk = kkr * lax.rsqrt(jnp.maximum(ss, 1e-24))
    r_ref[0] = r
    v_ref[0] = v
    kk_ref[0] = kk
    g_ref[0] = g
    bonus = jnp.zeros_like(v)
    for d in range(2):
        lw = -W_DECAY_SCALE * _sigmoid(z[:, 256 * d:256 * d + 256])
        a = _sigmoid(z[:, 512 + 256 * d:768 + 256 * d])
        kd = k * (1.0 + (a - 1.0) * ka_ref[...])
        lw_ref[d, 0] = lw
        kd_ref[d, 0] = kd
        bb_ref[d, 0] = kk * a
        bonus = bonus + _dot_exact_rhs(r * kd * rk_ref[...], bd) * v
    bonus_ref[0] = bonus


def rwkv_prep(proj, v_first, p, ts=512):
    b, s, _ = proj.shape
    has_vres = v_first is not None
    nblk = ts // HALO
    last = s // HALO - 1
    row = lambda shape: pl.BlockSpec(shape, lambda bi, i: (0,) * len(shape))
    in_specs = [
        pl.BlockSpec((1, ts, D_SHIFT), lambda bi, i: (bi, i, 0)),
        pl.BlockSpec((1, HALO, D_SHIFT), lambda bi, i: (bi, jnp.maximum(i * nblk - 1, 0), 0)),
        pl.BlockSpec((1, HALO, D_SHIFT), lambda bi, i: (bi, jnp.minimum((i + 1) * nblk, last), 0)),
    ]
    args = [proj, proj, proj]
    if has_vres:
        in_specs += [
            pl.BlockSpec((1, ts, LANES), lambda bi, i: (bi, i, REST_VDOWN // LANES)),
            pl.BlockSpec((1, ts, D_RWKV), lambda bi, i: (bi, i, 0)),
            row((1, D_RWKV)), row((LANES, D_RWKV)),
        ]
        args += [proj, v_first, p["vres_bias"], p["vres_up"]]
    in_specs += [row((2, D_SHIFT)), row((256, 1024)), row((1, 1024)), row((GATE_LORA, D_RWKV)),
                 row((1, D_RWKV)), row((1, D_RWKV)), row((1, D_RWKV)), row((D_RWKV, D_RWKV))]
    args += [p["tshift"], p["w_lora"], p["b_lora"], p["gate_up"], p["k_k"], p["k_a"], p["r_k"], p["bd"]]
    one = pl.BlockSpec((1, ts, D_RWKV), lambda bi, i: (bi, i, 0))
    two = pl.BlockSpec((2, 1, ts, D_RWKV), lambda bi, i: (0, bi, i, 0))
    s1 = jax.ShapeDtypeStruct((b, s, D_RWKV), F32)
    s2 = jax.ShapeDtypeStruct((2, b, s, D_RWKV), F32)
    return pl.pallas_call(
        functools.partial(_rwkv_prep_kernel, has_vres),
        grid=(b, s // ts),
        in_specs=in_specs,
        out_specs=[one, one, one, two, two, two, one, one],
        out_shape=[s1, s1, s1, s2, s2, s2, s1, s1],
        compiler_params=_cparams(("parallel", "parallel")),
        name="rwkv_prep",
    )(*args)


def _unit_lower_inverses(ms, eye):
    xs = [eye - m for m in ms]
    ps = [_bdot(m, m) for m in ms]
    levels = int(math.log2(ms[0].shape[0])) - 1
    for k in range(levels):
        nxt = [_bdot(p, p) for p in ps] if k < levels - 1 else None
        xs = [x + _bdot(x, p) for x, p in zip(xs, ps)]
        ps = nxt
    return xs


def _split3(x):
    hi = x.astype(BF16)
    r1 = x - hi.astype(F32)
    mid = r1.astype(BF16)
    lo = (r1 - mid.astype(F32)).astype(BF16)
    return hi, mid, lo


def _chunk_terms(rev, r, v, kk, lw, kd, bb):
    c = SCAN_CHUNK
    n = len(r)
    ti = lax.broadcasted_iota(jnp.int32, (c, c), 0)
    tj = lax.broadcasted_iota(jnp.int32, (c, c), 1)
    eye = (ti == tj).astype(F32)
    same_head = (ti // HEAD_DIM) == (tj // HEAD_DIM)
    first = lax.broadcasted_iota(jnp.int32, (c, LANES), 1) < HEAD_DIM
    first2 = jnp.concatenate([first, first], axis=1)
    strict = [(ti - tj if rv else tj - ti) < 0 for rv in rev]
    incl = [(ti - tj if rv else tj - ti) <= 0 for rv in rev]

    cum = []
    for i in range(n):
        cum3 = _dot(jnp.where(incl[i], 1.0, 0.0).astype(BF16), jnp.concatenate(_split3(lw[i]), axis=1))
        cum.append(cum3[:, :LANES] + cum3[:, LANES:2 * LANES] + cum3[:, 2 * LANES:])
    a = [kk[i] * jnp.exp(cum[i] - lw[i]) for i in range(n)]
    rt = [r[i] * jnp.exp(cum[i]) for i in range(n)]
    einv = [jnp.exp(-cm) for cm in cum]
    kt_t = [(kd[i] * einv[i]).T for i in range(n)]
    bt_t = [(bb[i] * einv[i]).T for i in range(n)]
    last = [cum[i][0:1] if rev[i] else cum[i][c - 1:c] for i in range(n)]
    g_all = [jnp.exp(jnp.broadcast_to(x, (c, LANES))).T for x in last]
    vb = [x.astype(BF16) for x in v]
    rhs = [jnp.concatenate([bt_t[i], kt_t[i]], axis=1).astype(BF16) for i in range(n)]
    heads = [(i, hm) for i in range(n) for hm in (first, jnp.logical_not(first))]
    gm = [_dot(jnp.concatenate([jnp.where(hm, a[i], 0.0), jnp.where(hm, rt[i], 0.0)], axis=0).astype(BF16), rhs[i])
          for i, hm in heads]
    t_inv = _unit_lower_inverses([jnp.where(strict[i], g[:c, :c], 0.0) for (i, _), g in zip(heads, gm)], eye)
    mv = [_dot(jnp.where(strict[i], g[:c, c:], 0.0).astype(BF16), vb[i]) for (i, _), g in zip(heads, gm)]
    tam = [_bdot(t, jnp.concatenate([a[i], m], axis=1)) for (i, _), t, m in zip(heads, t_inv, mv)]
    lt = [_bdot(jnp.where(incl[i], g[c:, :c], 0.0), x) for (i, _), g, x in zip(heads, gm, tam)]
    lv = [_dot(jnp.where(incl[i], g[c:, c:], 0.0).astype(BF16), vb[i]) for (i, _), g in zip(heads, gm)]
    tam = [jnp.where(first2, tam[2 * i], tam[2 * i + 1]) for i in range(n)]
    lt = [jnp.where(first2, lt[2 * i], lt[2 * i + 1]) for i in range(n)]
    lv = [jnp.where(first, lv[2 * i], lv[2 * i + 1]) for i in range(n)]
    btx = [_bdot(bt_t[i], tam[i]) for i in range(n)]
    kv = [_dot(kt_t[i].astype(BF16), vb[i]) for i in range(n)]
    phi = [g_all[i] * (eye - jnp.where(same_head, btx[i][:, :LANES], 0.0)) for i in range(n)]
    psi = [g_all[i] * jnp.where(same_head, kv[i] - btx[i][:, LANES:], 0.0) for i in range(n)]
    ra = [rt[i] - lt[i][:, :LANES] for i in range(n)]
    oc = [lv[i] - lt[i][:, LANES:] for i in range(n)]
    return phi, psi, ra, oc


def _wkv_kernel(rf_ref, vf_ref, kkf_ref, lwf_ref, kdf_ref, bbf_ref,
                rb_ref, vb_ref, kkb_ref, lwb_ref, kdb_ref, bbb_ref, yf_ref, yb_ref, h_ref):
    @pl.when(pl.program_id(1) == 0)
    def _():
        h_ref[...] = jnp.zeros_like(h_ref)

    c = SCAN_CHUNK
    nsub = rf_ref.shape[1] // c
    pairs = (slice(0, LANES), slice(LANES, 2 * LANES))
    fwd = ((rf_ref, vf_ref, kkf_ref), (lwf_ref, kdf_ref, bbf_ref), yf_ref)
    bwd = ((rb_ref, vb_ref, kkb_ref), (lwb_ref, kdb_ref, bbb_ref), yb_ref)
    probs = [(False, pl.ds(c * j, c), sl) + fwd for sl in pairs for j in range(nsub)]
    probs += [(True, pl.ds(c * j, c), sl) + bwd for sl in pairs for j in reversed(range(nsub))]
    rev = [pr[0] for pr in probs]
    r, v, kk = ([pr[3][j][0, pr[1], pr[2]] for pr in probs] for j in range(3))
    lw, kd, bb = ([pr[4][j][0, 0, pr[1], pr[2]] for pr in probs] for j in range(3))
    phi, psi, ra, oc = _chunk_terms(rev, r, v, kk, lw, kd, bb)
    h = [h_ref[i] for i in range(len(probs) // nsub)]
    for j in range(nsub):
        res = []
        for i in range(len(h)):
            h_hi = h[i].astype(BF16)
            h_lo = (h[i] - h_hi.astype(F32)).astype(BF16)
            lhs = jnp.concatenate([phi[i * nsub + j], ra[i * nsub + j]], axis=0).astype(BF16)
            res.append(_dot(jnp.concatenate([lhs, lhs], axis=1), jnp.concatenate([h_hi, h_lo], axis=0)))
        for i in range(len(h)):
            pr = probs[i * nsub + j]
            h[i] = res[i][:c] + psi[i * nsub + j]
            pr[5][0, pr[1], pr[2]] = res[i][c:] + oc[i * nsub + j]
    for i in range(len(h)):
        h_ref[i] = h[i]


def wkv_scan(r, v, kk, lw, kd, bb, chunks_per_step=2):
    b, s, _ = r.shape
    c = SCAN_CHUNK * chunks_per_step
    nc = s // c
    fwd = pl.BlockSpec((1, c, D_RWKV), lambda bi, ci: (bi, ci, 0))
    bwd = pl.BlockSpec((1, c, D_RWKV), lambda bi, ci: (bi, nc - 1 - ci, 0))
    fwd_d = pl.BlockSpec((1, 1, c, D_RWKV), lambda bi, ci: (0, bi, ci, 0))
    bwd_d = pl.BlockSpec((1, 1, c, D_RWKV), lambda bi, ci: (1, bi, nc - 1 - ci, 0))
    out = jax.ShapeDtypeStruct((b, s, D_RWKV), F32)
    return pl.pallas_call(
        _wkv_kernel,
        grid=(b, nc),
        in_specs=[fwd, fwd, fwd, fwd_d, fwd_d, fwd_d, bwd, bwd, bwd, bwd_d, bwd_d, bwd_d],
        out_specs=[fwd, bwd],
        out_shape=[out, out],
        scratch_shapes=[pltpu.VMEM((4, LANES, LANES), F32)],
        compiler_params=_cparams(("parallel", "arbitrary")),
        name="wkv_scan",
    )(r, v, kk, lw, kd, bb, r, v, kk, lw, kd, bb)


def _attn_kernel(lam_init, q_rows, qa_ref, qb_ref, k_ref, v_ref, lq_ref, lk_ref, sw_ref, o_ref):
    kt = k_ref[0]
    vt = v_ref[0]
    e = jnp.exp(jnp.sum(lq_ref[...] * lk_ref[...], axis=1, keepdims=True))
    lam = e[0:1] - e[1:2] + lam_init
    blocks = [pl.ds(r0, q_rows) for r0 in range(0, qa_ref.shape[1], q_rows)]
    n = len(blocks)
    s, p = [None] * n, [None] * n
    for step in range(n + 2):
        if step < n:
            q2 = jnp.concatenate([qa_ref[0, blocks[step], :], qb_ref[0, blocks[step], :]], axis=0)
            s[step] = lax.dot_general(q2, kt, (((1,), (1,)), ((), ())), preferred_element_type=F32)
        if 0 <= step - 1 < n:
            p[step - 1] = jnp.exp2(s[step - 1] - jnp.max(s[step - 1], axis=1, keepdims=True)).astype(BF16)
            s[step - 1] = None
        if 0 <= step - 2 < n:
            pv = _dot(p[step - 2], vt)
            ratio = pv[:, :DIFF_V] / pv[:, DIFF_V:]
            o = ratio[:q_rows] - lam * ratio[q_rows:]
            o = o * lax.rsqrt(jnp.mean(o * o, axis=1, keepdims=True) + SUBLN_EPS) * sw_ref[...]
            o_ref[0, blocks[step - 2], :] = o * (1.0 - lam_init)
            p[step - 2] = None


def diff_attention(qa, qb, k, v, lq, lk, sw, lam_init, tq=1024, q_rows=128):
    b, s, _ = qa.shape
    tq = min(tq, s)
    qspec = pl.BlockSpec((1, tq, LANES), lambda bi, h, i: (bi, i, h))
    kspec = pl.BlockSpec((1, s, LANES), lambda bi, h, i: (bi, 0, h))
    small = lambda shape: pl.BlockSpec(shape, lambda bi, h, i: (0, 0))
    return pl.pallas_call(
        functools.partial(_attn_kernel, lam_init, min(q_rows, tq)),
        grid=(b, DIFF_HEADS, s // tq),
        in_specs=[qspec, qspec, kspec, pl.BlockSpec((1, s, 2 * DIFF_V), lambda bi, h, i: (bi, 0, h)),
                  small((2, DIFF_QK)), small((2, DIFF_QK)), small((1, DIFF_V))],
        out_specs=qspec,
        out_shape=jax.ShapeDtypeStruct((b, s, D_DIFF), F32),
        compiler_params=_cparams(("parallel", "parallel", "parallel")),
        name="diff_attention",
    )(qa, qb, k, v, lq, lk, sw.reshape(1, DIFF_V))


def _pool_kernel(seq, u_ref, up_ref, un_ref, mix_ref, scale_ref, o_ref):
    i = pl.program_id(1)
    n = pl.num_programs(1)
    u = u_ref[0]
    ts = u.shape[0]
    prev = jnp.where(i > 0, up_ref[0], 0.0)
    nxt = jnp.where(i < n - 1, un_ref[0], 0.0)
    ext = jnp.concatenate([prev, u, nxt], axis=0)
    rows = ext.shape[0]
    back = lambda t, sft: pltpu.roll(t, sft, 0)
    fwd = lambda t, sft: pltpu.roll(t, rows - sft, 0)
    w2 = ext + back(ext, 1)
    w4 = back(w2, 1) + fwd(w2, 1)
    w8 = back(w4, 2) + fwd(w4, 2)
    w16 = back(w8, 4) + fwd(w8, 4)
    lane = lax.broadcasted_iota(jnp.int32, (ts, D_POOL), 1)
    t = lax.broadcasted_iota(jnp.int32, (ts, D_POOL), 0) + i * ts
    grp = lane // HEAD_DIM
    sel = lambda vals: jnp.where(grp == 0, vals[0], jnp.where(grp == 1, vals[1], jnp.where(grp == 2, vals[2], vals[3])))
    wsum = sel([w[HALO:HALO + ts] for w in (w2, w4, w8, w16)])
    half = sel([wd // 2 for wd in POOL_WINDOWS])
    width = sel(list(POOL_WINDOWS))
    cnt = jnp.minimum(t + width - half, seq) - jnp.maximum(t - half, 0)
    pooled = wsum / cnt.astype(F32) - u
    o_ref[0] = _dot(pooled, mix_ref[...], precision=HI) * scale_ref[...]


def pool_mixer(proj, mix_bd, scale, ts=512):
    b, s, _ = proj.shape
    nblk = ts // HALO
    last = s // HALO - 1
    cb = REST_POOL // D_POOL
    hb = REST_POOL // D_POOL
    return pl.pallas_call(
        functools.partial(_pool_kernel, s),
        grid=(b, s // ts),
        in_specs=[
            pl.BlockSpec((1, ts, D_POOL), lambda bi, i: (bi, i, cb)),
            pl.BlockSpec((1, HALO, D_POOL), lambda bi, i: (bi, jnp.maximum(i * nblk - 1, 0), hb)),
            pl.BlockSpec((1, HALO, D_POOL), lambda bi, i: (bi, jnp.minimum((i + 1) * nblk, last), hb)),
            pl.BlockSpec((D_POOL, D_POOL), lambda bi, i: (0, 0)),
            pl.BlockSpec((1, D_POOL), lambda bi, i: (0, 0)),
        ],
        out_specs=pl.BlockSpec((1, ts, D_POOL), lambda bi, i: (bi, i, 0)),
        out_shape=jax.ShapeDtypeStruct((b, s, D_POOL), F32),
        compiler_params=_cparams(("parallel", "parallel")),
        name="pool_mixer",
    )(proj, proj, proj, mix_bd, scale.reshape(1, D_POOL))


def _out_proj_kernel(x_ref, yf_ref, yb_ref, bonus_ref, g_ref, att_ref, pool_ref, lw_ref, lb_ref, bd_ref, w_ref, o_ref):
    y = yf_ref[...] + yb_ref[...]
    bd = bd_ref[...]
    mu = _dot_exact_rhs(y, bd) * (1.0 / HEAD_DIM)
    yc = y - mu
    var = _dot_exact_rhs(yc * yc, bd) * (1.0 / HEAD_DIM)
    ya = (yc * lax.rsqrt(var + GN_EPS) * lw_ref[...] + lb_ref[...] + bonus_ref[...]) * g_ref[...]
    cat = jnp.concatenate([ya, att_ref[...], pool_ref[...]], axis=1).astype(BF16)
    o_ref[...] = x_ref[...] + _dot(cat, w_ref[...])


def out_proj(x2d, yf, yb, bonus, g, att, pool, lnx_w, lnx_b, bd, w_bf16, tm=512):
    t, d = x2d.shape
    rows = lambda n: pl.BlockSpec((tm, n), lambda i: (i, 0))
    fixed = lambda shape: pl.BlockSpec(shape, lambda i: (0, 0))
    return pl.pallas_call(
        _out_proj_kernel,
        grid=(t // tm,),
        in_specs=[rows(d),
                  rows(D_RWKV), rows(D_RWKV), rows(D_RWKV), rows(D_RWKV), rows(D_DIFF), rows(D_POOL),
                  fixed((1, D_RWKV)), fixed((1, D_RWKV)), fixed((D_RWKV, D_RWKV)), fixed((d, d))],
        out_specs=rows(d),
        out_shape=jax.ShapeDtypeStruct((t, d), F32),
        compiler_params=_cparams(("parallel",)),
        name="out_proj",
    )(x2d, yf, yb, bonus, g, att, pool, lnx_w.reshape(1, -1), lnx_b.reshape(1, -1), bd, w_bf16)


FF_TILE = 512
MOE_FF_TILE = 1792
TOP_K = 2


def _swiglu_tile(x, wg, wu, wd):
    gate = _dot(x, wg)
    return _dot((gate * _sigmoid(gate) * _dot(x, wu)).astype(BF16), wd)


def _ffn_kernel(x_ref, g_ref, wg_ref, wu_ref, wd_ref, o_ref, h_ref, acc_ref):
    f = pl.program_id(1)

    @pl.when(f == 0)
    def _():
        h_ref[...] = _rms(x_ref[...], g_ref[...]).astype(BF16)
        acc_ref[...] = jnp.zeros_like(acc_ref)

    acc_ref[...] += _swiglu_tile(h_ref[...], wg_ref[...], wu_ref[...], wd_ref[...])

    @pl.when(f == pl.num_programs(1) - 1)
    def _():
        o_ref[...] = x_ref[...] + acc_ref[...]


def ffn_dense(x2d, g, wg, wu, wd, tm=1024):
    t, d = x2d.shape
    ff = wg.shape[1]
    return pl.pallas_call(
        _ffn_kernel,
        grid=(t // tm, ff // FF_TILE),
        in_specs=[pl.BlockSpec((tm, d), lambda i, f: (i, 0)),
                  pl.BlockSpec((1, d), lambda i, f: (0, 0)),
                  pl.BlockSpec((d, FF_TILE), lambda i, f: (0, f)),
                  pl.BlockSpec((d, FF_TILE), lambda i, f: (0, f)),
                  pl.BlockSpec((FF_TILE, d), lambda i, f: (f, 0))],
        out_specs=pl.BlockSpec((tm, d), lambda i, f: (i, 0)),
        out_shape=jax.ShapeDtypeStruct((t, d), F32),
        scratch_shapes=[pltpu.VMEM((tm, d), BF16), pltpu.VMEM((tm, d), F32)],
        compiler_params=_cparams(("parallel", "arbitrary")),
        name="ffn_dense",
    )(x2d, g.reshape(1, d), wg, wu, wd)


def _router_kernel(x_ref, g_ref, wr_ref, h_ref, comb_ref, rank_ref, cnt_ref):
    h = _rms(x_ref[...], g_ref[...])
    h_ref[...] = h.astype(BF16)
    logits = _dot(h, wr_ref[...], precision=HI)
    lane = lax.broadcasted_iota(jnp.int32, logits.shape, 1)
    logits = jnp.where(lane < N_EXPERTS, logits, -jnp.inf)
    v1 = jnp.max(logits, axis=1, keepdims=True)
    i1 = jnp.min(jnp.where(logits == v1, lane, LANES), axis=1, keepdims=True)
    rest = jnp.where(lane == i1, -jnp.inf, logits)
    v2 = jnp.max(rest, axis=1, keepdims=True)
    i2 = jnp.min(jnp.where(rest == v2, lane, LANES), axis=1, keepdims=True)
    e2 = jnp.exp(v2 - v1)
    g1 = 1.0 / (1.0 + e2)
    comb = jnp.where(lane == i1, g1, jnp.where(lane == i2, e2 * g1, 0.0))
    comb_ref[...] = comb
    tm = comb.shape[0]
    routed = comb > 0.0
    mask = jnp.where(routed, 1.0, 0.0).astype(BF16)
    earlier = lax.broadcasted_iota(jnp.int32, (tm, tm), 1) < lax.broadcasted_iota(jnp.int32, (tm, tm), 0)
    rank = _dot(jnp.where(earlier, 1.0, 0.0).astype(BF16), mask)
    rank_ref[...] = jnp.where(routed, rank, -1.0)
    cnt_ref[0] = _dot(jnp.ones((HALO, tm), BF16), mask)


def moe_router(x2d, g, wr_pad, tm):
    t, d = x2d.shape
    tok = pl.BlockSpec((tm, LANES), lambda i: (i, 0))
    return pl.pallas_call(
        _router_kernel,
        grid=(t // tm,),
        in_specs=[pl.BlockSpec((tm, d), lambda i: (i, 0)),
                  pl.BlockSpec((1, d), lambda i: (0, 0)),
                  pl.BlockSpec((d, LANES), lambda i: (0, 0))],
        out_specs=[pl.BlockSpec((tm, d), lambda i: (i, 0)), tok, tok,
                   pl.BlockSpec((1, HALO, LANES), lambda i: (i, 0, 0))],
        out_shape=[jax.ShapeDtypeStruct((t, d), BF16), jax.ShapeDtypeStruct((t, LANES), F32),
                   jax.ShapeDtypeStruct((t, LANES), F32), jax.ShapeDtypeStruct((t // tm, HALO, LANES), F32)],
        compiler_params=_cparams(("parallel",)),
        name="moe_router",
    )(x2d, g.reshape(1, d), wr_pad)


def _moe_kernel(item_ref, h_ref, comb_ref, rank_ref, rank_t_ref, wg_ref, wu_ref, wd_ref, o_ref,
                xe_ref, ye_ref, gc_ref):
    i, k, f = pl.program_id(0), pl.program_id(1), pl.program_id(2)
    tm = h_ref.shape[0]
    cap = xe_ref.shape[0]
    e, j, on = item_ref[0, i, k], item_ref[1, i, k], item_ref[2, i, k] > 0
    first, last = f == 0, f == pl.num_programs(2) - 1

    @pl.when(jnp.logical_and(k == 0, first))
    def _():
        o_ref[...] = jnp.zeros_like(o_ref)

    def lane_column(x):
        onehot = jnp.where(lax.broadcasted_iota(jnp.int32, (LANES, LANES), 0) == e, 1.0, 0.0).astype(BF16)
        hi = x.astype(BF16)
        lo = (x - hi.astype(F32)).astype(BF16)
        both = _dot(jnp.concatenate([hi, lo], axis=0), onehot)
        return both[:x.shape[0]] + both[x.shape[0]:]

    @pl.when(jnp.logical_and(on, first))
    def _():
        gate = lane_column(comb_ref[...])
        g_hi = gate.astype(BF16)
        g_lo = (gate - g_hi.astype(F32)).astype(BF16)
        rid = lax.broadcasted_iota(jnp.int32, (cap, tm), 0) + j * cap
        pick = jnp.where(rank_t_ref[0] == rid.astype(F32), 1.0, 0.0).astype(BF16)
        xe_ref[...] = _dot(pick, h_ref[...]).astype(BF16)
        gc2 = _dot(pick, jnp.concatenate([g_hi, g_lo], axis=1))
        gc_ref[...] = gc2[:, :LANES] + gc2[:, LANES:]

    @pl.when(on)
    def _():
        y = _swiglu_tile(xe_ref[...], wg_ref[0], wu_ref[0], wd_ref[0])
        ye_ref[...] = jnp.where(first, y, ye_ref[...] + y)

    @pl.when(jnp.logical_and(on, last))
    def _():
        slot = jnp.concatenate([lane_column(rank_ref[...])] * (-(-cap // LANES)), axis=1)[:, :cap]
        cid = lax.broadcasted_iota(jnp.int32, (tm, cap), 1) + j * cap
        place = jnp.where(slot == cid.astype(F32), 1.0, 0.0).astype(BF16)
        ys = ye_ref[...] * jnp.concatenate([gc_ref[...]] * (ye_ref.shape[1] // LANES), axis=1)
        o_ref[...] += _dot(place, ys.astype(BF16))


def _moe_items(cnt, cap, n_items):
    nblk = (cnt + cap - 1) // cap
    ends = jnp.cumsum(nblk, axis=1)
    k = jnp.arange(n_items, dtype=jnp.int32)
    total = ends[:, -1:]
    kk = jnp.minimum(k[None, :], total - 1)
    e = jnp.sum((kk[:, :, None] >= ends[:, None, :]).astype(jnp.int32), axis=2)
    start = jnp.take_along_axis(ends - nblk, e, axis=1)
    return jnp.stack([e, kk - start, (k[None, :] < total).astype(jnp.int32)]).astype(jnp.int32)


def moe_ffn(h, comb, rank, rank_t, cnt, wg, wu, wd, tm, cap):
    t, d = h.shape
    ne, _, ff = wg.shape
    tf = MOE_FF_TILE
    nf = ff // tf
    n_items = ne + (TOP_K * tm) // cap
    items = _moe_items(cnt, cap, n_items)
    tok = lambda n: pl.BlockSpec((tm, n), lambda i, k, f, it: (i, 0), pipeline_mode=pl.Buffered(1))
    fidx = lambda k, f, it, i: jnp.where(it[2, i, k] > 0, f, nf - 1)
    grid_spec = pltpu.PrefetchScalarGridSpec(
        num_scalar_prefetch=1,
        grid=(t // tm, n_items, nf),
        in_specs=[tok(d), tok(LANES), tok(LANES),
                  pl.BlockSpec((1, 1, tm), lambda i, k, f, it: (it[0, i, k], 0, i)),
                  pl.BlockSpec((1, d, tf), lambda i, k, f, it: (it[0, i, k], 0, fidx(k, f, it, i))),
                  pl.BlockSpec((1, d, tf), lambda i, k, f, it: (it[0, i, k], 0, fidx(k, f, it, i))),
                  pl.BlockSpec((1, tf, d), lambda i, k, f, it: (it[0, i, k], fidx(k, f, it, i), 0))],
        out_specs=tok(d),
        scratch_shapes=[pltpu.VMEM((cap, d), BF16), pltpu.VMEM((cap, d), F32), pltpu.VMEM((cap, LANES), F32)],
    )
    return pl.pallas_call(
        _moe_kernel,
        grid_spec=grid_spec,
        out_shape=jax.ShapeDtypeStruct((t, d), F32),
        compiler_params=_cparams(("parallel", "arbitrary", "arbitrary")),
        name="moe_ffn",
    )(items, h, comb, rank, rank_t, wg, wu, wd)


def _add_norm_kernel(x_ref, y_ref, g_ref, o_ref):
    o_ref[...] = _rms(x_ref[...] + y_ref[...], g_ref[...])


def add_norm(x2d, y2d, g, tm=1024):
    t, d = x2d.shape
    row = pl.BlockSpec((tm, d), lambda i: (i, 0))
    return pl.pallas_call(
        _add_norm_kernel,
        grid=(t // tm,),
        in_specs=[row, row, pl.BlockSpec((1, d), lambda i: (0, 0))],
        out_specs=row,
        out_shape=jax.ShapeDtypeStruct((t, d), F32),
        compiler_params=_cparams(("parallel",)),
        name="add_norm",
    )(x2d, y2d, g.reshape(1, d))


def _block_diag_ones(n, blk):
    i = jnp.arange(n) // blk
    return (i[:, None] == i[None, :]).astype(F32)


def _rope_tables(positions):
    half = ROT_DIM // 2
    inv_freq = jnp.power(ROPE_THETA, -(jnp.arange(half, dtype=F32) * 2.0 / ROT_DIM))
    ang = positions.astype(F32)[..., None] * inv_freq
    cos, sin = jnp.cos(ang), jnp.sin(ang)
    one = jnp.ones(cos.shape[:-1] + (DIFF_QK - ROT_DIM,), F32)
    zero = jnp.zeros_like(one)
    z8 = jnp.zeros_like(sin)
    cs = jnp.concatenate([cos, cos, one], axis=-1)
    s1 = jnp.concatenate([z8, sin, zero], axis=-1)
    s2 = jnp.concatenate([-sin, z8, zero], axis=-1)
    return tuple(jnp.concatenate([t, t], axis=-1) for t in (cs, s1, s2))


def kernel(x, positions, norm_mix, w_in_first, w_in_rest, tshift, decay_bias, decay_up, iclr_bias, iclr_up, gate_up, k_k, k_a, r_k, lnx_w, lnx_b, vres_bias, vres_up, lambda_q, lambda_k, subln_w, pool_mix, pool_scale, w_out, norm_ffn, ffn_gate, ffn_up, ffn_down, router, exp_gate, exp_up, exp_down, norm_out):
    bsz, seq, d = x.shape
    depth = norm_mix.shape[0]
    x2d = x.reshape(bsz * seq, d)
    bd = _block_diag_ones(D_RWKV, HEAD_DIM)
    cs, s1, s2 = (t.reshape(bsz * seq, LANES) for t in _rope_tables(positions))
    v_first = None
    for l in range(depth):
        if l == 0:
            w_in = w_in_first
        else:
            wr = w_in_rest[l - 1]
            w_in = jnp.concatenate([wr[:, :D_SHIFT], wr[:, D_SHIFT + MV_LORA:], wr[:, D_SHIFT:D_SHIFT + MV_LORA],
                                    jnp.zeros((d, LANES - MV_LORA), F32)], axis=1)
        proj, qa, qb, kr, vb = in_proj(x2d, norm_mix[l], w_in.astype(BF16), cs, s1, s2)
        proj, qa, qb, kr, vb = (t.reshape(bsz, seq, -1) for t in (proj, qa, qb, kr, vb))

        zeros = jnp.zeros((DECAY_LORA, 2 * D_RWKV), F32)
        dec = jnp.concatenate([decay_up[l, 0], decay_up[l, 1]], axis=1)
        icl = jnp.concatenate([iclr_up[l, 0], iclr_up[l, 1]], axis=1)
        w_lora = jnp.concatenate([
            jnp.concatenate([dec, zeros], axis=1), jnp.zeros((AAA_LORA, 4 * D_RWKV), F32),
            jnp.zeros((DECAY_LORA, 4 * D_RWKV), F32), jnp.concatenate([zeros, icl], axis=1)], axis=0)
        prm = dict(tshift=tshift[l], w_lora=w_lora,
                   b_lora=jnp.concatenate([decay_bias[l, 0], decay_bias[l, 1], iclr_bias[l, 0], iclr_bias[l, 1]]).reshape(1, -1),
                   gate_up=gate_up[l], k_k=k_k[l].reshape(1, -1), k_a=k_a[l].reshape(1, -1),
                   r_k=r_k[l].reshape(1, -1), bd=bd)
        if l > 0:
            prm["vres_bias"] = vres_bias[l - 1].reshape(1, -1)
            prm["vres_up"] = jnp.concatenate([vres_up[l - 1], jnp.zeros((LANES - MV_LORA, D_RWKV), F32)], axis=0)
        r, v, kk, lw, kd, bb, g, bonus = rwkv_prep(proj, v_first, prm)
        if l == 0:
            v_first = v
        yf, yb = wkv_scan(r, v, kk, lw, kd, bb)

        lam_init = 0.8 - 0.6 * math.exp(-0.3 * l)
        att = diff_attention(qa, qb, kr, vb, lambda_q[l], lambda_k[l], subln_w[l], lam_init)

        mix_bd = jax.scipy.linalg.block_diag(*[pool_mix[l, gi] for gi in range(len(POOL_WINDOWS))])
        pool = pool_mixer(proj, mix_bd, pool_scale[l])

        flat = lambda t: t.reshape(bsz * seq, -1)
        x2d = out_proj(x2d, flat(yf), flat(yb), flat(bonus), flat(g), flat(att), flat(pool), lnx_w[l], lnx_b[l], bd,
                       w_out[l].astype(BF16))
        i = l // 2
        if l % 2 == 0:
            x2d = ffn_dense(x2d, norm_ffn[l], ffn_gate[i].astype(BF16), ffn_up[i].astype(BF16), ffn_down[i].astype(BF16))
        else:
            wr_pad = jnp.concatenate([router[i], jnp.zeros((d, LANES - N_EXPERTS), F32)], axis=1)
            tm = min(2048, bsz * seq)
            cap = 9 * tm // 32
            h, comb, rank, cnt = moe_router(x2d, norm_ffn[l], wr_pad, tm)
            rank_t = rank[:, :N_EXPERTS].T.reshape(N_EXPERTS, 1, -1)
            cnt = cnt[:, 0, :N_EXPERTS].astype(jnp.int32)
            y = moe_ffn(h, comb, rank, rank_t, cnt, exp_gate[i].astype(BF16), exp_up[i].astype(BF16),
                        exp_down[i].astype(BF16), tm, cap)
            assert l == depth - 1, "the routed mixer's residual add is fused with the final norm"
            x2d = add_norm(x2d, y, norm_out)
    return x2d.reshape(bsz, seq, d)
```

```python
import functools
import math

import jax
import jax.numpy as jnp
from jax import lax
from jax.experimental import pallas as pl
from jax.experimental.pallas import tpu as pltpu

F32 = jnp.float32
BF16 = jnp.bfloat16
HI = lax.Precision.HIGHEST

D_MODEL = 1024
RWKV_HEADS = 4
HEAD_DIM = 64
D_RWKV = RWKV_HEADS * HEAD_DIM
DECAY_LORA = 64
AAA_LORA = 64
MV_LORA = 32
GATE_LORA = 128
D_SHIFT = 3 * D_RWKV + DECAY_LORA + AAA_LORA + GATE_LORA
W_DECAY_SCALE = 0.6065306597126334
GN_EPS = 64e-5
DIFF_HEADS = 4
DIFF_QK = 64
DIFF_V = 128
D_DIFF_QK = DIFF_HEADS * 2 * DIFF_QK
D_DIFF = DIFF_HEADS * DIFF_V
ROT_DIM = DIFF_QK // 4
ROPE_THETA = 500000.0
SUBLN_EPS = 1e-5
POOL_WINDOWS = (2, 4, 8, 16)
D_POOL = 256
D_FF = 3584
N_EXPERTS = 8
NORM_EPS = 1e-6
P_IN = D_SHIFT + 2 * D_DIFF_QK + D_DIFF + D_POOL
COL_Q, COL_K, COL_V, COL_POOL = 1024, 1536, 2048, 2560
REST_POOL, REST_VDOWN = D_SHIFT, D_SHIFT + D_POOL

LANES = 128
HALO = 8
VMEM_LIMIT = 56 * 1024 * 1024

SCAN_CHUNK = 128


def _cparams(sem):
    return pltpu.CompilerParams(dimension_semantics=sem, vmem_limit_bytes=VMEM_LIMIT)


def _sigmoid(x):
    return 1.0 / (1.0 + jnp.exp(-x))


def _dot(a, b, **kw):
    return jnp.dot(a, b, preferred_element_type=F32, **kw)


def _bdot(a, b):
    return jnp.dot(a.astype(BF16), b.astype(BF16), preferred_element_type=F32)


def _rms(x, g):
    return x * lax.rsqrt(jnp.mean(x * x, axis=-1, keepdims=True) + NORM_EPS) * g


def _hi_lo(x):
    hi = x.astype(BF16)
    return hi, (x - hi.astype(F32)).astype(BF16)


def _dot_exact_rhs(a, w):
    hi, lo = _hi_lo(a)
    both = _dot(jnp.concatenate([hi, lo], axis=0), w.astype(BF16))
    return both[:a.shape[0]] + both[a.shape[0]:]


def _dot3(a, w):
    a_hi, a_lo = _hi_lo(a)
    w_hi, w_lo = _hi_lo(w)
    both = _dot(jnp.concatenate([a_hi, a_lo], axis=0), w_hi)
    return both[:a.shape[0]] + both[a.shape[0]:] + _dot(a_hi, w_lo)


def _in_proj_kernel(x_ref, g_ref, w_ref, c_ref, s1_ref, s2_ref, rest_ref, qa_ref, qb_ref, ko_ref, vo_ref):
    h = _rms(x_ref[...], g_ref[...]).astype(BF16)
    p = _dot(h, w_ref[...])
    rest_ref[:, :D_SHIFT] = p[:, :D_SHIFT]
    rest_ref[:, D_SHIFT:] = p[:, COL_POOL:]
    cs, s1, s2 = c_ref[...], s1_ref[...], s2_ref[...]
    first = lax.broadcasted_iota(jnp.int32, cs.shape, 1) < DIFF_QK
    scale = DIFF_QK ** -0.5 * math.log2(math.e)

    def rope(t):
        return t * cs + pltpu.roll(t, ROT_DIM // 2, 1) * s1 + pltpu.roll(t, LANES - ROT_DIM // 2, 1) * s2

    for j in range(D_DIFF_QK // LANES):
        sl = slice(LANES * j, LANES * j + LANES)
        q = rope(p[:, COL_Q + LANES * j:COL_Q + LANES * (j + 1)]) * scale
        qa_ref[:, sl] = jnp.where(first, q, 0.0).astype(BF16)
        qb_ref[:, sl] = jnp.where(first, 0.0, q).astype(BF16)
        ko_ref[:, sl] = rope(p[:, COL_K + LANES * j:COL_K + LANES * (j + 1)]).astype(BF16)
    ones = jnp.ones((p.shape[0], DIFF_V), BF16)
    for hd in range(DIFF_HEADS):
        vo_ref[:, 2 * DIFF_V * hd:2 * DIFF_V * hd + DIFF_V] = p[:, COL_V + DIFF_V * hd:COL_V + DIFF_V * (hd + 1)].astype(BF16)
        vo_ref[:, 2 * DIFF_V * hd + DIFF_V:2 * DIFF_V * (hd + 1)] = ones


def in_proj(x2d, g, w_bf16, cs, s1, s2, tm=512):
    t, d = x2d.shape
    n = w_bf16.shape[1]
    n_rest = n - (COL_POOL - D_SHIFT)
    rows = lambda width: pl.BlockSpec((tm, width), lambda i: (i, 0))
    fixed = lambda shape: pl.BlockSpec(shape, lambda i: (0, 0))
    qkv = jax.ShapeDtypeStruct((t, D_DIFF_QK), BF16)
    return pl.pallas_call(
        _in_proj_kernel,
        grid=(t // tm,),
        in_specs=[rows(d), fixed((1, d)), fixed((d, n)), rows(LANES), rows(LANES), rows(LANES)],
        out_specs=[rows(n_rest), rows(D_DIFF_QK), rows(D_DIFF_QK), rows(D_DIFF_QK), rows(2 * D_DIFF)],
        out_shape=[jax.ShapeDtypeStruct((t, n_rest), F32), qkv, qkv, qkv, jax.ShapeDtypeStruct((t, 2 * D_DIFF), BF16)],
        compiler_params=_cparams(("parallel",)),
        name="in_proj",
    )(x2d, g.reshape(1, d), w_bf16, cs, s1, s2)


def _shifted(u, prev_row, next_row):
    ts = u.shape[0]
    row = lax.broadcasted_iota(jnp.int32, u.shape, 0)
    prev = jnp.where(row == 0, prev_row, pltpu.roll(u, 1, 0))
    nxt = jnp.where(row == ts - 1, next_row, pltpu.roll(u, ts - 1, 0))
    return prev, nxt


def _rwkv_prep_kernel(has_vres, *refs):
    if has_vres:
        (u_ref, up_ref, un_ref, vd_ref, vf_ref, vb_ref, vu_ref, mu_ref, wl_ref, bias_ref, gu_ref,
         kkw_ref, ka_ref, rk_ref, bd_ref,
         r_ref, v_ref, kk_ref, lw_ref, kd_ref, bb_ref, g_ref, bonus_ref) = refs
    else:
        (u_ref, up_ref, un_ref, mu_ref, wl_ref, bias_ref, gu_ref,
         kkw_ref, ka_ref, rk_ref, bd_ref,
         r_ref, v_ref, kk_ref, lw_ref, kd_ref, bb_ref, g_ref, bonus_ref) = refs
    i = pl.program_id(1)
    n = pl.num_programs(1)
    u = u_ref[0]
    prev_row = jnp.where(i > 0, up_ref[0, HALO - 1:HALO, :], 0.0)
    next_row = jnp.where(i < n - 1, un_ref[0, 0:1, :], 0.0)
    prev, nxt = _shifted(u, prev_row, next_row)
    us = u + mu_ref[0:1, :] * (prev - u) + mu_ref[1:2, :] * (nxt - u)
    r = us[:, 0:256]
    k = us[:, 256:512]
    v = us[:, 512:768]
    xwa = us[:, 768:896]
    xg = us[:, 896:1024]
    lhs = jnp.concatenate([jnp.tanh(xwa), xwa], axis=1)
    z = _dot3(lhs, wl_ref[...]) + bias_ref[...]
    if has_vres:
        mix = _sigmoid(vb_ref[...] + _dot3(vd_ref[0], vu_ref[...]))
        v = v + (vf_ref[0] - v) * mix
    g = _dot3(_sigmoid(xg), gu_ref[...])
    bd = bd_ref[...]
    kkr = k * kkw_ref[...]
    ss = _dot_exact_rhs(kkr * kkr, bd)
    kk = kkr * lax.rsqrt(jnp.maximum(ss, 1e-24))
    r_ref[0] = r
    v_ref[0] = v
    kk_ref[0] = kk
    g_ref[0] = g
    bonus = jnp.zeros_like(v)
    for d in range(2):
        lw = -W_DECAY_SCALE * _sigmoid(z[:, 256 * d:256 * d + 256])
        a = _sigmoid(z[:, 512 + 256 * d:768 + 256 * d])
        kd = k * (1.0 + (a - 1.0) * ka_ref[...])
        lw_ref[d, 0] = lw
        kd_ref[d, 0] = kd
        bb_ref[d, 0] = kk * a
        bonus = bonus + _dot_exact_rhs(r * kd * rk_ref[...], bd) * v
    bonus_ref[0] = bonus


def rwkv_prep(proj, v_first, p, ts=512):
    b, s, _ = proj.shape
    has_vres = v_first is not None
    nblk = ts // HALO
    last = s // HALO - 1
    row = lambda shape: pl.BlockSpec(shape, lambda bi, i: (0,) * len(shape))
    in_specs = [
        pl.BlockSpec((1, ts, D_SHIFT), lambda bi, i: (bi, i, 0)),
        pl.BlockSpec((1, HALO, D_SHIFT), lambda bi, i: (bi, jnp.maximum(i * nblk - 1, 0), 0)),
        pl.BlockSpec((1, HALO, D_SHIFT), lambda bi, i: (bi, jnp.minimum((i + 1) * nblk, last), 0)),
    ]
    args = [proj, proj, proj]
    if has_vres:
        in_specs += [
            pl.BlockSpec((1, ts, LANES), lambda bi, i: (bi, i, REST_VDOWN // LANES)),
            pl.BlockSpec((1, ts, D_RWKV), lambda bi, i: (bi, i, 0)),
            row((1, D_RWKV)), row((LANES, D_RWKV)),
        ]
        args += [proj, v_first, p["vres_bias"], p["vres_up"]]
    in_specs += [row((2, D_SHIFT)), row((256, 1024)), row((1, 1024)), row((GATE_LORA, D_RWKV)),
                 row((1, D_RWKV)), row((1, D_RWKV)), row((1, D_RWKV)), row((D_RWKV, D_RWKV))]
    args += [p["tshift"], p["w_lora"], p["b_lora"], p["gate_up"], p["k_k"], p["k_a"], p["r_k"], p["bd"]]
    one = pl.BlockSpec((1, ts, D_RWKV), lambda bi, i: (bi, i, 0))
    two = pl.BlockSpec((2, 1, ts, D_RWKV), lambda bi, i: (0, bi, i, 0))
    s1 = jax.ShapeDtypeStruct((b, s, D_RWKV), F32)
    s2 = jax.ShapeDtypeStruct((2, b, s, D_RWKV), F32)
    return pl.pallas_call(
        functools.partial(_rwkv_prep_kernel, has_vres),
        grid=(b, s // ts),
        in_specs=in_specs,
        out_specs=[one, one, one, two, two, two, one, one],
        out_shape=[s1, s1, s1, s2, s2, s2, s1, s1],
        compiler_params=_cparams(("parallel", "parallel")),
        name="rwkv_prep",
    )(*args)


def _unit_lower_inverses(ms, eye):
    xs = [eye - m for m in ms]
    ps = [_bdot(m, m) for m in ms]
    levels = int(math.log2(ms[0].shape[0])) - 1
    for k in range(levels):
        nxt = [_bdot(p, p) for p in ps] if k < levels - 1 else None
        xs = [x + _bdot(x, p) for x, p in zip(xs, ps)]
        ps = nxt
    return xs


def _split3(x):
    hi = x.astype(BF16)
    r1 = x - hi.astype(F32)
    mid = r1.astype(BF16)
    lo = (r1 - mid.astype(F32)).astype(BF16)
    return hi, mid, lo


def _chunk_terms(rev, r, v, kk, lw, kd, bb):
    c = SCAN_CHUNK
    n = len(r)
    ti = lax.broadcasted_iota(jnp.int32, (c, c), 0)
    tj = lax.broadcasted_iota(jnp.int32, (c, c), 1)
    eye = (ti == tj).astype(F32)
    same_head = (ti // HEAD_DIM) == (tj // HEAD_DIM)
    first = lax.broadcasted_iota(jnp.int32, (c, LANES), 1) < HEAD_DIM
    first2 = jnp.concatenate([first, first], axis=1)
    strict = [(ti - tj if rv else tj - ti) < 0 for rv in rev]
    incl = [(ti - tj if rv else tj - ti) <= 0 for rv in rev]

    cum = []
    for i in range(n):
        cum3 = _dot(jnp.where(incl[i], 1.0, 0.0).astype(BF16), jnp.concatenate(_split3(lw[i]), axis=1))
        cum.append(cum3[:, :LANES] + cum3[:, LANES:2 * LANES] + cum3[:, 2 * LANES:])
    a = [kk[i] * jnp.exp(cum[i] - lw[i]) for i in range(n)]
    rt = [r[i] * jnp.exp(cum[i]) for i in range(n)]
    einv = [jnp.exp(-cm) for cm in cum]
    kt_t = [(kd[i] * einv[i]).T for i in range(n)]
    bt_t = [(bb[i] * einv[i]).T for i in range(n)]
    last = [cum[i][0:1] if rev[i] else cum[i][c - 1:c] for i in range(n)]
    g_all = [jnp.exp(jnp.broadcast_to(x, (c, LANES))).T for x in last]
    vb = [x.astype(BF16) for x in v]
    rhs = [jnp.concatenate([bt_t[i], kt_t[i]], axis=1).astype(BF16) for i in range(n)]
    heads = [(i, hm) for i in range(n) for hm in (first, jnp.logical_not(first))]
    gm = [_dot(jnp.concatenate([jnp.where(hm, a[i], 0.0), jnp.where(hm, rt[i], 0.0)], axis=0).astype(BF16), rhs[i])
          for i, hm in heads]
    t_inv = _unit_lower_inverses([jnp.where(strict[i], g[:c, :c], 0.0) for (i, _), g in zip(heads, gm)], eye)
    mv = [_dot(jnp.where(strict[i], g[:c, c:], 0.0).astype(BF16), vb[i]) for (i, _), g in zip(heads, gm)]
    tam = [_bdot(t, jnp.concatenate([a[i], m], axis=1)) for (i, _), t, m in zip(heads, t_inv, mv)]
    lt = [_bdot(jnp.where(incl[i], g[c:, :c], 0.0), x) for (i, _), g, x in zip(heads, gm, tam)]
    lv = [_dot(jnp.where(incl[i], g[c:, c:], 0.0).astype(BF16), vb[i]) for (i, _), g in zip(heads, gm)]
    tam = [jnp.where(first2, tam[2 * i], tam[2 * i + 1]) for i in range(n)]
    lt = [jnp.where(first2, lt[2 * i], lt[2 * i + 1]) for i in range(n)]
    lv = [jnp.where(first, lv[2 * i], lv[2 * i + 1]) for i in range(n)]
    btx = [_bdot(bt_t[i], tam[i]) for i in range(n)]
    kv = [_dot(kt_t[i].astype(BF16), vb[i]) for i in range(n)]
    phi = [g_all[i] * (eye - jnp.where(same_head, btx[i][:, :LANES], 0.0)) for i in range(n)]
    psi = [g_all[i] * jnp.where(same_head, kv[i] - btx[i][:, LANES:], 0.0) for i in range(n)]
    ra = [rt[i] - lt[i][:, :LANES] for i in range(n)]
    oc = [lv[i] - lt[i][:, LANES:] for i in range(n)]
    return phi, psi, ra, oc


def _wkv_kernel(rf_ref, vf_ref, kkf_ref, lwf_ref, kdf_ref, bbf_ref,
                rb_ref, vb_ref, kkb_ref, lwb_ref, kdb_ref, bbb_ref, yf_ref, yb_ref, h_ref):
    @pl.when(pl.program_id(1) == 0)
    def _():
        h_ref[...] = jnp.zeros_like(h_ref)

    c = SCAN_CHUNK
    nsub = rf_ref.shape[1] // c
    pairs = (slice(0, LANES), slice(LANES, 2 * LANES))
    fwd = ((rf_ref, vf_ref, kkf_ref), (lwf_ref, kdf_ref, bbf_ref), yf_ref)
    bwd = ((rb_ref, vb_ref, kkb_ref), (lwb_ref, kdb_ref, bbb_ref), yb_ref)
    probs = [(False, pl.ds(c * j, c), sl) + fwd for sl in pairs for j in range(nsub)]
    probs += [(True, pl.ds(c * j, c), sl) + bwd for sl in pairs for j in reversed(range(nsub))]
    rev = [pr[0] for pr in probs]
    r, v, kk = ([pr[3][j][0, pr[1], pr[2]] for pr in probs] for j in range(3))
    lw, kd, bb = ([pr[4][j][0, 0, pr[1], pr[2]] for pr in probs] for j in range(3))
    phi, psi, ra, oc = _chunk_terms(rev, r, v, kk, lw, kd, bb)
    h = [h_ref[i] for i in range(len(probs) // nsub)]
    for j in range(nsub):
        res = []
        for i in range(len(h)):
            h_hi = h[i].astype(BF16)
            h_lo = (h[i] - h_hi.astype(F32)).astype(BF16)
            lhs = jnp.concatenate([phi[i * nsub + j], ra[i * nsub + j]], axis=0).astype(BF16)
            res.append(_dot(jnp.concatenate([lhs, lhs], axis=1), jnp.concatenate([h_hi, h_lo], axis=0)))
        for i in range(len(h)):
            pr = probs[i * nsub + j]
            h[i] = res[i][:c] + psi[i * nsub + j]
            pr[5][0, pr[1], pr[2]] = res[i][c:] + oc[i * nsub + j]
    for i in range(len(h)):
        h_ref[i] = h[i]


def wkv_scan(r, v, kk, lw, kd, bb, chunks_per_step=2):
    b, s, _ = r.shape
    c = SCAN_CHUNK * chunks_per_step
    nc = s // c
    fwd = pl.BlockSpec((1, c, D_RWKV), lambda bi, ci: (bi, ci, 0))
    bwd = pl.BlockSpec((1, c, D_RWKV), lambda bi, ci: (bi, nc - 1 - ci, 0))
    fwd_d = pl.BlockSpec((1, 1, c, D_RWKV), lambda bi, ci: (0, bi, ci, 0))
    bwd_d = pl.BlockSpec((1, 1, c, D_RWKV), lambda bi, ci: (1, bi, nc - 1 - ci, 0))
    out = jax.ShapeDtypeStruct((b, s, D_RWKV), F32)
    return pl.pallas_call(
        _wkv_kernel,
        grid=(b, nc),
        in_specs=[fwd, fwd, fwd, fwd_d, fwd_d, fwd_d, bwd, bwd, bwd, bwd_d, bwd_d, bwd_d],
        out_specs=[fwd, bwd],
        out_shape=[out, out],
        scratch_shapes=[pltpu.VMEM((4, LANES, LANES), F32)],
        compiler_params=_cparams(("parallel", "arbitrary")),
        name="wkv_scan",
    )(r, v, kk, lw, kd, bb, r, v, kk, lw, kd, bb)


def _attn_kernel(lam_init, q_rows, qa_ref, qb_ref, k_ref, v_ref, lq_ref, lk_ref, sw_ref, o_ref):
    kt = k_ref[0]
    vt = v_ref[0]
    e = jnp.exp(jnp.sum(lq_ref[...] * lk_ref[...], axis=1, keepdims=True))
    lam = e[0:1] - e[1:2] + lam_init
    blocks = [pl.ds(r0, q_rows) for r0 in range(0, qa_ref.shape[1], q_rows)]
    n = len(blocks)
    s, p = [None] * n, [None] * n
    for step in range(n + 2):
        if step < n:
            q2 = jnp.concatenate([qa_ref[0, blocks[step], :], qb_ref[0, blocks[step], :]], axis=0)
            s[step] = lax.dot_general(q2, kt, (((1,), (1,)), ((), ())), preferred_element_type=F32)
        if 0 <= step - 1 < n:
            p[step - 1] = jnp.exp2(s[step - 1] - jnp.max(s[step - 1], axis=1, keepdims=True)).astype(BF16)
            s[step - 1] = None
        if 0 <= step - 2 < n:
            pv = _dot(p[step - 2], vt)
            ratio = pv[:, :DIFF_V] / pv[:, DIFF_V:]
            o = ratio[:q_rows] - lam * ratio[q_rows:]
            o = o * lax.rsqrt(jnp.mean(o * o, axis=1, keepdims=True) + SUBLN_EPS) * sw_ref[...]
            o_ref[0, blocks[step - 2], :] = o * (1.0 - lam_init)
            p[step - 2] = None


def diff_attention(qa, qb, k, v, lq, lk, sw, lam_init, tq=1024, q_rows=128):
    b, s, _ = qa.shape
    tq = min(tq, s)
    qspec = pl.BlockSpec((1, tq, LANES), lambda bi, h, i: (bi, i, h))
    kspec = pl.BlockSpec((1, s, LANES), lambda bi, h, i: (bi, 0, h))
    small = lambda shape: pl.BlockSpec(shape, lambda bi, h, i: (0, 0))
    return pl.pallas_call(
        functools.partial(_attn_kernel, lam_init, min(q_rows, tq)),
        grid=(b, DIFF_HEADS, s // tq),
        in_specs=[qspec, qspec, kspec, pl.BlockSpec((1, s, 2 * DIFF_V), lambda bi, h, i: (bi, 0, h)),
                  small((2, DIFF_QK)), small((2, DIFF_QK)), small((1, DIFF_V))],
        out_specs=qspec,
        out_shape=jax.ShapeDtypeStruct((b, s, D_DIFF), F32),
        compiler_params=_cparams(("parallel", "parallel", "parallel")),
        name="diff_attention",
    )(qa, qb, k, v, lq, lk, sw.reshape(1, DIFF_V))


def _pool_kernel(seq, u_ref, up_ref, un_ref, mix_ref, scale_ref, o_ref):
    i = pl.program_id(1)
    n = pl.num_programs(1)
    u = u_ref[0]
    ts = u.shape[0]
    prev = jnp.where(i > 0, up_ref[0], 0.0)
    nxt = jnp.where(i < n - 1, un_ref[0], 0.0)
    ext = jnp.concatenate([prev, u, nxt], axis=0)
    rows = ext.shape[0]
    back = lambda t, sft: pltpu.roll(t, sft, 0)
    fwd = lambda t, sft: pltpu.roll(t, rows - sft, 0)
    w2 = ext + back(ext, 1)
    w4 = back(w2, 1) + fwd(w2, 1)
    w8 = back(w4, 2) + fwd(w4, 2)
    w16 = back(w8, 4) + fwd(w8, 4)
    lane = lax.broadcasted_iota(jnp.int32, (ts, D_POOL), 1)
    t = lax.broadcasted_iota(jnp.int32, (ts, D_POOL), 0) + i * ts
    grp = lane // HEAD_DIM
    sel = lambda vals: jnp.where(grp == 0, vals[0], jnp.where(grp == 1, vals[1], jnp.where(grp == 2, vals[2], vals[3])))
    wsum = sel([w[HALO:HALO + ts] for w in (w2, w4, w8, w16)])
    half = sel([wd // 2 for wd in POOL_WINDOWS])
    width = sel(list(POOL_WINDOWS))
    cnt = jnp.minimum(t + width - half, seq) - jnp.maximum(t - half, 0)
    pooled = wsum / cnt.astype(F32) - u
    o_ref[0] = _dot(pooled, mix_ref[...], precision=HI) * scale_ref[...]


def pool_mixer(proj, mix_bd, scale, ts=512):
    b, s, _ = proj.shape
    nblk = ts // HALO
    last = s // HALO - 1
    cb = REST_POOL // D_POOL
    hb = REST_POOL // D_POOL
    return pl.pallas_call(
        functools.partial(_pool_kernel, s),
        grid=(b, s // ts),
        in_specs=[
            pl.BlockSpec((1, ts, D_POOL), lambda bi, i: (bi, i, cb)),
            pl.BlockSpec((1, HALO, D_POOL), lambda bi, i: (bi, jnp.maximum(i * nblk - 1, 0), hb)),
            pl.BlockSpec((1, HALO, D_POOL), lambda bi, i: (bi, jnp.minimum((i + 1) * nblk, last), hb)),
            pl.BlockSpec((D_POOL, D_POOL), lambda bi, i: (0, 0)),
            pl.BlockSpec((1, D_POOL), lambda bi, i: (0, 0)),
        ],
        out_specs=pl.BlockSpec((1, ts, D_POOL), lambda bi, i: (bi, i, 0)),
        out_shape=jax.ShapeDtypeStruct((b, s, D_POOL), F32),
        compiler_params=_cparams(("parallel", "parallel")),
        name="pool_mixer",
    )(proj, proj, proj, mix_bd, scale.reshape(1, D_POOL))


def _out_proj_kernel(x_ref, yf_ref, yb_ref, bonus_ref, g_ref, att_ref, pool_ref, lw_ref, lb_ref, bd_ref, w_ref, o_ref):
    y = yf_ref[...] + yb_ref[...]
    bd = bd_ref[...]
    mu = _dot_exact_rhs(y, bd) * (1.0 / HEAD_DIM)
    yc = y - mu
    var = _dot_exact_rhs(yc * yc, bd) * (1.0 / HEAD_DIM)
    ya = (yc * lax.rsqrt(var + GN_EPS) * lw_ref[...] + lb_ref[...] + bonus_ref[...]) * g_ref[...]
    cat = jnp.concatenate([ya, att_ref[...], pool_ref[...]], axis=1).astype(BF16)
    o_ref[...] = x_ref[...] + _dot(cat, w_ref[...])


def out_proj(x2d, yf, yb, bonus, g, att, pool, lnx_w, lnx_b, bd, w_bf16, tm=512):
    t, d = x2d.shape
    rows = lambda n: pl.BlockSpec((tm, n), lambda i: (i, 0))
    fixed = lambda shape: pl.BlockSpec(shape, lambda i: (0, 0))
    return pl.pallas_call(
        _out_proj_kernel,
        grid=(t // tm,),
        in_specs=[rows(d),
                  rows(D_RWKV), rows(D_RWKV), rows(D_RWKV), rows(D_RWKV), rows(D_DIFF), rows(D_POOL),
                  fixed((1, D_RWKV)), fixed((1, D_RWKV)), fixed((D_RWKV, D_RWKV)), fixed((d, d))],
        out_specs=rows(d),
        out_shape=jax.ShapeDtypeStruct((t, d), F32),
        compiler_params=_cparams(("parallel",)),
        name="out_proj",
    )(x2d, yf, yb, bonus, g, att, pool, lnx_w.reshape(1, -1), lnx_b.reshape(1, -1), bd, w_bf16)


FF_TILE = 512
TOP_K = 2


def _swiglu_tile(x, wg, wu, wd):
    gate = _dot(x, wg)
    return _dot((gate * _sigmoid(gate) * _dot(x, wu)).astype(BF16), wd)


def _ffn_kernel(x_ref, g_ref, wg_ref, wu_ref, wd_ref, o_ref, h_ref, acc_ref):
    f = pl.program_id(1)

    @pl.when(f == 0)
    def _():
        h_ref[...] = _rms(x_ref[...], g_ref[...]).astype(BF16)
        acc_ref[...] = jnp.zeros_like(acc_ref)

    acc_ref[...] += _swiglu_tile(h_ref[...], wg_ref[...], wu_ref[...], wd_ref[...])

    @pl.when(f == pl.num_programs(1) - 1)
    def _():
        o_ref[...] = x_ref[...] + acc_ref[...]


def ffn_dense(x2d, g, wg, wu, wd, tm=1024):
    t, d = x2d.shape
    ff = wg.shape[1]
    return pl.pallas_call(
        _ffn_kernel,
        grid=(t // tm, ff // FF_TILE),
        in_specs=[pl.BlockSpec((tm, d), lambda i, f: (i, 0)),
                  pl.BlockSpec((1, d), lambda i, f: (0, 0)),
                  pl.BlockSpec((d, FF_TILE), lambda i, f: (0, f)),
                  pl.BlockSpec((d, FF_TILE), lambda i, f: (0, f)),
                  pl.BlockSpec((FF_TILE, d), lambda i, f: (f, 0))],
        out_specs=pl.BlockSpec((tm, d), lambda i, f: (i, 0)),
        out_shape=jax.ShapeDtypeStruct((t, d), F32),
        scratch_shapes=[pltpu.VMEM((tm, d), BF16), pltpu.VMEM((tm, d), F32)],
        compiler_params=_cparams(("parallel", "arbitrary")),
        name="ffn_dense",
    )(x2d, g.reshape(1, d), wg, wu, wd)


def _router_kernel(x_ref, g_ref, wr_ref, h_ref, comb_ref, rank_ref, cnt_ref):
    h = _rms(x_ref[...], g_ref[...])
    h_ref[...] = h.astype(BF16)
    logits = _dot(h, wr_ref[...], precision=HI)
    lane = lax.broadcasted_iota(jnp.int32, logits.shape, 1)
    logits = jnp.where(lane < N_EXPERTS, logits, -jnp.inf)
    v1 = jnp.max(logits, axis=1, keepdims=True)
    i1 = jnp.min(jnp.where(logits == v1, lane, LANES), axis=1, keepdims=True)
    rest = jnp.where(lane == i1, -jnp.inf, logits)
    v2 = jnp.max(rest, axis=1, keepdims=True)
    i2 = jnp.min(jnp.where(rest == v2, lane, LANES), axis=1, keepdims=True)
    e2 = jnp.exp(v2 - v1)
    g1 = 1.0 / (1.0 + e2)
    comb = jnp.where(lane == i1, g1, jnp.where(lane == i2, e2 * g1, 0.0))
    comb_ref[...] = comb
    tm = comb.shape[0]
    routed = comb > 0.0
    mask = jnp.where(routed, 1.0, 0.0).astype(BF16)
    earlier = lax.broadcasted_iota(jnp.int32, (tm, tm), 1) < lax.broadcasted_iota(jnp.int32, (tm, tm), 0)
    rank = _dot(jnp.where(earlier, 1.0, 0.0).astype(BF16), mask)
    rank_ref[...] = jnp.where(routed, rank, -1.0)
    cnt_ref[0] = _dot(jnp.ones((HALO, tm), BF16), mask)


def moe_router(x2d, g, wr_pad, tm):
    t, d = x2d.shape
    tok = pl.BlockSpec((tm, LANES), lambda i: (i, 0))
    return pl.pallas_call(
        _router_kernel,
        grid=(t // tm,),
        in_specs=[pl.BlockSpec((tm, d), lambda i: (i, 0)),
                  pl.BlockSpec((1, d), lambda i: (0, 0)),
                  pl.BlockSpec((d, LANES), lambda i: (0, 0))],
        out_specs=[pl.BlockSpec((tm, d), lambda i: (i, 0)), tok, tok,
                   pl.BlockSpec((1, HALO, LANES), lambda i: (i, 0, 0))],
        out_shape=[jax.ShapeDtypeStruct((t, d), BF16), jax.ShapeDtypeStruct((t, LANES), F32),
                   jax.ShapeDtypeStruct((t, LANES), F32), jax.ShapeDtypeStruct((t // tm, HALO, LANES), F32)],
        compiler_params=_cparams(("parallel",)),
        name="moe_router",
    )(x2d, g.reshape(1, d), wr_pad)


MOE_GROUP = 8


def _lane_column(x, e):
    onehot = jnp.where(lax.broadcasted_iota(jnp.int32, (LANES, LANES), 0) == e, 1.0, 0.0)
    return _dot_exact_rhs(x, onehot)


def _moe_gather_kernel(item_ref, dest_ref, h_ref, comb_ref, rank_t_ref, xc_ref, gc_ref):
    i, k = pl.program_id(0), pl.program_id(1)
    cap, tm = xc_ref.shape[1], h_ref.shape[0]
    e, j, on = item_ref[0, i, k], item_ref[1, i, k], item_ref[2, i, k] > 0

    @pl.when(on)
    def _():
        g_hi, g_lo = _hi_lo(_lane_column(comb_ref[...], e))
        rid = lax.broadcasted_iota(jnp.int32, (cap, tm), 0) + j * cap
        pick = jnp.where(rank_t_ref[0] == rid.astype(F32), 1.0, 0.0).astype(BF16)
        xc_ref[0] = _dot(pick, h_ref[...]).astype(BF16)
        gc2 = _dot(pick, jnp.concatenate([g_hi, g_lo], axis=1))
        gc_ref[0] = gc2[:, :LANES] + gc2[:, LANES:]

    @pl.when(jnp.logical_not(on))
    def _():
        xc_ref[...] = jnp.zeros_like(xc_ref)
        gc_ref[...] = jnp.zeros_like(gc_ref)


def _moe_expert_kernel(total_ref, x_ref, gc_ref, wg_ref, wu_ref, wd_ref, y_ref, acc_ref):
    e, r, f = pl.program_id(0), pl.program_id(1), pl.program_id(2)
    group = x_ref.shape[0]
    nb = jnp.clip(total_ref[e] - r * group, 0, group)
    first, last = f == 0, f == pl.num_programs(2) - 1

    def block(b, carry):
        y = _swiglu_tile(x_ref[b], wg_ref[0], wu_ref[0], wd_ref[0])
        acc_ref[b] = jnp.where(first, y, acc_ref[b] + y)
        return carry

    lax.fori_loop(0, nb, block, 0)

    @pl.when(jnp.logical_and(last, r * group < jnp.maximum(total_ref[e], 1)))
    def _():
        def emit(b, carry):
            gate = jnp.concatenate([gc_ref[b]] * (acc_ref.shape[2] // LANES), axis=1)
            y_ref[b] = jnp.where(b < nb, acc_ref[b] * gate, 0.0).astype(BF16)
            return carry

        lax.fori_loop(0, group, emit, 0)


def _moe_scatter_kernel(item_ref, dest_ref, x_ref, yc_ref, rank_ref, gout_ref, o_ref, acc_ref):
    i, k = pl.program_id(0), pl.program_id(1)
    cap, tm = yc_ref.shape[1], x_ref.shape[0]
    e, j, on = item_ref[0, i, k], item_ref[1, i, k], item_ref[2, i, k] > 0

    @pl.when(k == 0)
    def _():
        acc_ref[...] = jnp.zeros_like(acc_ref)

    @pl.when(on)
    def _():
        slot = jnp.concatenate([_lane_column(rank_ref[...], e)] * (-(-cap // LANES)), axis=1)[:, :cap]
        cid = lax.broadcasted_iota(jnp.int32, (tm, cap), 1) + j * cap
        place = jnp.where(slot == cid.astype(F32), 1.0, 0.0).astype(BF16)
        acc_ref[...] += _dot(place, yc_ref[0])

    @pl.when(k == pl.num_programs(1) - 1)
    def _():
        o_ref[...] = _rms(x_ref[...] + acc_ref[...], gout_ref[...])


def _moe_plan(cnt, cap, n_items, rounds):
    nt, ne = cnt.shape
    nblk = (cnt + cap - 1) // cap
    ends = jnp.cumsum(nblk, axis=1)
    k = jnp.arange(n_items, dtype=jnp.int32)
    total = ends[:, -1:]
    kk = jnp.minimum(k[None, :], total - 1)
    e = jnp.sum((kk[:, :, None] >= ends[:, None, :]).astype(jnp.int32), axis=2)
    j = kk - jnp.take_along_axis(ends - nblk, e, axis=1)
    on = (k[None, :] < total).astype(jnp.int32)
    earlier = jnp.cumsum(nblk, axis=0) - nblk
    region = rounds * MOE_GROUP
    spare = ne * region + jnp.arange(nt, dtype=jnp.int32)[:, None]
    dest = jnp.where(on > 0, e * region + jnp.take_along_axis(earlier, e, axis=1) + j, spare)
    as_i32 = lambda t: t.astype(jnp.int32)
    return as_i32(jnp.stack([e, j, on])), as_i32(dest), as_i32(jnp.sum(nblk, axis=0))


def moe_ffn(x2d, h, comb, rank, rank_t, cnt, wg, wu, wd, g_out, tm, cap):
    t, d = h.shape
    ne, _, ff = wg.shape
    nt, nf, group = t // tm, ff // FF_TILE, MOE_GROUP
    n_items = ne + (TOP_K * tm) // cap
    rounds = -(-(nt + t // cap) // group)
    items, dest, total = _moe_plan(cnt, cap, n_items, rounds)
    n_blocks = (ne * rounds - (-nt // group)) * group
    once = pl.Buffered(1)

    tile = lambda n, **kw: pl.BlockSpec((tm, n), lambda i, k, it, ds: (i, 0), **kw)
    blk = lambda n: pl.BlockSpec((1, cap, n), lambda i, k, it, ds: (ds[i, k], 0, 0))
    xc, gc = pl.pallas_call(
        _moe_gather_kernel,
        grid_spec=pltpu.PrefetchScalarGridSpec(
            num_scalar_prefetch=2, grid=(nt, n_items),
            in_specs=[tile(d), tile(LANES), pl.BlockSpec((1, 1, tm), lambda i, k, it, ds: (it[0, i, k], 0, i))],
            out_specs=[blk(d), blk(LANES)]),
        out_shape=[jax.ShapeDtypeStruct((n_blocks, cap, d), BF16), jax.ShapeDtypeStruct((n_blocks, cap, LANES), F32)],
        compiler_params=_cparams(("parallel", "arbitrary")),
        name="moe_gather",
    )(items, dest, h, comb, rank_t)

    def grp(e, r, tot):
        return e * rounds + jnp.minimum(r, jnp.maximum(-(-tot[e] // group), 1) - 1)

    live = lambda e, r, tot: r * group < jnp.maximum(tot[e], 1)
    fidx = lambda e, r, f, tot: jnp.where(live(e, r, tot), f, nf - 1)
    rows = lambda n, **kw: pl.BlockSpec((group, cap, n), lambda e, r, f, tot: (grp(e, r, tot), 0, 0), **kw)
    yc = pl.pallas_call(
        _moe_expert_kernel,
        grid_spec=pltpu.PrefetchScalarGridSpec(
            num_scalar_prefetch=1, grid=(ne, rounds, nf),
            in_specs=[rows(d, pipeline_mode=once), rows(LANES, pipeline_mode=once),
                      pl.BlockSpec((1, d, FF_TILE), lambda e, r, f, tot: (e, 0, fidx(e, r, f, tot))),
                      pl.BlockSpec((1, d, FF_TILE), lambda e, r, f, tot: (e, 0, fidx(e, r, f, tot))),
                      pl.BlockSpec((1, FF_TILE, d), lambda e, r, f, tot: (e, fidx(e, r, f, tot), 0))],
            out_specs=rows(d, pipeline_mode=once),
            scratch_shapes=[pltpu.VMEM((group, cap, d), F32)]),
        out_shape=jax.ShapeDtypeStruct((n_blocks, cap, d), BF16),
        compiler_params=_cparams(("parallel", "arbitrary", "arbitrary")),
        name="moe_expert",
    )(total, xc, gc, wg, wu, wd)

    return pl.pallas_call(
        _moe_scatter_kernel,
        grid_spec=pltpu.PrefetchScalarGridSpec(
            num_scalar_prefetch=2, grid=(nt, n_items),
            in_specs=[tile(d, pipeline_mode=once), blk(d), tile(LANES),
                      pl.BlockSpec((1, d), lambda i, k, it, ds: (0, 0))],
            out_specs=tile(d, pipeline_mode=once),
            scratch_shapes=[pltpu.VMEM((tm, d), F32)]),
        out_shape=jax.ShapeDtypeStruct((t, d), F32),
        compiler_params=_cparams(("parallel", "arbitrary")),
        name="moe_scatter",
    )(items, dest, x2d, yc, rank, g_out.reshape(1, d))


def _block_diag_ones(n, blk):
    i = jnp.arange(n) // blk
    return (i[:, None] == i[None, :]).astype(F32)


def _rope_tables(positions):
    half = ROT_DIM // 2
    inv_freq = jnp.power(ROPE_THETA, -(jnp.arange(half, dtype=F32) * 2.0 / ROT_DIM))
    ang = positions.astype(F32)[..., None] * inv_freq
    cos, sin = jnp.cos(ang), jnp.sin(ang)
    one = jnp.ones(cos.shape[:-1] + (DIFF_QK - ROT_DIM,), F32)
    zero = jnp.zeros_like(one)
    z8 = jnp.zeros_like(sin)
    cs = jnp.concatenate([cos, cos, one], axis=-1)
    s1 = jnp.concatenate([z8, sin, zero], axis=-1)
    s2 = jnp.concatenate([-sin, z8, zero], axis=-1)
    return tuple(jnp.concatenate([t, t], axis=-1) for t in (cs, s1, s2))


def kernel(x, positions, norm_mix, w_in_first, w_in_rest, tshift, decay_bias, decay_up, iclr_bias, iclr_up, gate_up, k_k, k_a, r_k, lnx_w, lnx_b, vres_bias, vres_up, lambda_q, lambda_k, subln_w, pool_mix, pool_scale, w_out, norm_ffn, ffn_gate, ffn_up, ffn_down, router, exp_gate, exp_up, exp_down, norm_out):
    bsz, seq, d = x.shape
    depth = norm_mix.shape[0]
    x2d = x.reshape(bsz * seq, d)
    bd = _block_diag_ones(D_RWKV, HEAD_DIM)
    cs, s1, s2 = (t.reshape(bsz * seq, LANES) for t in _rope_tables(positions))
    v_first = None
    for l in range(depth):
        if l == 0:
            w_in = w_in_first
        else:
            wr = w_in_rest[l - 1]
            w_in = jnp.concatenate([wr[:, :D_SHIFT], wr[:, D_SHIFT + MV_LORA:], wr[:, D_SHIFT:D_SHIFT + MV_LORA],
                                    jnp.zeros((d, LANES - MV_LORA), F32)], axis=1)
        proj, qa, qb, kr, vb = in_proj(x2d, norm_mix[l], w_in.astype(BF16), cs, s1, s2)
        proj, qa, qb, kr, vb = (t.reshape(bsz, seq, -1) for t in (proj, qa, qb, kr, vb))

        zeros = jnp.zeros((DECAY_LORA, 2 * D_RWKV), F32)
        dec = jnp.concatenate([decay_up[l, 0], decay_up[l, 1]], axis=1)
        icl = jnp.concatenate([iclr_up[l, 0], iclr_up[l, 1]], axis=1)
        w_lora = jnp.concatenate([
            jnp.concatenate([dec, zeros], axis=1), jnp.zeros((AAA_LORA, 4 * D_RWKV), F32),
            jnp.zeros((DECAY_LORA, 4 * D_RWKV), F32), jnp.concatenate([zeros, icl], axis=1)], axis=0)
        prm = dict(tshift=tshift[l], w_lora=w_lora,
                   b_lora=jnp.concatenate([decay_bias[l, 0], decay_bias[l, 1], iclr_bias[l, 0], iclr_bias[l, 1]]).reshape(1, -1),
                   gate_up=gate_up[l], k_k=k_k[l].reshape(1, -1), k_a=k_a[l].reshape(1, -1),
                   r_k=r_k[l].reshape(1, -1), bd=bd)
        if l > 0:
            prm["vres_bias"] = vres_bias[l - 1].reshape(1, -1)
            prm["vres_up"] = jnp.concatenate([vres_up[l - 1], jnp.zeros((LANES - MV_LORA, D_RWKV), F32)], axis=0)
        r, v, kk, lw, kd, bb, g, bonus = rwkv_prep(proj, v_first, prm)
        if l == 0:
            v_first = v
        yf, yb = wkv_scan(r, v, kk, lw, kd, bb)

        lam_init = 0.8 - 0.6 * math.exp(-0.3 * l)
        att = diff_attention(qa, qb, kr, vb, lambda_q[l], lambda_k[l], subln_w[l], lam_init)

        mix_bd = jax.scipy.linalg.block_diag(*[pool_mix[l, gi] for gi in range(len(POOL_WINDOWS))])
        pool = pool_mixer(proj, mix_bd, pool_scale[l])

        flat = lambda t: t.reshape(bsz * seq, -1)
        x2d = out_proj(x2d, flat(yf), flat(yb), flat(bonus), flat(g), flat(att), flat(pool), lnx_w[l], lnx_b[l], bd,
                       w_out[l].astype(BF16))
        i = l // 2
        if l % 2 == 0:
            x2d = ffn_dense(x2d, norm_ffn[l], ffn_gate[i].astype(BF16), ffn_up[i].astype(BF16), ffn_down[i].astype(BF16))
        else:
            wr_pad = jnp.concatenate([router[i], jnp.zeros((d, LANES - N_EXPERTS), F32)], axis=1)
            tm = min(2048, bsz * seq)
            cap = 9 * tm // 32
            h, comb, rank, cnt = moe_router(x2d, norm_ffn[l], wr_pad, tm)
            rank_t = rank[:, :N_EXPERTS].T.reshape(N_EXPERTS, 1, -1)
            cnt = cnt[:, 0, :N_EXPERTS].astype(jnp.int32)
            assert l == depth - 1, "the routed mixer's residual add is fused with the final norm"
            x2d = moe_ffn(x2d, h, comb, rank, rank_t, cnt, exp_gate[i].astype(BF16), exp_up[i].astype(BF16),
                          exp_down[i].astype(BF16), norm_out, tm, cap)
    return x2d.reshape(bsz, seq, d)
```

```python
import functools
import math

import jax
import jax.numpy as jnp
from jax import lax
from jax.experimental import pallas as pl
from jax.experimental.pallas import tpu as pltpu

F32 = jnp.float32
BF16 = jnp.bfloat16
HI = lax.Precision.HIGHEST

D_MODEL = 1024
RWKV_HEADS = 4
HEAD_DIM = 64
D_RWKV = RWKV_HEADS * HEAD_DIM
DECAY_LORA = 64
AAA_LORA = 64
MV_LORA = 32
GATE_LORA = 128
D_SHIFT = 3 * D_RWKV + DECAY_LORA + AAA_LORA + GATE_LORA
W_DECAY_SCALE = 0.6065306597126334
GN_EPS = 64e-5
DIFF_HEADS = 4
DIFF_QK = 64
DIFF_V = 128
D_DIFF_QK = DIFF_HEADS * 2 * DIFF_QK
D_DIFF = DIFF_HEADS * DIFF_V
ROT_DIM = DIFF_QK // 4
ROPE_THETA = 500000.0
SUBLN_EPS = 1e-5
POOL_WINDOWS = (2, 4, 8, 16)
D_POOL = 256
D_FF = 3584
N_EXPERTS = 8
NORM_EPS = 1e-6
P_IN = D_SHIFT + 2 * D_DIFF_QK + D_DIFF + D_POOL
COL_Q, COL_K, COL_V, COL_POOL = 1024, 1536, 2048, 2560
REST_POOL, REST_VDOWN = D_SHIFT, D_SHIFT + D_POOL

LANES = 128
HALO = 8
VMEM_LIMIT = 56 * 1024 * 1024

SCAN_CHUNK = 128


def _cparams(sem):
    return pltpu.CompilerParams(dimension_semantics=sem, vmem_limit_bytes=VMEM_LIMIT)


def _sigmoid(x):
    return 1.0 / (1.0 + jnp.exp(-x))


def _dot(a, b, **kw):
    return jnp.dot(a, b, preferred_element_type=F32, **kw)


def _bdot(a, b):
    return jnp.dot(a.astype(BF16), b.astype(BF16), preferred_element_type=F32)


def _rms(x, g):
    return x * lax.rsqrt(jnp.mean(x * x, axis=-1, keepdims=True) + NORM_EPS) * g


def _hi_lo(x):
    hi = x.astype(BF16)
    return hi, (x - hi.astype(F32)).astype(BF16)


def _dot_exact_rhs(a, w):
    hi, lo = _hi_lo(a)
    both = _dot(jnp.concatenate([hi, lo], axis=0), w.astype(BF16))
    return both[:a.shape[0]] + both[a.shape[0]:]


def _dot3(a, w):
    a_hi, a_lo = _hi_lo(a)
    w_hi, w_lo = _hi_lo(w)
    both = _dot(jnp.concatenate([a_hi, a_lo], axis=0), w_hi)
    return both[:a.shape[0]] + both[a.shape[0]:] + _dot(a_hi, w_lo)


def _in_proj_kernel(x_ref, g_ref, w_ref, c_ref, s1_ref, s2_ref, rest_ref, qa_ref, qb_ref, ko_ref, vo_ref):
    h = _rms(x_ref[...], g_ref[...]).astype(BF16)
    p = _dot(h, w_ref[...])
    rest_ref[:, :D_SHIFT] = p[:, :D_SHIFT]
    rest_ref[:, D_SHIFT:] = p[:, COL_POOL:]
    cs, s1, s2 = c_ref[...], s1_ref[...], s2_ref[...]
    first = lax.broadcasted_iota(jnp.int32, cs.shape, 1) < DIFF_QK
    scale = DIFF_QK ** -0.5 * math.log2(math.e)

    def rope(t):
        return t * cs + pltpu.roll(t, ROT_DIM // 2, 1) * s1 + pltpu.roll(t, LANES - ROT_DIM // 2, 1) * s2

    for j in range(D_DIFF_QK // LANES):
        sl = slice(LANES * j, LANES * j + LANES)
        q = rope(p[:, COL_Q + LANES * j:COL_Q + LANES * (j + 1)]) * scale
        qa_ref[:, sl] = jnp.where(first, q, 0.0).astype(BF16)
        qb_ref[:, sl] = jnp.where(first, 0.0, q).astype(BF16)
        ko_ref[:, sl] = rope(p[:, COL_K + LANES * j:COL_K + LANES * (j + 1)]).astype(BF16)
    ones = jnp.ones((p.shape[0], DIFF_V), BF16)
    for hd in range(DIFF_HEADS):
        vo_ref[:, 2 * DIFF_V * hd:2 * DIFF_V * hd + DIFF_V] = p[:, COL_V + DIFF_V * hd:COL_V + DIFF_V * (hd + 1)].astype(BF16)
        vo_ref[:, 2 * DIFF_V * hd + DIFF_V:2 * DIFF_V * (hd + 1)] = ones


def in_proj(x2d, g, w_bf16, cs, s1, s2, tm=512):
    t, d = x2d.shape
    n = w_bf16.shape[1]
    n_rest = n - (COL_POOL - D_SHIFT)
    rows = lambda width: pl.BlockSpec((tm, width), lambda i: (i, 0))
    fixed = lambda shape: pl.BlockSpec(shape, lambda i: (0, 0))
    qkv = jax.ShapeDtypeStruct((t, D_DIFF_QK), BF16)
    return pl.pallas_call(
        _in_proj_kernel,
        grid=(t // tm,),
        in_specs=[rows(d), fixed((1, d)), fixed((d, n)), rows(LANES), rows(LANES), rows(LANES)],
        out_specs=[rows(n_rest), rows(D_DIFF_QK), rows(D_DIFF_QK), rows(D_DIFF_QK), rows(2 * D_DIFF)],
        out_shape=[jax.ShapeDtypeStruct((t, n_rest), F32), qkv, qkv, qkv, jax.ShapeDtypeStruct((t, 2 * D_DIFF), BF16)],
        compiler_params=_cparams(("parallel",)),
        name="in_proj",
    )(x2d, g.reshape(1, d), w_bf16, cs, s1, s2)


def _shifted(u, prev_row, next_row):
    ts = u.shape[0]
    row = lax.broadcasted_iota(jnp.int32, u.shape, 0)
    prev = jnp.where(row == 0, prev_row, pltpu.roll(u, 1, 0))
    nxt = jnp.where(row == ts - 1, next_row, pltpu.roll(u, ts - 1, 0))
    return prev, nxt


def _rwkv_prep_kernel(has_vres, *refs):
    if has_vres:
        (u_ref, up_ref, un_ref, vd_ref, vf_ref, vb_ref, vu_ref, mu_ref, wl_ref, bias_ref, gu_ref,
         kkw_ref, ka_ref, rk_ref, bd_ref,
         r_ref, v_ref, kk_ref, lw_ref, kd_ref, bb_ref, g_ref, bonus_ref) = refs
    else:
        (u_ref, up_ref, un_ref, mu_ref, wl_ref, bias_ref, gu_ref,
         kkw_ref, ka_ref, rk_ref, bd_ref,
         r_ref, v_ref, kk_ref, lw_ref, kd_ref, bb_ref, g_ref, bonus_ref) = refs
    i = pl.program_id(1)
    n = pl.num_programs(1)
    u = u_ref[0]
    prev_row = jnp.where(i > 0, up_ref[0, HALO - 1:HALO, :], 0.0)
    next_row = jnp.where(i < n - 1, un_ref[0, 0:1, :], 0.0)
    prev, nxt = _shifted(u, prev_row, next_row)
    us = u + mu_ref[0:1, :] * (prev - u) + mu_ref[1:2, :] * (nxt - u)
    r = us[:, 0:256]
    k = us[:, 256:512]
    v = us[:, 512:768]
    xwa = us[:, 768:896]
    xg = us[:, 896:1024]
    lhs = jnp.concatenate([jnp.tanh(xwa), xwa], axis=1)
    z = _dot3(lhs, wl_ref[...]) + bias_ref[...]
    if has_vres:
        mix = _sigmoid(vb_ref[...] + _dot3(vd_ref[0], vu_ref[...]))
        v = v + (vf_ref[0] - v) * mix
    g = _dot3(_sigmoid(xg), gu_ref[...])
    bd = bd_ref[...]
    kkr = k * kkw_ref[...]
    ss = _dot_exact_rhs(kkr * kkr, bd)
    kk = kkr * lax.rsqrt(jnp.maximum(ss, 1e-24))
    r_ref[0] = r
    v_ref[0] = v
    kk_ref[0] = kk
    g_ref[0] = g
    bonus = jnp.zeros_like(v)
    for d in range(2):
        lw = -W_DECAY_SCALE * _sigmoid(z[:, 256 * d:256 * d + 256])
        a = _sigmoid(z[:, 512 + 256 * d:768 + 256 * d])
        kd = k * (1.0 + (a - 1.0) * ka_ref[...])
        lw_ref[d, 0] = lw
        kd_ref[d, 0] = kd
        bb_ref[d, 0] = kk * a
        bonus = bonus + _dot_exact_rhs(r * kd * rk_ref[...], bd) * v
    bonus_ref[0] = bonus


def rwkv_prep(proj, v_first, p, ts=512):
    b, s, _ = proj.shape
    has_vres = v_first is not None
    nblk = ts // HALO
    last = s // HALO - 1
    row = lambda shape: pl.BlockSpec(shape, lambda bi, i: (0,) * len(shape))
    in_specs = [
        pl.BlockSpec((1, ts, D_SHIFT), lambda bi, i: (bi, i, 0)),
        pl.BlockSpec((1, HALO, D_SHIFT), lambda bi, i: (bi, jnp.maximum(i * nblk - 1, 0), 0)),
        pl.BlockSpec((1, HALO, D_SHIFT), lambda bi, i: (bi, jnp.minimum((i + 1) * nblk, last), 0)),
    ]
    args = [proj, proj, proj]
    if has_vres:
        in_specs += [
            pl.BlockSpec((1, ts, LANES), lambda bi, i: (bi, i, REST_VDOWN // LANES)),
            pl.BlockSpec((1, ts, D_RWKV), lambda bi, i: (bi, i, 0)),
            row((1, D_RWKV)), row((LANES, D_RWKV)),
        ]
        args += [proj, v_first, p["vres_bias"], p["vres_up"]]
    in_specs += [row((2, D_SHIFT)), row((256, 1024)), row((1, 1024)), row((GATE_LORA, D_RWKV)),
                 row((1, D_RWKV)), row((1, D_RWKV)), row((1, D_RWKV)), row((D_RWKV, D_RWKV))]
    args += [p["tshift"], p["w_lora"], p["b_lora"], p["gate_up"], p["k_k"], p["k_a"], p["r_k"], p["bd"]]
    one = pl.BlockSpec((1, ts, D_RWKV), lambda bi, i: (bi, i, 0))
    two = pl.BlockSpec((2, 1, ts, D_RWKV), lambda bi, i: (0, bi, i, 0))
    s1 = jax.ShapeDtypeStruct((b, s, D_RWKV), F32)
    s2 = jax.ShapeDtypeStruct((2, b, s, D_RWKV), F32)
    return pl.pallas_call(
        functools.partial(_rwkv_prep_kernel, has_vres),
        grid=(b, s // ts),
        in_specs=in_specs,
        out_specs=[one, one, one, two, two, two, one, one],
        out_shape=[s1, s1, s1, s2, s2, s2, s1, s1],
        compiler_params=_cparams(("parallel", "parallel")),
        name="rwkv_prep",
    )(*args)


def _unit_lower_inverses(ms, eye):
    xs = [eye - m for m in ms]
    ps = [_bdot(m, m) for m in ms]
    levels = int(math.log2(ms[0].shape[0])) - 1
    for k in range(levels):
        nxt = [_bdot(p, p) for p in ps] if k < levels - 1 else None
        xs = [x + _bdot(x, p) for x, p in zip(xs, ps)]
        ps = nxt
    return xs


def _split3(x):
    hi = x.astype(BF16)
    r1 = x - hi.astype(F32)
    mid = r1.astype(BF16)
    lo = (r1 - mid.astype(F32)).astype(BF16)
    return hi, mid, lo


def _chunk_terms(rev, r, v, kk, lw, kd, bb):
    c = SCAN_CHUNK
    n = len(r)
    ti = lax.broadcasted_iota(jnp.int32, (c, c), 0)
    tj = lax.broadcasted_iota(jnp.int32, (c, c), 1)
    eye = (ti == tj).astype(F32)
    same_head = (ti // HEAD_DIM) == (tj // HEAD_DIM)
    first = lax.broadcasted_iota(jnp.int32, (c, LANES), 1) < HEAD_DIM
    first2 = jnp.concatenate([first, first], axis=1)
    strict = [(ti - tj if rv else tj - ti) < 0 for rv in rev]
    incl = [(ti - tj if rv else tj - ti) <= 0 for rv in rev]

    cum = []
    for i in range(n):
        cum3 = _dot(jnp.where(incl[i], 1.0, 0.0).astype(BF16), jnp.concatenate(_split3(lw[i]), axis=1))
        cum.append(cum3[:, :LANES] + cum3[:, LANES:2 * LANES] + cum3[:, 2 * LANES:])
    a = [kk[i] * jnp.exp(cum[i] - lw[i]) for i in range(n)]
    rt = [r[i] * jnp.exp(cum[i]) for i in range(n)]
    einv = [jnp.exp(-cm) for cm in cum]
    kt_t = [(kd[i] * einv[i]).T for i in range(n)]
    bt_t = [(bb[i] * einv[i]).T for i in range(n)]
    last = [cum[i][0:1] if rev[i] else cum[i][c - 1:c] for i in range(n)]
    g_all = [jnp.exp(jnp.broadcast_to(x, (c, LANES))).T for x in last]
    vb = [x.astype(BF16) for x in v]
    rhs = [jnp.concatenate([bt_t[i], kt_t[i]], axis=1).astype(BF16) for i in range(n)]
    heads = [(i, hm) for i in range(n) for hm in (first, jnp.logical_not(first))]
    gm = [_dot(jnp.concatenate([jnp.where(hm, a[i], 0.0), jnp.where(hm, rt[i], 0.0)], axis=0).astype(BF16), rhs[i])
          for i, hm in heads]
    t_inv = _unit_lower_inverses([jnp.where(strict[i], g[:c, :c], 0.0) for (i, _), g in zip(heads, gm)], eye)
    mv = [_dot(jnp.where(strict[i], g[:c, c:], 0.0).astype(BF16), vb[i]) for (i, _), g in zip(heads, gm)]
    tam = [_bdot(t, jnp.concatenate([a[i], m], axis=1)) for (i, _), t, m in zip(heads, t_inv, mv)]
    lt = [_bdot(jnp.where(incl[i], g[c:, :c], 0.0), x) for (i, _), g, x in zip(heads, gm, tam)]
    lv = [_dot(jnp.where(incl[i], g[c:, c:], 0.0).astype(BF16), vb[i]) for (i, _), g in zip(heads, gm)]
    tam = [jnp.where(first2, tam[2 * i], tam[2 * i + 1]) for i in range(n)]
    lt = [jnp.where(first2, lt[2 * i], lt[2 * i + 1]) for i in range(n)]
    lv = [jnp.where(first, lv[2 * i], lv[2 * i + 1]) for i in range(n)]
    btx = [_bdot(bt_t[i], tam[i]) for i in range(n)]
    kv = [_dot(kt_t[i].astype(BF16), vb[i]) for i in range(n)]
    phi = [g_all[i] * (eye - jnp.where(same_head, btx[i][:, :LANES], 0.0)) for i in range(n)]
    psi = [g_all[i] * jnp.where(same_head, kv[i] - btx[i][:, LANES:], 0.0) for i in range(n)]
    ra = [rt[i] - lt[i][:, :LANES] for i in range(n)]
    oc = [lv[i] - lt[i][:, LANES:] for i in range(n)]
    return phi, psi, ra, oc


def _wkv_kernel(rf_ref, vf_ref, kkf_ref, lwf_ref, kdf_ref, bbf_ref,
                rb_ref, vb_ref, kkb_ref, lwb_ref, kdb_ref, bbb_ref, yf_ref, yb_ref, h_ref):
    @pl.when(pl.program_id(1) == 0)
    def _():
        h_ref[...] = jnp.zeros_like(h_ref)

    c = SCAN_CHUNK
    nsub = rf_ref.shape[1] // c
    pairs = (slice(0, LANES), slice(LANES, 2 * LANES))
    fwd = ((rf_ref, vf_ref, kkf_ref), (lwf_ref, kdf_ref, bbf_ref), yf_ref)
    bwd = ((rb_ref, vb_ref, kkb_ref), (lwb_ref, kdb_ref, bbb_ref), yb_ref)
    probs = [(False, pl.ds(c * j, c), sl) + fwd for sl in pairs for j in range(nsub)]
    probs += [(True, pl.ds(c * j, c), sl) + bwd for sl in pairs for j in reversed(range(nsub))]
    rev = [pr[0] for pr in probs]
    r, v, kk = ([pr[3][j][0, pr[1], pr[2]] for pr in probs] for j in range(3))
    lw, kd, bb = ([pr[4][j][0, 0, pr[1], pr[2]] for pr in probs] for j in range(3))
    phi, psi, ra, oc = _chunk_terms(rev, r, v, kk, lw, kd, bb)
    h = [h_ref[i] for i in range(len(probs) // nsub)]
    for j in range(nsub):
        res = []
        for i in range(len(h)):
            h_hi = h[i].astype(BF16)
            h_lo = (h[i] - h_hi.astype(F32)).astype(BF16)
            lhs = jnp.concatenate([phi[i * nsub + j], ra[i * nsub + j]], axis=0).astype(BF16)
            res.append(_dot(jnp.concatenate([lhs, lhs], axis=1), jnp.concatenate([h_hi, h_lo], axis=0)))
        for i in range(len(h)):
            pr = probs[i * nsub + j]
            h[i] = res[i][:c] + psi[i * nsub + j]
            pr[5][0, pr[1], pr[2]] = res[i][c:] + oc[i * nsub + j]
    for i in range(len(h)):
        h_ref[i] = h[i]


def wkv_scan(r, v, kk, lw, kd, bb, chunks_per_step=2):
    b, s, _ = r.shape
    c = SCAN_CHUNK * chunks_per_step
    nc = s // c
    fwd = pl.BlockSpec((1, c, D_RWKV), lambda bi, ci: (bi, ci, 0))
    bwd = pl.BlockSpec((1, c, D_RWKV), lambda bi, ci: (bi, nc - 1 - ci, 0))
    fwd_d = pl.BlockSpec((1, 1, c, D_RWKV), lambda bi, ci: (0, bi, ci, 0))
    bwd_d = pl.BlockSpec((1, 1, c, D_RWKV), lambda bi, ci: (1, bi, nc - 1 - ci, 0))
    out = jax.ShapeDtypeStruct((b, s, D_RWKV), F32)
    return pl.pallas_call(
        _wkv_kernel,
        grid=(b, nc),
        in_specs=[fwd, fwd, fwd, fwd_d, fwd_d, fwd_d, bwd, bwd, bwd, bwd_d, bwd_d, bwd_d],
        out_specs=[fwd, bwd],
        out_shape=[out, out],
        scratch_shapes=[pltpu.VMEM((4, LANES, LANES), F32)],
        compiler_params=_cparams(("parallel", "arbitrary")),
        name="wkv_scan",
    )(r, v, kk, lw, kd, bb, r, v, kk, lw, kd, bb)


def _attn_kernel(lam_init, q_rows, qa_ref, qb_ref, k_ref, v_ref, lq_ref, lk_ref, sw_ref, o_ref):
    kt = k_ref[0]
    vt = v_ref[0]
    e = jnp.exp(jnp.sum(lq_ref[...] * lk_ref[...], axis=1, keepdims=True))
    lam = e[0:1] - e[1:2] + lam_init
    blocks = [pl.ds(r0, q_rows) for r0 in range(0, qa_ref.shape[1], q_rows)]
    n = len(blocks)
    s, p = [None] * n, [None] * n
    for step in range(n + 2):
        if step < n:
            q2 = jnp.concatenate([qa_ref[0, blocks[step], :], qb_ref[0, blocks[step], :]], axis=0)
            s[step] = lax.dot_general(q2, kt, (((1,), (1,)), ((), ())), preferred_element_type=F32)
        if 0 <= step - 1 < n:
            p[step - 1] = jnp.exp2(s[step - 1] - jnp.max(s[step - 1], axis=1, keepdims=True)).astype(BF16)
            s[step - 1] = None
        if 0 <= step - 2 < n:
            pv = _dot(p[step - 2], vt)
            ratio = pv[:, :DIFF_V] / pv[:, DIFF_V:]
            o = ratio[:q_rows] - lam * ratio[q_rows:]
            o = o * lax.rsqrt(jnp.mean(o * o, axis=1, keepdims=True) + SUBLN_EPS) * sw_ref[...]
            o_ref[0, blocks[step - 2], :] = o * (1.0 - lam_init)
            p[step - 2] = None


def diff_attention(qa, qb, k, v, lq, lk, sw, lam_init, tq=1024, q_rows=128):
    b, s, _ = qa.shape
    tq = min(tq, s)
    qspec = pl.BlockSpec((1, tq, LANES), lambda bi, h, i: (bi, i, h))
    kspec = pl.BlockSpec((1, s, LANES), lambda bi, h, i: (bi, 0, h))
    small = lambda shape: pl.BlockSpec(shape, lambda bi, h, i: (0, 0))
    return pl.pallas_call(
        functools.partial(_attn_kernel, lam_init, min(q_rows, tq)),
        grid=(b, DIFF_HEADS, s // tq),
        in_specs=[qspec, qspec, kspec, pl.BlockSpec((1, s, 2 * DIFF_V), lambda bi, h, i: (bi, 0, h)),
                  small((2, DIFF_QK)), small((2, DIFF_QK)), small((1, DIFF_V))],
        out_specs=qspec,
        out_shape=jax.ShapeDtypeStruct((b, s, D_DIFF), F32),
        compiler_params=_cparams(("parallel", "parallel", "parallel")),
        name="diff_attention",
    )(qa, qb, k, v, lq, lk, sw.reshape(1, DIFF_V))


def _pool_kernel(seq, u_ref, up_ref, un_ref, mix_ref, scale_ref, o_ref):
    i = pl.program_id(1)
    n = pl.num_programs(1)
    u = u_ref[0]
    ts = u.shape[0]
    prev = jnp.where(i > 0, up_ref[0], 0.0)
    nxt = jnp.where(i < n - 1, un_ref[0], 0.0)
    ext = jnp.concatenate([prev, u, nxt], axis=0)
    rows = ext.shape[0]
    back = lambda t, sft: pltpu.roll(t, sft, 0)
    fwd = lambda t, sft: pltpu.roll(t, rows - sft, 0)
    w2 = ext + back(ext, 1)
    w4 = back(w2, 1) + fwd(w2, 1)
    w8 = back(w4, 2) + fwd(w4, 2)
    w16 = back(w8, 4) + fwd(w8, 4)
    lane = lax.broadcasted_iota(jnp.int32, (ts, D_POOL), 1)
    t = lax.broadcasted_iota(jnp.int32, (ts, D_POOL), 0) + i * ts
    grp = lane // HEAD_DIM
    sel = lambda vals: jnp.where(grp == 0, vals[0], jnp.where(grp == 1, vals[1], jnp.where(grp == 2, vals[2], vals[3])))
    wsum = sel([w[HALO:HALO + ts] for w in (w2, w4, w8, w16)])
    half = sel([wd // 2 for wd in POOL_WINDOWS])
    width = sel(list(POOL_WINDOWS))
    cnt = jnp.minimum(t + width - half, seq) - jnp.maximum(t - half, 0)
    pooled = wsum / cnt.astype(F32) - u
    o_ref[0] = _dot(pooled, mix_ref[...], precision=HI) * scale_ref[...]


def pool_mixer(proj, mix_bd, scale, ts=512):
    b, s, _ = proj.shape
    nblk = ts // HALO
    last = s // HALO - 1
    cb = REST_POOL // D_POOL
    hb = REST_POOL // D_POOL
    return pl.pallas_call(
        functools.partial(_pool_kernel, s),
        grid=(b, s // ts),
        in_specs=[
            pl.BlockSpec((1, ts, D_POOL), lambda bi, i: (bi, i, cb)),
            pl.BlockSpec((1, HALO, D_POOL), lambda bi, i: (bi, jnp.maximum(i * nblk - 1, 0), hb)),
            pl.BlockSpec((1, HALO, D_POOL), lambda bi, i: (bi, jnp.minimum((i + 1) * nblk, last), hb)),
            pl.BlockSpec((D_POOL, D_POOL), lambda bi, i: (0, 0)),
            pl.BlockSpec((1, D_POOL), lambda bi, i: (0, 0)),
        ],
        out_specs=pl.BlockSpec((1, ts, D_POOL), lambda bi, i: (bi, i, 0)),
        out_shape=jax.ShapeDtypeStruct((b, s, D_POOL), F32),
        compiler_params=_cparams(("parallel", "parallel")),
        name="pool_mixer",
    )(proj, proj, proj, mix_bd, scale.reshape(1, D_POOL))


def _out_proj_kernel(x_ref, yf_ref, yb_ref, bonus_ref, g_ref, att_ref, pool_ref, lw_ref, lb_ref, bd_ref, w_ref, o_ref):
    y = yf_ref[...] + yb_ref[...]
    bd = bd_ref[...]
    mu = _dot_exact_rhs(y, bd) * (1.0 / HEAD_DIM)
    yc = y - mu
    var = _dot_exact_rhs(yc * yc, bd) * (1.0 / HEAD_DIM)
    ya = (yc * lax.rsqrt(var + GN_EPS) * lw_ref[...] + lb_ref[...] + bonus_ref[...]) * g_ref[...]
    cat = jnp.concatenate([ya, att_ref[...], pool_ref[...]], axis=1).astype(BF16)
    o_ref[...] = x_ref[...] + _dot(cat, w_ref[...])


def out_proj(x2d, yf, yb, bonus, g, att, pool, lnx_w, lnx_b, bd, w_bf16, tm=512):
    t, d = x2d.shape
    rows = lambda n: pl.BlockSpec((tm, n), lambda i: (i, 0))
    fixed = lambda shape: pl.BlockSpec(shape, lambda i: (0, 0))
    return pl.pallas_call(
        _out_proj_kernel,
        grid=(t // tm,),
        in_specs=[rows(d),
                  rows(D_RWKV), rows(D_RWKV), rows(D_RWKV), rows(D_RWKV), rows(D_DIFF), rows(D_POOL),
                  fixed((1, D_RWKV)), fixed((1, D_RWKV)), fixed((D_RWKV, D_RWKV)), fixed((d, d))],
        out_specs=rows(d),
        out_shape=jax.ShapeDtypeStruct((t, d), F32),
        compiler_params=_cparams(("parallel",)),
        name="out_proj",
    )(x2d, yf, yb, bonus, g, att, pool, lnx_w.reshape(1, -1), lnx_b.reshape(1, -1), bd, w_bf16)


FF_TILE = 512
TOP_K = 2


def _swiglu_tile(x, wg, wu, wd):
    gate = _dot(x, wg)
    return _dot((gate * _sigmoid(gate) * _dot(x, wu)).astype(BF16), wd)


def _ffn_kernel(x_ref, g_ref, wg_ref, wu_ref, wd_ref, o_ref, h_ref, acc_ref):
    f = pl.program_id(1)

    @pl.when(f == 0)
    def _():
        h_ref[...] = _rms(x_ref[...], g_ref[...]).astype(BF16)
        acc_ref[...] = jnp.zeros_like(acc_ref)

    acc_ref[...] += _swiglu_tile(h_ref[...], wg_ref[...], wu_ref[...], wd_ref[...])

    @pl.when(f == pl.num_programs(1) - 1)
    def _():
        o_ref[...] = x_ref[...] + acc_ref[...]


def ffn_dense(x2d, g, wg, wu, wd, tm=1024):
    t, d = x2d.shape
    ff = wg.shape[1]
    return pl.pallas_call(
        _ffn_kernel,
        grid=(t // tm, ff // FF_TILE),
        in_specs=[pl.BlockSpec((tm, d), lambda i, f: (i, 0)),
                  pl.BlockSpec((1, d), lambda i, f: (0, 0)),
                  pl.BlockSpec((d, FF_TILE), lambda i, f: (0, f)),
                  pl.BlockSpec((d, FF_TILE), lambda i, f: (0, f)),
                  pl.BlockSpec((FF_TILE, d), lambda i, f: (f, 0))],
        out_specs=pl.BlockSpec((tm, d), lambda i, f: (i, 0)),
        out_shape=jax.ShapeDtypeStruct((t, d), F32),
        scratch_shapes=[pltpu.VMEM((tm, d), BF16), pltpu.VMEM((tm, d), F32)],
        compiler_params=_cparams(("parallel", "arbitrary")),
        name="ffn_dense",
    )(x2d, g.reshape(1, d), wg, wu, wd)


def _router_kernel(x_ref, g_ref, wr_ref, h_ref, comb_ref, rank_ref, cnt_ref):
    h = _rms(x_ref[...], g_ref[...])
    h_ref[...] = h.astype(BF16)
    logits = _dot(h, wr_ref[...], precision=HI)
    lane = lax.broadcasted_iota(jnp.int32, logits.shape, 1)
    logits = jnp.where(lane < N_EXPERTS, logits, -jnp.inf)
    v1 = jnp.max(logits, axis=1, keepdims=True)
    i1 = jnp.min(jnp.where(logits == v1, lane, LANES), axis=1, keepdims=True)
    rest = jnp.where(lane == i1, -jnp.inf, logits)
    v2 = jnp.max(rest, axis=1, keepdims=True)
    i2 = jnp.min(jnp.where(rest == v2, lane, LANES), axis=1, keepdims=True)
    e2 = jnp.exp(v2 - v1)
    g1 = 1.0 / (1.0 + e2)
    comb = jnp.where(lane == i1, g1, jnp.where(lane == i2, e2 * g1, 0.0))
    comb_ref[...] = comb
    tm = comb.shape[0]
    routed = comb > 0.0
    mask = jnp.where(routed, 1.0, 0.0).astype(BF16)
    earlier = lax.broadcasted_iota(jnp.int32, (tm, tm), 1) < lax.broadcasted_iota(jnp.int32, (tm, tm), 0)
    rank = _dot(jnp.where(earlier, 1.0, 0.0).astype(BF16), mask)
    rank_ref[...] = jnp.where(routed, rank, -1.0)
    cnt_ref[0] = _dot(jnp.ones((HALO, tm), BF16), mask)


def moe_router(x2d, g, wr_pad, tm):
    t, d = x2d.shape
    tok = pl.BlockSpec((tm, LANES), lambda i: (i, 0))
    return pl.pallas_call(
        _router_kernel,
        grid=(t // tm,),
        in_specs=[pl.BlockSpec((tm, d), lambda i: (i, 0)),
                  pl.BlockSpec((1, d), lambda i: (0, 0)),
                  pl.BlockSpec((d, LANES), lambda i: (0, 0))],
        out_specs=[pl.BlockSpec((tm, d), lambda i: (i, 0)), tok, tok,
                   pl.BlockSpec((1, HALO, LANES), lambda i: (i, 0, 0))],
        out_shape=[jax.ShapeDtypeStruct((t, d), BF16), jax.ShapeDtypeStruct((t, LANES), F32),
                   jax.ShapeDtypeStruct((t, LANES), F32), jax.ShapeDtypeStruct((t // tm, HALO, LANES), F32)],
        compiler_params=_cparams(("parallel",)),
        name="moe_router",
    )(x2d, g.reshape(1, d), wr_pad)


MOE_GROUP = 4


def _lane_column(x, e):
    onehot = jnp.where(lax.broadcasted_iota(jnp.int32, (LANES, LANES), 0) == e, 1.0, 0.0)
    return _dot_exact_rhs(x, onehot)


def _moe_gather_kernel(item_ref, dest_ref, h_ref, comb_ref, rank_t_ref, xc_ref, gc_ref):
    i, k = pl.program_id(0), pl.program_id(1)
    cap, tm = xc_ref.shape[1], h_ref.shape[0]
    e, j, on = item_ref[0, i, k], item_ref[1, i, k], item_ref[2, i, k] > 0

    @pl.when(on)
    def _():
        g_hi, g_lo = _hi_lo(_lane_column(comb_ref[...], e))
        rid = lax.broadcasted_iota(jnp.int32, (cap, tm), 0) + j * cap
        pick = jnp.where(rank_t_ref[0] == rid.astype(F32), 1.0, 0.0).astype(BF16)
        xc_ref[0] = _dot(pick, h_ref[...]).astype(BF16)
        gc2 = _dot(pick, jnp.concatenate([g_hi, g_lo], axis=1))
        gc_ref[0] = gc2[:, :LANES] + gc2[:, LANES:]

    @pl.when(jnp.logical_not(on))
    def _():
        xc_ref[...] = jnp.zeros_like(xc_ref)
        gc_ref[...] = jnp.zeros_like(gc_ref)


def _moe_expert_kernel(work_ref, total_ref, x_ref, gc_ref, wg_ref, wu_ref, wd_ref, y_ref, acc_ref):
    w, f = pl.program_id(0), pl.program_id(1)
    group, cap, d = x_ref.shape
    e, r, live = work_ref[0, w], work_ref[1, w], work_ref[2, w] > 0
    nb = jnp.clip(total_ref[e] - r * group, 0, group)
    in_use = lax.broadcasted_iota(jnp.int32, (group * cap, 1), 0) < nb * cap

    @pl.when(live)
    def _():
        x = jnp.where(in_use, x_ref[...].reshape(group * cap, d), 0.0).astype(BF16)
        y = _swiglu_tile(x, wg_ref[0], wu_ref[0], wd_ref[0])
        acc_ref[...] = jnp.where(f == 0, y, acc_ref[...] + y)

    @pl.when(jnp.logical_and(live, f == pl.num_programs(1) - 1))
    def _():
        gate = jnp.concatenate([gc_ref[...].reshape(group * cap, LANES)] * (d // LANES), axis=1)
        y_ref[...] = jnp.where(in_use, acc_ref[...] * gate, 0.0).astype(BF16).reshape(group, cap, d)


def _moe_scatter_kernel(item_ref, dest_ref, x_ref, yc_ref, rank_ref, gout_ref, o_ref, acc_ref):
    i, k = pl.program_id(0), pl.program_id(1)
    cap, tm = yc_ref.shape[1], x_ref.shape[0]
    e, j, on = item_ref[0, i, k], item_ref[1, i, k], item_ref[2, i, k] > 0

    @pl.when(k == 0)
    def _():
        acc_ref[...] = jnp.zeros_like(acc_ref)

    @pl.when(on)
    def _():
        slot = jnp.concatenate([_lane_column(rank_ref[...], e)] * (-(-cap // LANES)), axis=1)[:, :cap]
        cid = lax.broadcasted_iota(jnp.int32, (tm, cap), 1) + j * cap
        place = jnp.where(slot == cid.astype(F32), 1.0, 0.0).astype(BF16)
        acc_ref[...] += _dot(place, yc_ref[0])

    @pl.when(k == pl.num_programs(1) - 1)
    def _():
        o_ref[...] = _rms(x_ref[...] + acc_ref[...], gout_ref[...])


def _moe_plan(cnt, cap, n_items, rounds, n_work):
    nt, ne = cnt.shape
    nblk = (cnt + cap - 1) // cap
    ends = jnp.cumsum(nblk, axis=1)
    k = jnp.arange(n_items, dtype=jnp.int32)
    total = ends[:, -1:]
    kk = jnp.minimum(k[None, :], total - 1)
    e = jnp.sum((kk[:, :, None] >= ends[:, None, :]).astype(jnp.int32), axis=2)
    j = kk - jnp.take_along_axis(ends - nblk, e, axis=1)
    on = (k[None, :] < total).astype(jnp.int32)
    earlier = jnp.cumsum(nblk, axis=0) - nblk
    region = rounds * MOE_GROUP
    spare = ne * region + jnp.arange(nt, dtype=jnp.int32)[:, None]
    dest = jnp.where(on > 0, e * region + jnp.take_along_axis(earlier, e, axis=1) + j, spare)
    blocks = jnp.sum(nblk, axis=0)
    passes = jnp.maximum(-(-blocks // MOE_GROUP), 1)
    pass_end = jnp.cumsum(passes)
    w = jnp.minimum(jnp.arange(n_work, dtype=jnp.int32), pass_end[-1] - 1)
    we = jnp.sum((w[:, None] >= pass_end[None, :]).astype(jnp.int32), axis=1)
    work = jnp.stack([we, w - (pass_end - passes)[we], (jnp.arange(n_work) < pass_end[-1]).astype(jnp.int32)])
    as_i32 = lambda t: t.astype(jnp.int32)
    return as_i32(jnp.stack([e, j, on])), as_i32(dest), as_i32(blocks), as_i32(work)


def moe_ffn(x2d, h, comb, rank, rank_t, cnt, wg, wu, wd, g_out, tm, cap):
    t, d = h.shape
    ne, _, ff = wg.shape
    nt, nf, group = t // tm, ff // FF_TILE, MOE_GROUP
    n_items = ne + (TOP_K * tm) // cap
    rounds = -(-(nt + t // cap) // group)
    n_work = ne + (nt * ne + (TOP_K * t) // cap) // group
    items, dest, total, work = _moe_plan(cnt, cap, n_items, rounds, n_work)
    n_blocks = (ne * rounds - (-nt // group)) * group
    once = pl.Buffered(1)

    tile = lambda n, **kw: pl.BlockSpec((tm, n), lambda i, k, it, ds: (i, 0), **kw)
    blk = lambda n: pl.BlockSpec((1, cap, n), lambda i, k, it, ds: (ds[i, k], 0, 0))
    xc, gc = pl.pallas_call(
        _moe_gather_kernel,
        grid_spec=pltpu.PrefetchScalarGridSpec(
            num_scalar_prefetch=2, grid=(nt, n_items),
            in_specs=[tile(d), tile(LANES), pl.BlockSpec((1, 1, tm), lambda i, k, it, ds: (it[0, i, k], 0, i))],
            out_specs=[blk(d), blk(LANES)]),
        out_shape=[jax.ShapeDtypeStruct((n_blocks, cap, d), BF16), jax.ShapeDtypeStruct((n_blocks, cap, LANES), F32)],
        compiler_params=_cparams(("parallel", "arbitrary")),
        name="moe_gather",
    )(items, dest, h, comb, rank_t)

    grp = lambda w, wk: wk[0, w] * rounds + wk[1, w]
    fidx = lambda w, f, wk: jnp.where(wk[2, w] > 0, f, nf - 1)
    rows = lambda n: pl.BlockSpec((group, cap, n), lambda w, f, wk, tot: (grp(w, wk), 0, 0), pipeline_mode=once)
    yc = pl.pallas_call(
        _moe_expert_kernel,
        grid_spec=pltpu.PrefetchScalarGridSpec(
            num_scalar_prefetch=2, grid=(n_work, nf),
            in_specs=[rows(d), rows(LANES),
                      pl.BlockSpec((1, d, FF_TILE), lambda w, f, wk, tot: (wk[0, w], 0, fidx(w, f, wk))),
                      pl.BlockSpec((1, d, FF_TILE), lambda w, f, wk, tot: (wk[0, w], 0, fidx(w, f, wk))),
                      pl.BlockSpec((1, FF_TILE, d), lambda w, f, wk, tot: (wk[0, w], fidx(w, f, wk), 0))],
            out_specs=rows(d),
            scratch_shapes=[pltpu.VMEM((group * cap, d), F32)]),
        out_shape=jax.ShapeDtypeStruct((n_blocks, cap, d), BF16),
        compiler_params=_cparams(("arbitrary", "arbitrary")),
        name="moe_expert",
    )(work, total, xc, gc, wg, wu, wd)

    return pl.pallas_call(
        _moe_scatter_kernel,
        grid_spec=pltpu.PrefetchScalarGridSpec(
            num_scalar_prefetch=2, grid=(nt, n_items),
            in_specs=[tile(d, pipeline_mode=once), blk(d), tile(LANES),
                      pl.BlockSpec((1, d), lambda i, k, it, ds: (0, 0))],
            out_specs=tile(d, pipeline_mode=once),
            scratch_shapes=[pltpu.VMEM((tm, d), F32)]),
        out_shape=jax.ShapeDtypeStruct((t, d), F32),
        compiler_params=_cparams(("parallel", "arbitrary")),
        name="moe_scatter",
    )(items, dest, x2d, yc, rank, g_out.reshape(1, d))


def _block_diag_ones(n, blk):
    i = jnp.arange(n) // blk
    return (i[:, None] == i[None, :]).astype(F32)


def _rope_tables(positions):
    half = ROT_DIM // 2
    inv_freq = jnp.power(ROPE_THETA, -(jnp.arange(half, dtype=F32) * 2.0 / ROT_DIM))
    ang = positions.astype(F32)[..., None] * inv_freq
    cos, sin = jnp.cos(ang), jnp.sin(ang)
    one = jnp.ones(cos.shape[:-1] + (DIFF_QK - ROT_DIM,), F32)
    zero = jnp.zeros_like(one)
    z8 = jnp.zeros_like(sin)
    cs = jnp.concatenate([cos, cos, one], axis=-1)
    s1 = jnp.concatenate([z8, sin, zero], axis=-1)
    s2 = jnp.concatenate([-sin, z8, zero], axis=-1)
    return tuple(jnp.concatenate([t, t], axis=-1) for t in (cs, s1, s2))


def kernel(x, positions, norm_mix, w_in_first, w_in_rest, tshift, decay_bias, decay_up, iclr_bias, iclr_up, gate_up, k_k, k_a, r_k, lnx_w, lnx_b, vres_bias, vres_up, lambda_q, lambda_k, subln_w, pool_mix, pool_scale, w_out, norm_ffn, ffn_gate, ffn_up, ffn_down, router, exp_gate, exp_up, exp_down, norm_out):
    bsz, seq, d = x.shape
    depth = norm_mix.shape[0]
    x2d = x.reshape(bsz * seq, d)
    bd = _block_diag_ones(D_RWKV, HEAD_DIM)
    cs, s1, s2 = (t.reshape(bsz * seq, LANES) for t in _rope_tables(positions))
    v_first = None
    for l in range(depth):
        if l == 0:
            w_in = w_in_first
        else:
            wr = w_in_rest[l - 1]
            w_in = jnp.concatenate([wr[:, :D_SHIFT], wr[:, D_SHIFT + MV_LORA:], wr[:, D_SHIFT:D_SHIFT + MV_LORA],
                                    jnp.zeros((d, LANES - MV_LORA), F32)], axis=1)
        proj, qa, qb, kr, vb = in_proj(x2d, norm_mix[l], w_in.astype(BF16), cs, s1, s2)
        proj, qa, qb, kr, vb = (t.reshape(bsz, seq, -1) for t in (proj, qa, qb, kr, vb))

        zeros = jnp.zeros((DECAY_LORA, 2 * D_RWKV), F32)
        dec = jnp.concatenate([decay_up[l, 0], decay_up[l, 1]], axis=1)
        icl = jnp.concatenate([iclr_up[l, 0], iclr_up[l, 1]], axis=1)
        w_lora = jnp.concatenate([
            jnp.concatenate([dec, zeros], axis=1), jnp.zeros((AAA_LORA, 4 * D_RWKV), F32),
            jnp.zeros((DECAY_LORA, 4 * D_RWKV), F32), jnp.concatenate([zeros, icl], axis=1)], axis=0)
        prm = dict(tshift=tshift[l], w_lora=w_lora,
                   b_lora=jnp.concatenate([decay_bias[l, 0], decay_bias[l, 1], iclr_bias[l, 0], iclr_bias[l, 1]]).reshape(1, -1),
                   gate_up=gate_up[l], k_k=k_k[l].reshape(1, -1), k_a=k_a[l].reshape(1, -1),
                   r_k=r_k[l].reshape(1, -1), bd=bd)
        if l > 0:
            prm["vres_bias"] = vres_bias[l - 1].reshape(1, -1)
            prm["vres_up"] = jnp.concatenate([vres_up[l - 1], jnp.zeros((LANES - MV_LORA, D_RWKV), F32)], axis=0)
        r, v, kk, lw, kd, bb, g, bonus = rwkv_prep(proj, v_first, prm)
        if l == 0:
            v_first = v
        yf, yb = wkv_scan(r, v, kk, lw, kd, bb)

        lam_init = 0.8 - 0.6 * math.exp(-0.3 * l)
        att = diff_attention(qa, qb, kr, vb, lambda_q[l], lambda_k[l], subln_w[l], lam_init)

        mix_bd = jax.scipy.linalg.block_diag(*[pool_mix[l, gi] for gi in range(len(POOL_WINDOWS))])
        pool = pool_mixer(proj, mix_bd, pool_scale[l])

        flat = lambda t: t.reshape(bsz * seq, -1)
        x2d = out_proj(x2d, flat(yf), flat(yb), flat(bonus), flat(g), flat(att), flat(pool), lnx_w[l], lnx_b[l], bd,
                       w_out[l].astype(BF16))
        i = l // 2
        if l % 2 == 0:
            x2d = ffn_dense(x2d, norm_ffn[l], ffn_gate[i].astype(BF16), ffn_up[i].astype(BF16), ffn_down[i].astype(BF16))
        else:
            wr_pad = jnp.concatenate([router[i], jnp.zeros((d, LANES - N_EXPERTS), F32)], axis=1)
            tm = min(2048, bsz * seq)
            cap = 9 * tm // 32
            h, comb, rank, cnt = moe_router(x2d, norm_ffn[l], wr_pad, tm)
            rank_t = rank[:, :N_EXPERTS].T.reshape(N_EXPERTS, 1, -1)
            cnt = cnt[:, 0, :N_EXPERTS].astype(jnp.int32)
            assert l == depth - 1, "the routed mixer's residual add is fused with the final norm"
            x2d = moe_ffn(x2d, h, comb, rank, rank_t, cnt, exp_gate[i].astype(BF16), exp_up[i].astype(BF16),
                          exp_down[i].astype(BF16), norm_out, tm, cap)
    return x2d.reshape(bsz, seq, d)
```

```python
import functools
import math

import jax
import jax.numpy as jnp
from jax import lax
from jax.experimental import pallas as pl
from jax.experimental.pallas import tpu as pltpu

F32 = jnp.float32
BF16 = jnp.bfloat16
HI = lax.Precision.HIGHEST

D_MODEL = 1024
RWKV_HEADS = 4
HEAD_DIM = 64
D_RWKV = RWKV_HEADS * HEAD_DIM
DECAY_LORA = 64
AAA_LORA = 64
MV_LORA = 32
GATE_LORA = 128
D_SHIFT = 3 * D_RWKV + DECAY_LORA + AAA_LORA + GATE_LORA
W_DECAY_SCALE = 0.6065306597126334
GN_EPS = 64e-5
DIFF_HEADS = 4
DIFF_QK = 64
DIFF_V = 128
D_DIFF_QK = DIFF_HEADS * 2 * DIFF_QK
D_DIFF = DIFF_HEADS * DIFF_V
ROT_DIM = DIFF_QK // 4
ROPE_THETA = 500000.0
SUBLN_EPS = 1e-5
POOL_WINDOWS = (2, 4, 8, 16)
D_POOL = 256
D_FF = 3584
N_EXPERTS = 8
NORM_EPS = 1e-6
P_IN = D_SHIFT + 2 * D_DIFF_QK + D_DIFF + D_POOL
COL_Q, COL_K, COL_V, COL_POOL = 1024, 1536, 2048, 2560
REST_POOL, REST_VDOWN = D_SHIFT, D_SHIFT + D_POOL

LANES = 128
HALO = 8
VMEM_LIMIT = 56 * 1024 * 1024

SCAN_CHUNK = 128


def _cparams(sem):
    return pltpu.CompilerParams(dimension_semantics=sem, vmem_limit_bytes=VMEM_LIMIT)


def _sigmoid(x):
    return 1.0 / (1.0 + jnp.exp(-x))


def _dot(a, b, **kw):
    return jnp.dot(a, b, preferred_element_type=F32, **kw)


def _bdot(a, b):
    return jnp.dot(a.astype(BF16), b.astype(BF16), preferred_element_type=F32)


def _rms(x, g):
    return x * lax.rsqrt(jnp.mean(x * x, axis=-1, keepdims=True) + NORM_EPS) * g


def _hi_lo(x):
    hi = x.astype(BF16)
    return hi, (x - hi.astype(F32)).astype(BF16)


def _dot_exact_rhs(a, w):
    hi, lo = _hi_lo(a)
    both = _dot(jnp.concatenate([hi, lo], axis=0), w.astype(BF16))
    return both[:a.shape[0]] + both[a.shape[0]:]


def _dot3(a, w):
    a_hi, a_lo = _hi_lo(a)
    w_hi, w_lo = _hi_lo(w)
    both = _dot(jnp.concatenate([a_hi, a_lo], axis=0), w_hi)
    return both[:a.shape[0]] + both[a.shape[0]:] + _dot(a_hi, w_lo)


def _in_proj_kernel(x_ref, g_ref, w_ref, c_ref, s1_ref, s2_ref, rest_ref, qa_ref, qb_ref, ko_ref, vo_ref):
    h = _rms(x_ref[...], g_ref[...]).astype(BF16)
    p = _dot(h, w_ref[...])
    rest_ref[:, :D_SHIFT] = p[:, :D_SHIFT]
    rest_ref[:, D_SHIFT:] = p[:, COL_POOL:]
    cs, s1, s2 = c_ref[...], s1_ref[...], s2_ref[...]
    first = lax.broadcasted_iota(jnp.int32, cs.shape, 1) < DIFF_QK
    scale = DIFF_QK ** -0.5 * math.log2(math.e)

    def rope(t):
        return t * cs + pltpu.roll(t, ROT_DIM // 2, 1) * s1 + pltpu.roll(t, LANES - ROT_DIM // 2, 1) * s2

    for j in range(D_DIFF_QK // LANES):
        sl = slice(LANES * j, LANES * j + LANES)
        q = rope(p[:, COL_Q + LANES * j:COL_Q + LANES * (j + 1)]) * scale
        qa_ref[:, sl] = jnp.where(first, q, 0.0).astype(BF16)
        qb_ref[:, sl] = jnp.where(first, 0.0, q).astype(BF16)
        ko_ref[:, sl] = rope(p[:, COL_K + LANES * j:COL_K + LANES * (j + 1)]).astype(BF16)
    ones = jnp.ones((p.shape[0], DIFF_V), BF16)
    for hd in range(DIFF_HEADS):
        vo_ref[:, 2 * DIFF_V * hd:2 * DIFF_V * hd + DIFF_V] = p[:, COL_V + DIFF_V * hd:COL_V + DIFF_V * (hd + 1)].astype(BF16)
        vo_ref[:, 2 * DIFF_V * hd + DIFF_V:2 * DIFF_V * (hd + 1)] = ones


def in_proj(x2d, g, w_bf16, cs, s1, s2, tm=512):
    t, d = x2d.shape
    n = w_bf16.shape[1]
    n_rest = n - (COL_POOL - D_SHIFT)
    rows = lambda width: pl.BlockSpec((tm, width), lambda i: (i, 0))
    fixed = lambda shape: pl.BlockSpec(shape, lambda i: (0, 0))
    qkv = jax.ShapeDtypeStruct((t, D_DIFF_QK), BF16)
    return pl.pallas_call(
        _in_proj_kernel,
        grid=(t // tm,),
        in_specs=[rows(d), fixed((1, d)), fixed((d, n)), rows(LANES), rows(LANES), rows(LANES)],
        out_specs=[rows(n_rest), rows(D_DIFF_QK), rows(D_DIFF_QK), rows(D_DIFF_QK), rows(2 * D_DIFF)],
        out_shape=[jax.ShapeDtypeStruct((t, n_rest), F32), qkv, qkv, qkv, jax.ShapeDtypeStruct((t, 2 * D_DIFF), BF16)],
        compiler_params=_cparams(("parallel",)),
        name="in_proj",
    )(x2d, g.reshape(1, d), w_bf16, cs, s1, s2)


def _shifted(u, prev_row, next_row):
    ts = u.shape[0]
    row = lax.broadcasted_iota(jnp.int32, u.shape, 0)
    prev = jnp.where(row == 0, prev_row, pltpu.roll(u, 1, 0))
    nxt = jnp.where(row == ts - 1, next_row, pltpu.roll(u, ts - 1, 0))
    return prev, nxt


def _rwkv_prep_kernel(has_vres, *refs):
    if has_vres:
        (u_ref, up_ref, un_ref, vd_ref, vf_ref, vb_ref, vu_ref, mu_ref, wl_ref, bias_ref, gu_ref,
         kkw_ref, ka_ref, rk_ref, bd_ref,
         r_ref, v_ref, kk_ref, lw_ref, kd_ref, bb_ref, g_ref, bonus_ref) = refs
    else:
        (u_ref, up_ref, un_ref, mu_ref, wl_ref, bias_ref, gu_ref,
         kkw_ref, ka_ref, rk_ref, bd_ref,
         r_ref, v_ref, kk_ref, lw_ref, kd_ref, bb_ref, g_ref, bonus_ref) = refs
    i = pl.program_id(1)
    n = pl.num_programs(1)
    u = u_ref[0]
    prev_row = jnp.where(i > 0, up_ref[0, HALO - 1:HALO, :], 0.0)
    next_row = jnp.where(i < n - 1, un_ref[0, 0:1, :], 0.0)
    prev, nxt = _shifted(u, prev_row, next_row)
    us = u + mu_ref[0:1, :] * (prev - u) + mu_ref[1:2, :] * (nxt - u)
    r = us[:, 0:256]
    k = us[:, 256:512]
    v = us[:, 512:768]
    xwa = us[:, 768:896]
    xg = us[:, 896:1024]
    lhs = jnp.concatenate([jnp.tanh(xwa), xwa], axis=1)
    z = _dot3(lhs, wl_ref[...]) + bias_ref[...]
    if has_vres:
        mix = _sigmoid(vb_ref[...] + _dot3(vd_ref[0], vu_ref[...]))
        v = v + (vf_ref[0] - v) * mix
    g = _dot3(_sigmoid(xg), gu_ref[...])
    bd = bd_ref[...]
    kkr = k * kkw_ref[...]
    ss = _dot_exact_rhs(kkr * kkr, bd)
    kk = kkr * lax.rsqrt(jnp.maximum(ss, 1e-24))
    r_ref[0] = r
    v_ref[0] = v
    kk_ref[0] = kk
    g_ref[0] = g
    bonus = jnp.zeros_like(v)
    for d in range(2):
        lw = -W_DECAY_SCALE * _sigmoid(z[:, 256 * d:256 * d + 256])
        a = _sigmoid(z[:, 512 + 256 * d:768 + 256 * d])
        kd = k * (1.0 + (a - 1.0) * ka_ref[...])
        lw_ref[d, 0] = lw
        kd_ref[d, 0] = kd
        bb_ref[d, 0] = kk * a
        bonus = bonus + _dot_exact_rhs(r * kd * rk_ref[...], bd) * v
    bonus_ref[0] = bonus


def rwkv_prep(proj, v_first, p, ts=512):
    b, s, _ = proj.shape
    has_vres = v_first is not None
    nblk = ts // HALO
    last = s // HALO - 1
    row = lambda shape: pl.BlockSpec(shape, lambda bi, i: (0,) * len(shape))
    in_specs = [
        pl.BlockSpec((1, ts, D_SHIFT), lambda bi, i: (bi, i, 0)),
        pl.BlockSpec((1, HALO, D_SHIFT), lambda bi, i: (bi, jnp.maximum(i * nblk - 1, 0), 0)),
        pl.BlockSpec((1, HALO, D_SHIFT), lambda bi, i: (bi, jnp.minimum((i + 1) * nblk, last), 0)),
    ]
    args = [proj, proj, proj]
    if has_vres:
        in_specs += [
            pl.BlockSpec((1, ts, LANES), lambda bi, i: (bi, i, REST_VDOWN // LANES)),
            pl.BlockSpec((1, ts, D_RWKV), lambda bi, i: (bi, i, 0)),
            row((1, D_RWKV)), row((LANES, D_RWKV)),
        ]
        args += [proj, v_first, p["vres_bias"], p["vres_up"]]
    in_specs += [row((2, D_SHIFT)), row((256, 1024)), row((1, 1024)), row((GATE_LORA, D_RWKV)),
                 row((1, D_RWKV)), row((1, D_RWKV)), row((1, D_RWKV)), row((D_RWKV, D_RWKV))]
    args += [p["tshift"], p["w_lora"], p["b_lora"], p["gate_up"], p["k_k"], p["k_a"], p["r_k"], p["bd"]]
    one = pl.BlockSpec((1, ts, D_RWKV), lambda bi, i: (bi, i, 0))
    two = pl.BlockSpec((2, 1, ts, D_RWKV), lambda bi, i: (0, bi, i, 0))
    s1 = jax.ShapeDtypeStruct((b, s, D_RWKV), F32)
    s2 = jax.ShapeDtypeStruct((2, b, s, D_RWKV), F32)
    return pl.pallas_call(
        functools.partial(_rwkv_prep_kernel, has_vres),
        grid=(b, s // ts),
        in_specs=in_specs,
        out_specs=[one, one, one, two, two, two, one, one],
        out_shape=[s1, s1, s1, s2, s2, s2, s1, s1],
        compiler_params=_cparams(("parallel", "parallel")),
        name="rwkv_prep",
    )(*args)


def _unit_lower_inverses(ms, eye):
    xs = [eye - m for m in ms]
    ps = [_bdot(m, m) for m in ms]
    levels = int(math.log2(ms[0].shape[0])) - 1
    for k in range(levels):
        nxt = [_bdot(p, p) for p in ps] if k < levels - 1 else None
        xs = [x + _bdot(x, p) for x, p in zip(xs, ps)]
        ps = nxt
    return xs


def _split3(x):
    hi = x.astype(BF16)
    r1 = x - hi.astype(F32)
    mid = r1.astype(BF16)
    lo = (r1 - mid.astype(F32)).astype(BF16)
    return hi, mid, lo


def _chunk_terms(rev, r, v, kk, lw, kd, bb):
    c = SCAN_CHUNK
    n = len(r)
    ti = lax.broadcasted_iota(jnp.int32, (c, c), 0)
    tj = lax.broadcasted_iota(jnp.int32, (c, c), 1)
    eye = (ti == tj).astype(F32)
    same_head = (ti // HEAD_DIM) == (tj // HEAD_DIM)
    first = lax.broadcasted_iota(jnp.int32, (c, LANES), 1) < HEAD_DIM
    first2 = jnp.concatenate([first, first], axis=1)
    strict = [(ti - tj if rv else tj - ti) < 0 for rv in rev]
    incl = [(ti - tj if rv else tj - ti) <= 0 for rv in rev]

    cum = []
    for i in range(n):
        cum3 = _dot(jnp.where(incl[i], 1.0, 0.0).astype(BF16), jnp.concatenate(_split3(lw[i]), axis=1))
        cum.append(cum3[:, :LANES] + cum3[:, LANES:2 * LANES] + cum3[:, 2 * LANES:])
    a = [kk[i] * jnp.exp(cum[i] - lw[i]) for i in range(n)]
    rt = [r[i] * jnp.exp(cum[i]) for i in range(n)]
    einv = [jnp.exp(-cm) for cm in cum]
    kt_t = [(kd[i] * einv[i]).T for i in range(n)]
    bt_t = [(bb[i] * einv[i]).T for i in range(n)]
    last = [cum[i][0:1] if rev[i] else cum[i][c - 1:c] for i in range(n)]
    g_all = [jnp.exp(jnp.broadcast_to(x, (c, LANES))).T for x in last]
    vb = [x.astype(BF16) for x in v]
    rhs = [jnp.concatenate([bt_t[i], kt_t[i]], axis=1).astype(BF16) for i in range(n)]
    heads = [(i, hm) for i in range(n) for hm in (first, jnp.logical_not(first))]
    gm = [_dot(jnp.concatenate([jnp.where(hm, a[i], 0.0), jnp.where(hm, rt[i], 0.0)], axis=0).astype(BF16), rhs[i])
          for i, hm in heads]
    t_inv = _unit_lower_inverses([jnp.where(strict[i], g[:c, :c], 0.0) for (i, _), g in zip(heads, gm)], eye)
    mv = [_dot(jnp.where(strict[i], g[:c, c:], 0.0).astype(BF16), vb[i]) for (i, _), g in zip(heads, gm)]
    tam = [_bdot(t, jnp.concatenate([a[i], m], axis=1)) for (i, _), t, m in zip(heads, t_inv, mv)]
    lt = [_bdot(jnp.where(incl[i], g[c:, :c], 0.0), x) for (i, _), g, x in zip(heads, gm, tam)]
    lv = [_dot(jnp.where(incl[i], g[c:, c:], 0.0).astype(BF16), vb[i]) for (i, _), g in zip(heads, gm)]
    tam = [jnp.where(first2, tam[2 * i], tam[2 * i + 1]) for i in range(n)]
    lt = [jnp.where(first2, lt[2 * i], lt[2 * i + 1]) for i in range(n)]
    lv = [jnp.where(first, lv[2 * i], lv[2 * i + 1]) for i in range(n)]
    btx = [_bdot(bt_t[i], tam[i]) for i in range(n)]
    kv = [_dot(kt_t[i].astype(BF16), vb[i]) for i in range(n)]
    phi = [g_all[i] * (eye - jnp.where(same_head, btx[i][:, :LANES], 0.0)) for i in range(n)]
    psi = [g_all[i] * jnp.where(same_head, kv[i] - btx[i][:, LANES:], 0.0) for i in range(n)]
    ra = [rt[i] - lt[i][:, :LANES] for i in range(n)]
    oc = [lv[i] - lt[i][:, LANES:] for i in range(n)]
    return phi, psi, ra, oc


def _wkv_kernel(rf_ref, vf_ref, kkf_ref, lwf_ref, kdf_ref, bbf_ref,
                rb_ref, vb_ref, kkb_ref, lwb_ref, kdb_ref, bbb_ref, yf_ref, yb_ref, h_ref):
    @pl.when(pl.program_id(1) == 0)
    def _():
        h_ref[...] = jnp.zeros_like(h_ref)

    c = SCAN_CHUNK
    nsub = rf_ref.shape[1] // c
    pairs = (slice(0, LANES), slice(LANES, 2 * LANES))
    fwd = ((rf_ref, vf_ref, kkf_ref), (lwf_ref, kdf_ref, bbf_ref), yf_ref)
    bwd = ((rb_ref, vb_ref, kkb_ref), (lwb_ref, kdb_ref, bbb_ref), yb_ref)
    probs = [(False, pl.ds(c * j, c), sl) + fwd for sl in pairs for j in range(nsub)]
    probs += [(True, pl.ds(c * j, c), sl) + bwd for sl in pairs for j in reversed(range(nsub))]
    rev = [pr[0] for pr in probs]
    r, v, kk = ([pr[3][j][0, pr[1], pr[2]] for pr in probs] for j in range(3))
    lw, kd, bb = ([pr[4][j][0, 0, pr[1], pr[2]] for pr in probs] for j in range(3))
    phi, psi, ra, oc = _chunk_terms(rev, r, v, kk, lw, kd, bb)
    h = [h_ref[i] for i in range(len(probs) // nsub)]
    for j in range(nsub):
        res = []
        for i in range(len(h)):
            h_hi = h[i].astype(BF16)
            h_lo = (h[i] - h_hi.astype(F32)).astype(BF16)
            lhs = jnp.concatenate([phi[i * nsub + j], ra[i * nsub + j]], axis=0).astype(BF16)
            res.append(_dot(jnp.concatenate([lhs, lhs], axis=1), jnp.concatenate([h_hi, h_lo], axis=0)))
        for i in range(len(h)):
            pr = probs[i * nsub + j]
            h[i] = res[i][:c] + psi[i * nsub + j]
            pr[5][0, pr[1], pr[2]] = res[i][c:] + oc[i * nsub + j]
    for i in range(len(h)):
        h_ref[i] = h[i]


def wkv_scan(r, v, kk, lw, kd, bb, chunks_per_step=2):
    b, s, _ = r.shape
    c = SCAN_CHUNK * chunks_per_step
    nc = s // c
    fwd = pl.BlockSpec((1, c, D_RWKV), lambda bi, ci: (bi, ci, 0))
    bwd = pl.BlockSpec((1, c, D_RWKV), lambda bi, ci: (bi, nc - 1 - ci, 0))
    fwd_d = pl.BlockSpec((1, 1, c, D_RWKV), lambda bi, ci: (0, bi, ci, 0))
    bwd_d = pl.BlockSpec((1, 1, c, D_RWKV), lambda bi, ci: (1, bi, nc - 1 - ci, 0))
    out = jax.ShapeDtypeStruct((b, s, D_RWKV), F32)
    return pl.pallas_call(
        _wkv_kernel,
        grid=(b, nc),
        in_specs=[fwd, fwd, fwd, fwd_d, fwd_d, fwd_d, bwd, bwd, bwd, bwd_d, bwd_d, bwd_d],
        out_specs=[fwd, bwd],
        out_shape=[out, out],
        scratch_shapes=[pltpu.VMEM((4, LANES, LANES), F32)],
        compiler_params=_cparams(("parallel", "arbitrary")),
        name="wkv_scan",
    )(r, v, kk, lw, kd, bb, r, v, kk, lw, kd, bb)


def _attn_kernel(lam_init, q_rows, qa_ref, qb_ref, k_ref, v_ref, lq_ref, lk_ref, sw_ref, o_ref):
    kt = k_ref[0]
    vt = v_ref[0]
    e = jnp.exp(jnp.sum(lq_ref[...] * lk_ref[...], axis=1, keepdims=True))
    lam = e[0:1] - e[1:2] + lam_init
    blocks = [pl.ds(r0, q_rows) for r0 in range(0, qa_ref.shape[1], q_rows)]
    n = len(blocks)
    s, p = [None] * n, [None] * n
    for step in range(n + 2):
        if step < n:
            q2 = jnp.concatenate([qa_ref[0, blocks[step], :], qb_ref[0, blocks[step], :]], axis=0)
            s[step] = lax.dot_general(q2, kt, (((1,), (1,)), ((), ())), preferred_element_type=F32)
        if 0 <= step - 1 < n:
            p[step - 1] = jnp.exp2(s[step - 1] - jnp.max(s[step - 1], axis=1, keepdims=True)).astype(BF16)
            s[step - 1] = None
        if 0 <= step - 2 < n:
            pv = _dot(p[step - 2], vt)
            ratio = pv[:, :DIFF_V] / pv[:, DIFF_V:]
            o = ratio[:q_rows] - lam * ratio[q_rows:]
            o = o * lax.rsqrt(jnp.mean(o * o, axis=1, keepdims=True) + SUBLN_EPS) * sw_ref[...]
            o_ref[0, blocks[step - 2], :] = o * (1.0 - lam_init)
            p[step - 2] = None


def diff_attention(qa, qb, k, v, lq, lk, sw, lam_init, tq=1024, q_rows=128):
    b, s, _ = qa.shape
    tq = min(tq, s)
    qspec = pl.BlockSpec((1, tq, LANES), lambda bi, h, i: (bi, i, h))
    kspec = pl.BlockSpec((1, s, LANES), lambda bi, h, i: (bi, 0, h))
    small = lambda shape: pl.BlockSpec(shape, lambda bi, h, i: (0, 0))
    return pl.pallas_call(
        functools.partial(_attn_kernel, lam_init, min(q_rows, tq)),
        grid=(b, DIFF_HEADS, s // tq),
        in_specs=[qspec, qspec, kspec, pl.BlockSpec((1, s, 2 * DIFF_V), lambda bi, h, i: (bi, 0, h)),
                  small((2, DIFF_QK)), small((2, DIFF_QK)), small((1, DIFF_V))],
        out_specs=qspec,
        out_shape=jax.ShapeDtypeStruct((b, s, D_DIFF), F32),
        compiler_params=_cparams(("parallel", "parallel", "parallel")),
        name="diff_attention",
    )(qa, qb, k, v, lq, lk, sw.reshape(1, DIFF_V))


def _pool_kernel(seq, u_ref, up_ref, un_ref, mix_ref, scale_ref, o_ref):
    i = pl.program_id(1)
    n = pl.num_programs(1)
    u = u_ref[0]
    ts = u.shape[0]
    prev = jnp.where(i > 0, up_ref[0], 0.0)
    nxt = jnp.where(i < n - 1, un_ref[0], 0.0)
    ext = jnp.concatenate([prev, u, nxt], axis=0)
    rows = ext.shape[0]
    back = lambda t, sft: pltpu.roll(t, sft, 0)
    fwd = lambda t, sft: pltpu.roll(t, rows - sft, 0)
    w2 = ext + back(ext, 1)
    w4 = back(w2, 1) + fwd(w2, 1)
    w8 = back(w4, 2) + fwd(w4, 2)
    w16 = back(w8, 4) + fwd(w8, 4)
    lane = lax.broadcasted_iota(jnp.int32, (ts, D_POOL), 1)
    t = lax.broadcasted_iota(jnp.int32, (ts, D_POOL), 0) + i * ts
    grp = lane // HEAD_DIM
    sel = lambda vals: jnp.where(grp == 0, vals[0], jnp.where(grp == 1, vals[1], jnp.where(grp == 2, vals[2], vals[3])))
    wsum = sel([w[HALO:HALO + ts] for w in (w2, w4, w8, w16)])
    half = sel([wd // 2 for wd in POOL_WINDOWS])
    width = sel(list(POOL_WINDOWS))
    cnt = jnp.minimum(t + width - half, seq) - jnp.maximum(t - half, 0)
    pooled = wsum / cnt.astype(F32) - u
    o_ref[0] = _dot(pooled, mix_ref[...], precision=HI) * scale_ref[...]


def pool_mixer(proj, mix_bd, scale, ts=512):
    b, s, _ = proj.shape
    nblk = ts // HALO
    last = s // HALO - 1
    cb = REST_POOL // D_POOL
    hb = REST_POOL // D_POOL
    return pl.pallas_call(
        functools.partial(_pool_kernel, s),
        grid=(b, s // ts),
        in_specs=[
            pl.BlockSpec((1, ts, D_POOL), lambda bi, i: (bi, i, cb)),
            pl.BlockSpec((1, HALO, D_POOL), lambda bi, i: (bi, jnp.maximum(i * nblk - 1, 0), hb)),
            pl.BlockSpec((1, HALO, D_POOL), lambda bi, i: (bi, jnp.minimum((i + 1) * nblk, last), hb)),
            pl.BlockSpec((D_POOL, D_POOL), lambda bi, i: (0, 0)),
            pl.BlockSpec((1, D_POOL), lambda bi, i: (0, 0)),
        ],
        out_specs=pl.BlockSpec((1, ts, D_POOL), lambda bi, i: (bi, i, 0)),
        out_shape=jax.ShapeDtypeStruct((b, s, D_POOL), F32),
        compiler_params=_cparams(("parallel", "parallel")),
        name="pool_mixer",
    )(proj, proj, proj, mix_bd, scale.reshape(1, D_POOL))


def _out_proj_kernel(x_ref, yf_ref, yb_ref, bonus_ref, g_ref, att_ref, pool_ref, lw_ref, lb_ref, bd_ref, w_ref, o_ref):
    y = yf_ref[...] + yb_ref[...]
    bd = bd_ref[...]
    mu = _dot_exact_rhs(y, bd) * (1.0 / HEAD_DIM)
    yc = y - mu
    var = _dot_exact_rhs(yc * yc, bd) * (1.0 / HEAD_DIM)
    ya = (yc * lax.rsqrt(var + GN_EPS) * lw_ref[...] + lb_ref[...] + bonus_ref[...]) * g_ref[...]
    cat = jnp.concatenate([ya, att_ref[...], pool_ref[...]], axis=1).astype(BF16)
    o_ref[...] = x_ref[...] + _dot(cat, w_ref[...])


def out_proj(x2d, yf, yb, bonus, g, att, pool, lnx_w, lnx_b, bd, w_bf16, tm=512):
    t, d = x2d.shape
    rows = lambda n: pl.BlockSpec((tm, n), lambda i: (i, 0))
    fixed = lambda shape: pl.BlockSpec(shape, lambda i: (0, 0))
    return pl.pallas_call(
        _out_proj_kernel,
        grid=(t // tm,),
        in_specs=[rows(d),
                  rows(D_RWKV), rows(D_RWKV), rows(D_RWKV), rows(D_RWKV), rows(D_DIFF), rows(D_POOL),
                  fixed((1, D_RWKV)), fixed((1, D_RWKV)), fixed((D_RWKV, D_RWKV)), fixed((d, d))],
        out_specs=rows(d),
        out_shape=jax.ShapeDtypeStruct((t, d), F32),
        compiler_params=_cparams(("parallel",)),
        name="out_proj",
    )(x2d, yf, yb, bonus, g, att, pool, lnx_w.reshape(1, -1), lnx_b.reshape(1, -1), bd, w_bf16)


FF_TILE = 512
MOE_FF_TILE = 1792
MOE_SUB_TILE = 256
TOP_K = 2


def _swiglu_tile(x, wg, wu, wd):
    gate = _dot(x, wg)
    return _dot((gate * _sigmoid(gate) * _dot(x, wu)).astype(BF16), wd)


def _ffn_kernel(x_ref, g_ref, wg_ref, wu_ref, wd_ref, o_ref, h_ref, acc_ref):
    f = pl.program_id(1)

    @pl.when(f == 0)
    def _():
        h_ref[...] = _rms(x_ref[...], g_ref[...]).astype(BF16)
        acc_ref[...] = jnp.zeros_like(acc_ref)

    acc_ref[...] += _swiglu_tile(h_ref[...], wg_ref[...], wu_ref[...], wd_ref[...])

    @pl.when(f == pl.num_programs(1) - 1)
    def _():
        o_ref[...] = x_ref[...] + acc_ref[...]


def ffn_dense(x2d, g, wg, wu, wd, tm=1024):
    t, d = x2d.shape
    ff = wg.shape[1]
    return pl.pallas_call(
        _ffn_kernel,
        grid=(t // tm, ff // FF_TILE),
        in_specs=[pl.BlockSpec((tm, d), lambda i, f: (i, 0)),
                  pl.BlockSpec((1, d), lambda i, f: (0, 0)),
                  pl.BlockSpec((d, FF_TILE), lambda i, f: (0, f)),
                  pl.BlockSpec((d, FF_TILE), lambda i, f: (0, f)),
                  pl.BlockSpec((FF_TILE, d), lambda i, f: (f, 0))],
        out_specs=pl.BlockSpec((tm, d), lambda i, f: (i, 0)),
        out_shape=jax.ShapeDtypeStruct((t, d), F32),
        scratch_shapes=[pltpu.VMEM((tm, d), BF16), pltpu.VMEM((tm, d), F32)],
        compiler_params=_cparams(("parallel", "arbitrary")),
        name="ffn_dense",
    )(x2d, g.reshape(1, d), wg, wu, wd)


def _router_kernel(x_ref, g_ref, wr_ref, h_ref, comb_ref, rank_ref, cnt_ref):
    h = _rms(x_ref[...], g_ref[...])
    h_ref[...] = h.astype(BF16)
    logits = _dot(h, wr_ref[...], precision=HI)
    lane = lax.broadcasted_iota(jnp.int32, logits.shape, 1)
    logits = jnp.where(lane < N_EXPERTS, logits, -jnp.inf)
    v1 = jnp.max(logits, axis=1, keepdims=True)
    i1 = jnp.min(jnp.where(logits == v1, lane, LANES), axis=1, keepdims=True)
    rest = jnp.where(lane == i1, -jnp.inf, logits)
    v2 = jnp.max(rest, axis=1, keepdims=True)
    i2 = jnp.min(jnp.where(rest == v2, lane, LANES), axis=1, keepdims=True)
    e2 = jnp.exp(v2 - v1)
    g1 = 1.0 / (1.0 + e2)
    comb = jnp.where(lane == i1, g1, jnp.where(lane == i2, e2 * g1, 0.0))
    comb_ref[...] = comb
    tm = comb.shape[0]
    routed = comb > 0.0
    mask = jnp.where(routed, 1.0, 0.0).astype(BF16)
    earlier = lax.broadcasted_iota(jnp.int32, (tm, tm), 1) < lax.broadcasted_iota(jnp.int32, (tm, tm), 0)
    rank = _dot(jnp.where(earlier, 1.0, 0.0).astype(BF16), mask)
    rank_ref[...] = jnp.where(routed, rank, -1.0)
    cnt_ref[0] = _dot(jnp.ones((HALO, tm), BF16), mask)


def moe_router(x2d, g, wr_pad, tm):
    t, d = x2d.shape
    tok = pl.BlockSpec((tm, LANES), lambda i: (i, 0))
    return pl.pallas_call(
        _router_kernel,
        grid=(t // tm,),
        in_specs=[pl.BlockSpec((tm, d), lambda i: (i, 0)),
                  pl.BlockSpec((1, d), lambda i: (0, 0)),
                  pl.BlockSpec((d, LANES), lambda i: (0, 0))],
        out_specs=[pl.BlockSpec((tm, d), lambda i: (i, 0)), tok, tok,
                   pl.BlockSpec((1, HALO, LANES), lambda i: (i, 0, 0))],
        out_shape=[jax.ShapeDtypeStruct((t, d), BF16), jax.ShapeDtypeStruct((t, LANES), F32),
                   jax.ShapeDtypeStruct((t, LANES), F32), jax.ShapeDtypeStruct((t // tm, HALO, LANES), F32)],
        compiler_params=_cparams(("parallel",)),
        name="moe_router",
    )(x2d, g.reshape(1, d), wr_pad)


def _moe_kernel(item_ref, h_ref, comb_ref, rank_ref, rank_t_ref, wg_ref, wu_ref, wd_ref, o_ref,
                xe_ref, ye_ref, gc_ref):
    i, k, f = pl.program_id(0), pl.program_id(1), pl.program_id(2)
    tm = h_ref.shape[0]
    cap = xe_ref.shape[0]
    e, j, on = item_ref[0, i, k], item_ref[1, i, k], item_ref[2, i, k] > 0
    first, last = f == 0, f == pl.num_programs(2) - 1

    @pl.when(jnp.logical_and(k == 0, first))
    def _():
        o_ref[...] = jnp.zeros_like(o_ref)

    def lane_column(x):
        onehot = jnp.where(lax.broadcasted_iota(jnp.int32, (LANES, LANES), 0) == e, 1.0, 0.0).astype(BF16)
        hi = x.astype(BF16)
        lo = (x - hi.astype(F32)).astype(BF16)
        both = _dot(jnp.concatenate([hi, lo], axis=0), onehot)
        return both[:x.shape[0]] + both[x.shape[0]:]

    @pl.when(jnp.logical_and(on, first))
    def _():
        gate = lane_column(comb_ref[...])
        g_hi = gate.astype(BF16)
        g_lo = (gate - g_hi.astype(F32)).astype(BF16)
        rid = lax.broadcasted_iota(jnp.int32, (cap, tm), 0) + j * cap
        pick = jnp.where(rank_t_ref[0] == rid.astype(F32), 1.0, 0.0).astype(BF16)
        xe_ref[...] = _dot(pick, h_ref[...]).astype(BF16)
        gc2 = _dot(pick, jnp.concatenate([g_hi, g_lo], axis=1))
        gc_ref[...] = gc2[:, :LANES] + gc2[:, LANES:]

    @pl.when(on)
    def _():
        xe = xe_ref[...]
        y = None
        for c0 in range(0, wg_ref.shape[2], MOE_SUB_TILE):
            sl = slice(c0, c0 + MOE_SUB_TILE)
            part = _swiglu_tile(xe, wg_ref[0, :, sl], wu_ref[0, :, sl], wd_ref[0, sl, :])
            y = part if y is None else y + part
        ye_ref[...] = jnp.where(first, y, ye_ref[...] + y)

    @pl.when(jnp.logical_and(on, last))
    def _():
        slot = jnp.concatenate([lane_column(rank_ref[...])] * (-(-cap // LANES)), axis=1)[:, :cap]
        cid = lax.broadcasted_iota(jnp.int32, (tm, cap), 1) + j * cap
        place = jnp.where(slot == cid.astype(F32), 1.0, 0.0).astype(BF16)
        ys = ye_ref[...] * jnp.concatenate([gc_ref[...]] * (ye_ref.shape[1] // LANES), axis=1)
        o_ref[...] += _dot(place, ys.astype(BF16))


def _moe_items(cnt, cap, n_items):
    nblk = (cnt + cap - 1) // cap
    ends = jnp.cumsum(nblk, axis=1)
    k = jnp.arange(n_items, dtype=jnp.int32)
    total = ends[:, -1:]
    kk = jnp.minimum(k[None, :], total - 1)
    e = jnp.sum((kk[:, :, None] >= ends[:, None, :]).astype(jnp.int32), axis=2)
    start = jnp.take_along_axis(ends - nblk, e, axis=1)
    return jnp.stack([e, kk - start, (k[None, :] < total).astype(jnp.int32)]).astype(jnp.int32)


def moe_ffn(h, comb, rank, rank_t, cnt, wg, wu, wd, tm, cap):
    t, d = h.shape
    ne, _, ff = wg.shape
    tf = MOE_FF_TILE
    nf = ff // tf
    n_items = ne + (TOP_K * tm) // cap
    items = _moe_items(cnt, cap, n_items)
    tok = lambda n: pl.BlockSpec((tm, n), lambda i, k, f, it: (i, 0), pipeline_mode=pl.Buffered(1))
    fidx = lambda k, f, it, i: jnp.where(it[2, i, k] > 0, f, nf - 1)
    grid_spec = pltpu.PrefetchScalarGridSpec(
        num_scalar_prefetch=1,
        grid=(t // tm, n_items, nf),
        in_specs=[tok(d), tok(LANES), tok(LANES),
                  pl.BlockSpec((1, 1, tm), lambda i, k, f, it: (it[0, i, k], 0, i)),
                  pl.BlockSpec((1, d, tf), lambda i, k, f, it: (it[0, i, k], 0, fidx(k, f, it, i))),
                  pl.BlockSpec((1, d, tf), lambda i, k, f, it: (it[0, i, k], 0, fidx(k, f, it, i))),
                  pl.BlockSpec((1, tf, d), lambda i, k, f, it: (it[0, i, k], fidx(k, f, it, i), 0))],
        out_specs=tok(d),
        scratch_shapes=[pltpu.VMEM((cap, d), BF16), pltpu.VMEM((cap, d), F32), pltpu.VMEM((cap, LANES), F32)],
    )
    return pl.pallas_call(
        _moe_kernel,
        grid_spec=grid_spec,
        out_shape=jax.ShapeDtypeStruct((t, d), F32),
        compiler_params=_cparams(("parallel", "arbitrary", "arbitrary")),
        name="moe_ffn",
    )(items, h, comb, rank, rank_t, wg, wu, wd)


def _add_norm_kernel(x_ref, y_ref, g_ref, o_ref):
    o_ref[...] = _rms(x_ref[...] + y_ref[...], g_ref[...])


def add_norm(x2d, y2d, g, tm=1024):
    t, d = x2d.shape
    row = pl.BlockSpec((tm, d), lambda i: (i, 0))
    return pl.pallas_call(
        _add_norm_kernel,
        grid=(t // tm,),
        in_specs=[row, row, pl.BlockSpec((1, d), lambda i: (0, 0))],
        out_specs=row,
        out_shape=jax.ShapeDtypeStruct((t, d), F32),
        compiler_params=_cparams(("parallel",)),
        name="add_norm",
    )(x2d, y2d, g.reshape(1, d))


def _block_diag_ones(n, blk):
    i = jnp.arange(n) // blk
    return (i[:, None] == i[None, :]).astype(F32)


def _rope_tables(positions):
    half = ROT_DIM // 2
    inv_freq = jnp.power(ROPE_THETA, -(jnp.arange(half, dtype=F32) * 2.0 / ROT_DIM))
    ang = positions.astype(F32)[..., None] * inv_freq
    cos, sin = jnp.cos(ang), jnp.sin(ang)
    one = jnp.ones(cos.shape[:-1] + (DIFF_QK - ROT_DIM,), F32)
    zero = jnp.zeros_like(one)
    z8 = jnp.zeros_like(sin)
    cs = jnp.concatenate([cos, cos, one], axis=-1)
    s1 = jnp.concatenate([z8, sin, zero], axis=-1)
    s2 = jnp.concatenate([-sin, z8, zero], axis=-1)
    return tuple(jnp.concatenate([t, t], axis=-1) for t in (cs, s1, s2))


def kernel(x, positions, norm_mix, w_in_first, w_in_rest, tshift, decay_bias, decay_up, iclr_bias, iclr_up, gate_up, k_k, k_a, r_k, lnx_w, lnx_b, vres_bias, vres_up, lambda_q, lambda_k, subln_w, pool_mix, pool_scale, w_out, norm_ffn, ffn_gate, ffn_up, ffn_down, router, exp_gate, exp_up, exp_down, norm_out):
    bsz, seq, d = x.shape
    depth = norm_mix.shape[0]
    x2d = x.reshape(bsz * seq, d)
    bd = _block_diag_ones(D_RWKV, HEAD_DIM)
    cs, s1, s2 = (t.reshape(bsz * seq, LANES) for t in _rope_tables(positions))
    v_first = None
    for l in range(depth):
        if l == 0:
            w_in = w_in_first
        else:
            wr = w_in_rest[l - 1]
            w_in = jnp.concatenate([wr[:, :D_SHIFT], wr[:, D_SHIFT + MV_LORA:], wr[:, D_SHIFT:D_SHIFT + MV_LORA],
                                    jnp.zeros((d, LANES - MV_LORA), F32)], axis=1)
        proj, qa, qb, kr, vb = in_proj(x2d, norm_mix[l], w_in.astype(BF16), cs, s1, s2)
        proj, qa, qb, kr, vb = (t.reshape(bsz, seq, -1) for t in (proj, qa, qb, kr, vb))

        zeros = jnp.zeros((DECAY_LORA, 2 * D_RWKV), F32)
        dec = jnp.concatenate([decay_up[l, 0], decay_up[l, 1]], axis=1)
        icl = jnp.concatenate([iclr_up[l, 0], iclr_up[l, 1]], axis=1)
        w_lora = jnp.concatenate([
            jnp.concatenate([dec, zeros], axis=1), jnp.zeros((AAA_LORA, 4 * D_RWKV), F32),
            jnp.zeros((DECAY_LORA, 4 * D_RWKV), F32), jnp.concatenate([zeros, icl], axis=1)], axis=0)
        prm = dict(tshift=tshift[l], w_lora=w_lora,
                   b_lora=jnp.concatenate([decay_bias[l, 0], decay_bias[l, 1], iclr_bias[l, 0], iclr_bias[l, 1]]).reshape(1, -1),
                   gate_up=gate_up[l], k_k=k_k[l].reshape(1, -1), k_a=k_a[l].reshape(1, -1),
                   r_k=r_k[l].reshape(1, -1), bd=bd)
        if l > 0:
            prm["vres_bias"] = vres_bias[l - 1].reshape(1, -1)
            prm["vres_up"] = jnp.concatenate([vres_up[l - 1], jnp.zeros((LANES - MV_LORA, D_RWKV), F32)], axis=0)
        r, v, kk, lw, kd, bb, g, bonus = rwkv_prep(proj, v_first, prm)
        if l == 0:
            v_first = v
        yf, yb = wkv_scan(r, v, kk, lw, kd, bb)

        lam_init = 0.8 - 0.6 * math.exp(-0.3 * l)
        att = diff_attention(qa, qb, kr, vb, lambda_q[l], lambda_k[l], subln_w[l], lam_init)

        mix_bd = jax.scipy.linalg.block_diag(*[pool_mix[l, gi] for gi in range(len(POOL_WINDOWS))])
        pool = pool_mixer(proj, mix_bd, pool_scale[l])

        flat = lambda t: t.reshape(bsz * seq, -1)
        x2d = out_proj(x2d, flat(yf), flat(yb), flat(bonus), flat(g), flat(att), flat(pool), lnx_w[l], lnx_b[l], bd,
                       w_out[l].astype(BF16))
        i = l // 2
        if l % 2 == 0:
            x2d = ffn_dense(x2d, norm_ffn[l], ffn_gate[i].astype(BF16), ffn_up[i].astype(BF16), ffn_down[i].astype(BF16))
        else:
            wr_pad = jnp.concatenate([router[i], jnp.zeros((d, LANES - N_EXPERTS), F32)], axis=1)
            tm = min(2048, bsz * seq)
            cap = 9 * tm // 32
            h, comb, rank, cnt = moe_router(x2d, norm_ffn[l], wr_pad, tm)
            rank_t = rank[:, :N_EXPERTS].T.reshape(N_EXPERTS, 1, -1)
            cnt = cnt[:, 0, :N_EXPERTS].astype(jnp.int32)
            y = moe_ffn(h, comb, rank, rank_t, cnt, exp_gate[i].astype(BF16), exp_up[i].astype(BF16),
                        exp_down[i].astype(BF16), tm, cap)
            assert l == depth - 1, "the routed mixer's residual add is fused with the final norm"
            x2d = add_norm(x2d, y, norm_out)
    return x2d.reshape(bsz, seq, d)
```
